```python
import math
import jax, jax.numpy as jnp
from jax import lax
import numpy as np

D_MODEL = 1024
BATCH = 16
SEQ = 4096
DEPTH = 4

GRID_W = 64
CTX_LEN = 256
HEAD_DIM = 64
ATTN_WIDTH = D_MODEL // 2
N_HEADS = ATTN_WIDTH // HEAD_DIM
KV_HEADS = N_HEADS // 4
Q_PER_KV = N_HEADS // KV_HEADS
KV_WIDTH = KV_HEADS * HEAD_DIM
WINDOW = 128
ATTN_BLOCK = WINDOW
ATTN_SCALE = HEAD_DIM ** -0.5
ROPE_BASE = 10000.0
ROPE_PAIRS = HEAD_DIM // 4
SSM_WIDTH = D_MODEL // 4
SSM_GROUP = 16
SSM_GROUPS = SSM_WIDTH // SSM_GROUP
SSM_STATE = 64
LOG_DT_MIN = math.log(1e-3)
LOG_DT_MAX = math.log(1e-1)
CONV_WIDTH = D_MODEL // 4
CONV_K = 3
MIX_WIDTH = ATTN_WIDTH + SSM_WIDTH + CONV_WIDTH
Q_END = ATTN_WIDTH
K_END = Q_END + KV_WIDTH
V_END = K_END + KV_WIDTH
U_END = V_END + SSM_WIDTH
GB_END = U_END + CONV_WIDTH
GC_END = GB_END + CONV_WIDTH
IN_COLS = GC_END + CONV_WIDTH
SPLITS = (Q_END, K_END, V_END, U_END, GB_END, GC_END)
D_FF = ((8 * D_MODEL // 3 + 127) // 128) * 128
MACARON = 0.5
N_MOD = 9
EPS = 1e-6
NEG_INF = -1e30

kernel_name = 'hybrid_headgroup_diffusion_trunk'


def _rms_norm(x, g):
    xf = x.astype(jnp.float32)
    xf = xf * lax.rsqrt(jnp.mean(xf * xf, axis=-1, keepdims=True) + EPS)
    return (xf * g.astype(jnp.float32)).astype(x.dtype)


def _modulate(x, g, shift, scale):
    return _rms_norm(x, g) * (1 + scale) + shift


def _gated_post(y, g, gate):
    return gate * _rms_norm(y, g)


def _swiglu(h, wg, wu, wd):
    return (jax.nn.silu(h @ wg) * (h @ wu)) @ wd


def _ffn_sublayer(x, m, s, g_pre, g_post, wg, wu, wd):
    h = _modulate(x, g_pre, m[:, 3 * s], m[:, 3 * s + 1])
    return _gated_post(_swiglu(h, wg, wu, wd), g_post, m[:, 3 * s + 2])


def _axial_rope_tables(length):
    rows = length // GRID_W
    row = jnp.repeat(jnp.arange(rows, dtype=jnp.float32), GRID_W)
    col = jnp.tile(jnp.arange(GRID_W, dtype=jnp.float32), rows)
    inv_freq = ROPE_BASE ** (-jnp.arange(ROPE_PAIRS, dtype=jnp.float32) / ROPE_PAIRS)
    ang = jnp.stack([row[:, None] * inv_freq, col[:, None] * inv_freq], axis=1)
    return jnp.cos(ang), jnp.sin(ang)


def _apply_rope(t, cos, sin):
    tf = t.astype(jnp.float32).reshape(t.shape[:-1] + (2, 2, ROPE_PAIRS))
    t1, t2 = tf[..., 0, :], tf[..., 1, :]
    cs, sn = cos[None, :, None], sin[None, :, None]
    out = jnp.stack([t1 * cs - t2 * sn, t2 * cs + t1 * sn], axis=-2)
    return out.reshape(t.shape).astype(t.dtype)


def _band(t, nb):
    b_, length = t.shape[:2]
    tp = jnp.pad(t, ((0, 0), (ATTN_BLOCK, ATTN_BLOCK), (0, 0), (0, 0)))
    return jnp.concatenate(
        [tp[:, o * ATTN_BLOCK:o * ATTN_BLOCK + length].reshape((b_, nb, ATTN_BLOCK) + t.shape[2:]) for o in range(3)],
        axis=2)


def _window_attention(q, k, v, kc, vc, sink):
    b_, length = q.shape[:2]
    nb = length // ATTN_BLOCK
    n_loc, n_ctx = 3 * ATTN_BLOCK, kc.shape[1]
    qb = (q * ATTN_SCALE).reshape(b_, nb, ATTN_BLOCK, KV_HEADS, Q_PER_KV, HEAD_DIM)
    kb, vb = _band(k, nb), _band(v, nb)
    s_loc = jnp.einsum('bnqkgd,bnjkd->bnkgqj', qb, kb).astype(jnp.float32)
    s_ctx = jnp.einsum('bnqkgd,bckd->bnkgqc', qb, kc).astype(jnp.float32)
    qi = jnp.arange(ATTN_BLOCK)[:, None]
    kj = jnp.arange(n_loc)[None, :]
    kpos = jnp.arange(nb)[:, None, None] * ATTN_BLOCK + kj[None] - ATTN_BLOCK
    valid = (jnp.abs(kj - ATTN_BLOCK - qi) <= WINDOW)[None] & (kpos >= 0) & (kpos < length)
    s_loc = jnp.where(valid[None, :, None, None], s_loc, NEG_INF)
    sink_l = jnp.broadcast_to(sink.astype(jnp.float32).reshape(1, 1, KV_HEADS, Q_PER_KV, 1, 1), s_loc.shape[:-1] + (1,))
    p = jax.nn.softmax(jnp.concatenate([s_loc, s_ctx, sink_l], axis=-1), axis=-1).astype(v.dtype)
    out = (jnp.einsum('bnkgqj,bnjkd->bnqkgd', p[..., :n_loc], vb)
           + jnp.einsum('bnkgqc,bckd->bnqkgd', p[..., n_loc:n_loc + n_ctx], vc))
    return out.reshape(b_, length, ATTN_WIDTH)


def _context_attention(qc, kc, vc, sink):
    b_, n_ctx = qc.shape[:2]
    qg = (qc * ATTN_SCALE).reshape(b_, n_ctx, KV_HEADS, Q_PER_KV, HEAD_DIM)
    s = jnp.einsum('bqkgd,bckd->bkgqc', qg, kc).astype(jnp.float32)
    sink_c = jnp.broadcast_to(sink.astype(jnp.float32).reshape(1, KV_HEADS, Q_PER_KV, 1, 1), s.shape[:-1] + (1,))
    p = jax.nn.softmax(jnp.concatenate([s, sink_c], axis=-1), axis=-1)[..., :n_ctx].astype(vc.dtype)
    return jnp.einsum('bkgqc,bckd->bqkgd', p, vc).reshape(b_, n_ctx, ATTN_WIDTH)


def _complex_combine(left, right):
    a1r, a1i, b1r, b1i = left
    a2r, a2i, b2r, b2i = right
    return (a2r * a1r - a2i * a1i,
            a2r * a1i + a2i * a1r,
            a2r * b1r - a2i * b1i + b2r,
            a2r * b1i + a2i * b1r + b2i)


def _zoh(lam_re, lam_im, log_step, b_re, b_im):
    lr, li = lam_re.astype(jnp.float32), lam_im.astype(jnp.float32)
    dt = jnp.exp(log_step.astype(jnp.float32))[:, None]
    mag = jnp.exp(lr * dt)
    ar, ai = mag * jnp.cos(li * dt), mag * jnp.sin(li * dt)
    den = lr * lr + li * li
    gr = ((ar - 1) * lr + ai * li) / den
    gi = (ai * lr - (ar - 1) * li) / den
    br, bi = b_re.astype(jnp.float32), b_im.astype(jnp.float32)
    bbr = gr[..., None] * br - gi[..., None] * bi
    bbi = gr[..., None] * bi + gi[..., None] * br
    return ar, ai, bbr, bbi


def _diag_scan(u, ar, ai, bbr, bbi, init, reverse):
    length = u.shape[1]
    bu_r = jnp.einsum('gph,blgh->blgp', bbr, u)
    bu_i = jnp.einsum('gph,blgh->blgp', bbi, u)
    if init is not None:
        ir, ii = init
        pos = length - 1 if reverse else 0
        bu_r = bu_r.at[:, pos].add(ar * ir - ai * ii)
        bu_i = bu_i.at[:, pos].add(ar * ii + ai * ir)
    a_r = jnp.broadcast_to(ar, (1, length) + ar.shape)
    a_i = jnp.broadcast_to(ai, (1, length) + ai.shape)
    _, _, s_r, s_i = lax.associative_scan(_complex_combine, (a_r, a_i, bu_r, bu_i), reverse=reverse, axis=1)
    return s_r, s_i


def _readout(cr, ci, sr, si):
    return jnp.einsum('ghp,blgp->blgh', cr, sr) - jnp.einsum('ghp,blgp->blgh', ci, si)


def _glu(y, w, b):
    g = jax.nn.gelu(y)
    return g * jax.nn.sigmoid(g @ w + b)


def _s5_mixer(u, uc, lam_re, lam_im, log_step, b_re, b_im, c_re, c_im, d_skip, w_glu, b_glu, ctx_out):
    b_, length, _ = u.shape
    n_ctx = uc.shape[1]
    uf = u.astype(jnp.float32).reshape(b_, length, SSM_GROUPS, SSM_GROUP)
    ucf = uc.astype(jnp.float32).reshape(b_, n_ctx, SSM_GROUPS, SSM_GROUP)
    d = d_skip.astype(jnp.float32).reshape(SSM_GROUPS, SSM_GROUP)
    y = d * uf
    yc = d * ucf if ctx_out else None
    for dirn in range(2):
        rev = dirn == 1
        ar, ai, bbr, bbi = _zoh(lam_re[dirn], lam_im[dirn], log_step[dirn], b_re[dirn], b_im[dirn])
        cr, ci = c_re[dirn].astype(jnp.float32), c_im[dirn].astype(jnp.float32)
        sc_r, sc_i = _diag_scan(ucf, ar, ai, bbr, bbi, None, rev)
        end = 0 if rev else -1
        s_r, s_i = _diag_scan(uf, ar, ai, bbr, bbi, (sc_r[:, end], sc_i[:, end]), rev)
        y = y + _readout(cr, ci, s_r, s_i)
        if ctx_out:
            yc = yc + _readout(cr, ci, sc_r, sc_i)
    out = _glu(y.reshape(b_, length, SSM_WIDTH).astype(u.dtype), w_glu, b_glu)
    out_c = _glu(yc.reshape(b_, n_ctx, SSM_WIDTH).astype(uc.dtype), w_glu, b_glu) if ctx_out else None
    return out, out_c


def _short_conv(t, w):
    tp = jnp.pad(t, ((0, 0), (1, 1), (0, 0)))
    return tp[:, :-2] * w[0] + tp[:, 1:-1] * w[1] + tp[:, 2:] * w[2]


def _token_mixer(h, hc, cos, sin, w_in, w_out, sink, lam_re, lam_im, log_step, b_re, b_im, c_re, c_im,
                 d_skip, w_glu, b_glu, conv_w, ctx_out):
    b_, length, _ = h.shape
    n_ctx = hc.shape[1]
    q, k, v, u, gb, gc, z = jnp.split(h @ w_in, SPLITS, axis=-1)
    q = _apply_rope(q.reshape(b_, length, N_HEADS, HEAD_DIM), cos, sin)
    k = _apply_rope(k.reshape(b_, length, KV_HEADS, HEAD_DIM), cos, sin)
    v = v.reshape(b_, length, KV_HEADS, HEAD_DIM)
    kc, vc, uc = jnp.split(hc @ w_in[:, Q_END:U_END], (KV_WIDTH, 2 * KV_WIDTH), axis=-1)
    kc = kc.reshape(b_, n_ctx, KV_HEADS, HEAD_DIM)
    vc = vc.reshape(b_, n_ctx, KV_HEADS, HEAD_DIM)
    attn = _window_attention(q, k, v, kc, vc, sink)
    ssm, ssm_c = _s5_mixer(u, uc, lam_re, lam_im, log_step, b_re, b_im, c_re, c_im, d_skip, w_glu, b_glu, ctx_out)
    conv = gb * _short_conv(gc * z, conv_w)
    y = jnp.concatenate([attn, ssm, conv], axis=-1) @ w_out
    if not ctx_out:
        return y, None
    qc = (hc @ w_in[:, :Q_END]).reshape(b_, n_ctx, N_HEADS, HEAD_DIM)
    gbc, gcc, zc = jnp.split(hc @ w_in[:, U_END:], (CONV_WIDTH, 2 * CONV_WIDTH), axis=-1)
    yc = jnp.concatenate([_context_attention(qc, kc, vc, sink), ssm_c, gbc * _short_conv(gcc * zc, conv_w)], axis=-1) @ w_out
    return y, yc


def setup_inputs(seed: int = 0) -> dict:
    key = jax.random.key(seed)
    ks = jax.random.split(key, 26)
    f32 = jnp.float32

    def nrm(k, shape, s):
        return jax.random.normal(k, shape, f32) * s

    sg = (DEPTH, 2, SSM_GROUPS, SSM_STATE)
    n_idx = jnp.arange(SSM_STATE, dtype=f32)
    return {
        'x': nrm(ks[0], (BATCH, SEQ, D_MODEL), 1.0),
        'c': nrm(ks[1], (BATCH, D_MODEL), 1.0),
        'ctx': nrm(ks[2], (BATCH, CTX_LEN, D_MODEL), 1.0),
        'c_ctx': nrm(ks[3], (D_MODEL,), 1.0),
        'w_ada': nrm(ks[4], (DEPTH, D_MODEL, N_MOD * D_MODEL), 0.5 * D_MODEL ** -0.5),
        'b_ada': nrm(ks[5], (DEPTH, N_MOD * D_MODEL), 0.02),
        'norm_pre': 1.0 + nrm(ks[6], (DEPTH, 3, D_MODEL), 0.02),
        'norm_post': 1.0 + nrm(ks[7], (DEPTH, 3, D_MODEL), 0.02),
        'ffn_w_gate': nrm(ks[8], (DEPTH, 2, D_MODEL, D_FF), D_MODEL ** -0.5),
        'ffn_w_up': nrm(ks[9], (DEPTH, 2, D_MODEL, D_FF), D_MODEL ** -0.5),
        'ffn_w_down': nrm(ks[10], (DEPTH, 2, D_FF, D_MODEL), D_FF ** -0.5),
        'w_in': nrm(ks[11], (DEPTH, D_MODEL, IN_COLS), D_MODEL ** -0.5),
        'w_out': nrm(ks[12], (DEPTH, MIX_WIDTH, D_MODEL), MIX_WIDTH ** -0.5),
        'attn_sink': nrm(ks[13], (DEPTH, N_HEADS), 1.0),
        'ssm_lambda_re': -0.5 + nrm(ks[14], sg, 0.01),
        'ssm_lambda_im': jnp.pi * n_idx + nrm(ks[15], sg, 0.01),
        'ssm_log_step': jax.random.uniform(ks[16], (DEPTH, 2, SSM_GROUPS), f32, LOG_DT_MIN, LOG_DT_MAX),
        'ssm_b_re': nrm(ks[17], sg + (SSM_GROUP,), (2 * SSM_GROUP) ** -0.5),
        'ssm_b_im': nrm(ks[18], sg + (SSM_GROUP,), (2 * SSM_GROUP) ** -0.5),
        'ssm_c_re': nrm(ks[19], (DEPTH, 2, SSM_GROUPS, SSM_GROUP, SSM_STATE), SSM_STATE ** -0.5),
        'ssm_c_im': nrm(ks[20], (DEPTH, 2, SSM_GROUPS, SSM_GROUP, SSM_STATE), SSM_STATE ** -0.5),
        'ssm_d': nrm(ks[21], (DEPTH, SSM_WIDTH), 1.0),
        'ssm_w_glu': nrm(ks[22], (DEPTH, SSM_WIDTH, SSM_WIDTH), SSM_WIDTH ** -0.5),
        'ssm_b_glu': nrm(ks[23], (DEPTH, SSM_WIDTH), 0.02),
        'conv_w': nrm(ks[24], (DEPTH, CONV_K, CONV_WIDTH), CONV_K ** -0.5),
    }


def reference(x, c, ctx, c_ctx, w_ada, b_ada, norm_pre, norm_post, ffn_w_gate, ffn_w_up, ffn_w_down,
              w_in, w_out, attn_sink, ssm_lambda_re, ssm_lambda_im, ssm_log_step, ssm_b_re, ssm_b_im,
              ssm_c_re, ssm_c_im, ssm_d, ssm_w_glu, ssm_b_glu, conv_w):
    b_, length, _ = x.shape
    cos, sin = _axial_rope_tables(length)
    silu_c = jax.nn.silu(c)
    silu_cc = jax.nn.silu(c_ctx)
    xc = ctx
    for l in range(DEPTH):
        last = l == DEPTH - 1
        m = (silu_c @ w_ada[l] + b_ada[l]).reshape(b_, N_MOD, 1, D_MODEL)
        mc = (silu_cc @ w_ada[l] + b_ada[l]).reshape(1, N_MOD, 1, D_MODEL)
        f0 = (norm_pre[l, 0], norm_post[l, 0], ffn_w_gate[l, 0], ffn_w_up[l, 0], ffn_w_down[l, 0])
        x = x + MACARON * _ffn_sublayer(x, m, 0, *f0)
        xc = xc + MACARON * _ffn_sublayer(xc, mc, 0, *f0)
        h = _modulate(x, norm_pre[l, 1], m[:, 3], m[:, 4])
        hc = _modulate(xc, norm_pre[l, 1], mc[:, 3], mc[:, 4])
        y, yc = _token_mixer(h, hc, cos, sin, w_in[l], w_out[l], attn_sink[l],
                             ssm_lambda_re[l], ssm_lambda_im[l], ssm_log_step[l], ssm_b_re[l], ssm_b_im[l],
                             ssm_c_re[l], ssm_c_im[l], ssm_d[l], ssm_w_glu[l], ssm_b_glu[l], conv_w[l],
                             not last)
        x = x + _gated_post(y, norm_post[l, 1], m[:, 5])
        f1 = (norm_pre[l, 2], norm_post[l, 2], ffn_w_gate[l, 1], ffn_w_up[l, 1], ffn_w_down[l, 1])
        x = x + MACARON * _ffn_sublayer(x, m, 2, *f1)
        if not last:
            xc = xc + _gated_post(yc, norm_post[l, 1], mc[:, 5])
            xc = xc + MACARON * _ffn_sublayer(xc, mc, 2, *f1)
    return x
```

```python
import functools
import math

import jax
import jax.numpy as jnp
import numpy as np
from jax import lax
from jax.experimental import pallas as pl
from jax.experimental.pallas import tpu as pltpu

D_MODEL = 1024
BATCH = 16
SEQ = 4096
DEPTH = 4
GRID_W = 64
CTX_LEN = 256
HEAD_DIM = 64
ATTN_WIDTH = D_MODEL // 2
N_HEADS = ATTN_WIDTH // HEAD_DIM
KV_HEADS = N_HEADS // 4
Q_PER_KV = N_HEADS // KV_HEADS
KV_WIDTH = KV_HEADS * HEAD_DIM
WINDOW = 128
ATTN_SCALE = HEAD_DIM ** -0.5
ROPE_BASE = 10000.0
ROPE_PAIRS = HEAD_DIM // 4
SSM_WIDTH = D_MODEL // 4
SSM_GROUP = 16
SSM_GROUPS = SSM_WIDTH // SSM_GROUP
SSM_STATE = 64
CONV_WIDTH = D_MODEL // 4
IN_COLS = ATTN_WIDTH + 2 * KV_WIDTH + SSM_WIDTH + 3 * CONV_WIDTH
D_FF = ((8 * D_MODEL // 3 + 127) // 128) * 128
MACARON = 0.5
N_MOD = 9
EPS = 1e-6
NEG_INF = -1e30

TOKENS = CTX_LEN + SEQ
TILE = 256
N_TILES = TOKENS // TILE
MOD_ROWS = 24
CTX_ROW = BATCH
N_LOCAL = TILE + 2 * WINDOW
STATE_W = SSM_GROUPS * SSM_STATE
SCAN_T = 32
SCAN_ROWS = SCAN_T * BATCH
SCAN_LANES = 512
N_SCAN = TOKENS // SCAN_T
N_SCAN_CTX = CTX_LEN // SCAN_T
VMEM_LIMIT = 48 * 1024 * 1024

_F32 = jnp.float32
_BF16 = jnp.bfloat16


def _rms(x, g):
    return (x * lax.rsqrt(jnp.mean(x * x, axis=-1, keepdims=True) + EPS)) * g


def _modulated(x, g, shift, scale):
    return _rms(x, g) * (1 + scale) + shift


def _ada_kernel(cc_ref, w_ref, b_ref, o_ref):
    cc = cc_ref[...]
    s = cc * jax.nn.sigmoid(cc)
    o_ref[...] = jnp.dot(s, w_ref[...], precision=lax.Precision.HIGHEST,
                         preferred_element_type=_F32) + b_ref[...]


def _ada_table(cc, w_ada, b_ada):
    out = pl.pallas_call(
        _ada_kernel,
        grid=(DEPTH, N_MOD),
        in_specs=[
            pl.BlockSpec((MOD_ROWS, D_MODEL), lambda l, j: (0, 0)),
            pl.BlockSpec((None, D_MODEL, D_MODEL), lambda l, j: (l, 0, j)),
            pl.BlockSpec((None, None, 1, D_MODEL), lambda l, j: (l, j, 0, 0)),
        ],
        out_specs=pl.BlockSpec((None, None, MOD_ROWS, D_MODEL), lambda l, j: (l, j, 0, 0)),
        out_shape=jax.ShapeDtypeStruct((DEPTH, N_MOD, MOD_ROWS, D_MODEL), _F32),
        name="ada_table",
    )(cc, w_ada, b_ada.reshape(DEPTH, N_MOD, 1, D_MODEL))
    return out.reshape(DEPTH, N_MOD, MOD_ROWS, 1, D_MODEL)


def _mod_spec(layer, j, off):
    def index(b, i):
        return (layer, j, jnp.where(i + off == 0, CTX_ROW, b), 0, 0)
    return pl.BlockSpec((None, None, None, 1, D_MODEL), index)


def _const_spec(shape, index):
    return pl.BlockSpec(shape, lambda b, i: index, pipeline_mode=pl.Buffered(1))


def _tile_spec(width, off=0):
    return pl.BlockSpec((None, TILE, width), lambda b, i: (b, i + off, 0))


_TOKEN_PARAMS = pltpu.CompilerParams(
    dimension_semantics=("parallel", "parallel"), vmem_limit_bytes=VMEM_LIMIT)


def _ffn_kernel(x_ref, sh_ref, sc_ref, gt_ref, gpre_ref, gpost_ref, wg_ref, wu_ref, wd_ref, o_ref):
    x = x_ref[...]
    h = _modulated(x, gpre_ref[...], sh_ref[...], sc_ref[...]).astype(_BF16)
    a = jnp.dot(h, wg_ref[...], preferred_element_type=_F32)
    b = jnp.dot(h, wu_ref[...], preferred_element_type=_F32)
    s = ((a * jax.nn.sigmoid(a)) * b).astype(_BF16)
    y = jnp.dot(s, wd_ref[...], preferred_element_type=_F32)
    o_ref[...] = x + MACARON * (gt_ref[...] * _rms(y, gpost_ref[...]))


def _ffn(x_all, mods, norm_pre, norm_post, wg, wu, wd, layer, sub, which):
    n_tiles = x_all.shape[1] // TILE
    off = N_TILES - n_tiles
    return pl.pallas_call(
        _ffn_kernel,
        grid=(BATCH, n_tiles),
        in_specs=[
            _tile_spec(D_MODEL),
            _mod_spec(layer, 3 * sub, off),
            _mod_spec(layer, 3 * sub + 1, off),
            _mod_spec(layer, 3 * sub + 2, off),
            _const_spec((None, None, 1, D_MODEL), (layer, sub, 0, 0)),
            _const_spec((None, None, 1, D_MODEL), (layer, sub, 0, 0)),
            _const_spec((None, None, D_MODEL, D_FF), (layer, which, 0, 0)),
            _const_spec((None, None, D_MODEL, D_FF), (layer, which, 0, 0)),
            _const_spec((None, None, D_FF, D_MODEL), (layer, which, 0, 0)),
        ],
        out_specs=_tile_spec(D_MODEL),
        out_shape=jax.ShapeDtypeStruct((BATCH, n_tiles * TILE, D_MODEL), _F32),
        compiler_params=_TOKEN_PARAMS,
        name="ffn",
    )(x_all, mods, mods, mods, norm_pre, norm_post, wg, wu, wd)


def _proj_kernel(x_ref, sh_ref, sc_ref, gpre_ref, w_ref, cos_ref, sin_ref,
                 q_ref, k_ref, v_ref, u_ref, gb_ref, p_ref):
    h = _modulated(x_ref[...], gpre_ref[...], sh_ref[...], sc_ref[...]).astype(_BF16)
    pr = jnp.dot(h, w_ref[...], preferred_element_type=_F32)
    cosv = cos_ref[...]
    sinv = sin_ref[...]
    lane = lax.broadcasted_iota(jnp.int32, (TILE, 128), 1)
    first_half = (lane % (2 * ROPE_PAIRS)) < ROPE_PAIRS

    def rope(t):
        partner = jnp.where(first_half, pltpu.roll(t, 128 - ROPE_PAIRS, 1), pltpu.roll(t, ROPE_PAIRS, 1))
        return t * cosv + partner * sinv

    for j in range(ATTN_WIDTH // 128):
        q_ref[:, j * 128:(j + 1) * 128] = (rope(pr[:, j * 128:(j + 1) * 128]) * ATTN_SCALE).astype(_BF16)
    c = ATTN_WIDTH
    k_ref[...] = rope(pr[:, c:c + KV_WIDTH]).astype(_BF16)
    c += KV_WIDTH
    v_ref[...] = pr[:, c:c + KV_WIDTH].astype(_BF16)
    c += KV_WIDTH
    u_ref[...] = pr[:, c:c + SSM_WIDTH]
    c += SSM_WIDTH
    gb_ref[...] = pr[:, c:c + CONV_WIDTH]
    c += CONV_WIDTH
    p_ref[...] = pr[:, c:c + CONV_WIDTH] * pr[:, c + CONV_WIDTH:c + 2 * CONV_WIDTH]


def _proj(x_all, mods, norm_pre, w_in, cos_t, sin_t, layer):
    tok = lambda w, dt: jax.ShapeDtypeStruct((BATCH, TOKENS, w), dt)
    return pl.pallas_call(
        _proj_kernel,
        grid=(BATCH, N_TILES),
        in_specs=[
            _tile_spec(D_MODEL),
            _mod_spec(layer, 3, 0),
            _mod_spec(layer, 4, 0),
            _const_spec((None, None, 1, D_MODEL), (layer, 1, 0, 0)),
            _const_spec((None, D_MODEL, IN_COLS), (layer, 0, 0)),
            pl.BlockSpec((TILE, 128), lambda b, i: (i, 0)),
            pl.BlockSpec((TILE, 128), lambda b, i: (i, 0)),
        ],
        out_specs=[
            _tile_spec(ATTN_WIDTH), _tile_spec(KV_WIDTH), _tile_spec(KV_WIDTH),
            _tile_spec(SSM_WIDTH), _tile_spec(CONV_WIDTH), _tile_spec(CONV_WIDTH),
        ],
        out_shape=[
            tok(ATTN_WIDTH, _BF16), tok(KV_WIDTH, _BF16), tok(KV_WIDTH, _BF16),
            tok(SSM_WIDTH, _F32), tok(CONV_WIDTH, _F32), tok(CONV_WIDTH, _F32),
        ],
        compiler_params=_TOKEN_PARAMS,
        name="in_proj",
    )(x_all, mods, mods, norm_pre, w_in, cos_t, sin_t)


def _dot_nt(a, b):
    return lax.dot_general(a, b, (((1,), (1,)), ((), ())), preferred_element_type=_F32)


def _attn_kernel(sink_ref, q_ref, k_ref, v_ref, o_ref):
    i = pl.program_id(1)
    q0 = i * TILE
    start = pl.multiple_of(jnp.clip(q0 - WINDOW, CTX_LEN, TOKENS - N_LOCAL), WINDOW)
    k_loc = k_ref[pl.ds(start, N_LOCAL), :]
    v_loc = v_ref[pl.ds(start, N_LOCAL), :]
    k_ctx = k_ref[0:CTX_LEN, :]
    v_ctx = v_ref[0:CTX_LEN, :]
    qi = q0 + lax.broadcasted_iota(jnp.int32, (TILE, N_LOCAL), 0)
    kj = start + lax.broadcasted_iota(jnp.int32, (TILE, N_LOCAL), 1)
    valid = (jnp.abs(kj - qi) <= WINDOW) & (i >= 1)
    bias = jnp.where(valid, 0.0, NEG_INF).astype(_F32)
    q = q_ref[...]
    for h in range(N_HEADS):
        lo = (h // Q_PER_KV) * HEAD_DIM
        qh = q[:, h * HEAD_DIM:(h + 1) * HEAD_DIM]
        s_loc = _dot_nt(qh, k_loc[:, lo:lo + HEAD_DIM]) + bias
        s_ctx = _dot_nt(qh, k_ctx[:, lo:lo + HEAD_DIM])
        sink = sink_ref[h]
        m = jnp.maximum(jnp.maximum(jnp.max(s_loc, axis=1, keepdims=True),
                                    jnp.max(s_ctx, axis=1, keepdims=True)), sink)
        p_loc = jnp.exp(s_loc - m)
        p_ctx = jnp.exp(s_ctx - m)
        den = (jnp.sum(p_loc, axis=1, keepdims=True) + jnp.sum(p_ctx, axis=1, keepdims=True)
               + jnp.exp(sink - m))
        o = (jnp.dot(p_loc.astype(_BF16), v_loc[:, lo:lo + HEAD_DIM], preferred_element_type=_F32)
             + jnp.dot(p_ctx.astype(_BF16), v_ctx[:, lo:lo + HEAD_DIM], preferred_element_type=_F32))
        o_ref[:, h * HEAD_DIM:(h + 1) * HEAD_DIM] = (o / den).astype(_BF16)


def _attention(sink, q, k, v):
    seq_spec = pl.BlockSpec((None, TOKENS, KV_WIDTH), lambda b, i: (b, 0, 0))
    return pl.pallas_call(
        _attn_kernel,
        grid=(BATCH, N_TILES),
        in_specs=[
            pl.BlockSpec(memory_space=pltpu.SMEM),
            _tile_spec(ATTN_WIDTH), seq_spec, seq_spec,
        ],
        out_specs=_tile_spec(ATTN_WIDTH),
        out_shape=jax.ShapeDtypeStruct((BATCH, TOKENS, ATTN_WIDTH), _BF16),
        compiler_params=_TOKEN_PARAMS,
        name="window_attn",
    )(sink, q, k, v)


def _s5_kernel(uf_ref, ub_ref, perm_ref, perm_t_ref, bmat_ref, cmat_ref, a_ref, yf_ref, yb_ref,
               buf_ref, state_ref):
    @pl.when(pl.program_id(0) == 0)
    def _():
        state_ref[...] = jnp.zeros_like(state_ref)

    for d, (u_ref, y_ref) in enumerate(((uf_ref, yf_ref), (ub_ref, yb_ref))):
        u = u_ref[...].reshape(SCAN_ROWS, SSM_WIDTH).astype(_BF16)
        u_tb = jnp.dot(perm_ref[...], u, preferred_element_type=_F32).astype(_BF16)
        buf_ref[...] = jnp.dot(u_tb, bmat_ref[d], preferred_element_type=_F32)
        for c in range(STATE_W // SCAN_LANES):
            re = slice(c * SCAN_LANES, (c + 1) * SCAN_LANES)
            im = slice(STATE_W + c * SCAN_LANES, STATE_W + (c + 1) * SCAN_LANES)
            a_r = jnp.broadcast_to(a_ref[d, 0:1, re], (BATCH, SCAN_LANES))
            a_i = jnp.broadcast_to(a_ref[d, 1:2, re], (BATCH, SCAN_LANES))

            def body(t, carry, re=re, im=im, a_r=a_r, a_i=a_i, d=d):
                s_r, s_i = carry
                tt = t if d == 0 else SCAN_T - 1 - t
                rows = pl.ds(pl.multiple_of(tt * BATCH, BATCH), BATCH)
                n_r = a_r * s_r - a_i * s_i + buf_ref[rows, re]
                n_i = a_r * s_i + a_i * s_r + buf_ref[rows, im]
                buf_ref[rows, re] = n_r
                buf_ref[rows, im] = n_i
                return n_r, n_i

            s_r, s_i = lax.fori_loop(0, SCAN_T, body, (state_ref[d, :, re], state_ref[d, :, im]), unroll=4)
            state_ref[d, :, re] = s_r
            state_ref[d, :, im] = s_i
        y_tb = jnp.dot(buf_ref[...].astype(_BF16), cmat_ref[d], preferred_element_type=_F32)
        y = jnp.dot(perm_t_ref[...], y_tb.astype(_BF16), preferred_element_type=_F32)
        y_ref[...] = y.astype(_BF16).reshape(BATCH, SCAN_T, SSM_WIDTH)


def _backward_chunk(i):
    return jnp.where(i < N_SCAN_CTX, N_SCAN_CTX - 1 - i, N_SCAN - 1 - i + N_SCAN_CTX)


def _scan_permutation():
    r = np.arange(SCAN_ROWS)
    perm = np.zeros((SCAN_ROWS, SCAN_ROWS), np.float32)
    perm[r, (r % BATCH) * SCAN_T + r // BATCH] = 1.0
    return jnp.asarray(perm, _BF16), jnp.asarray(perm.T, _BF16)


def _s5_scan(u, bmat, cmat, a_bar):
    chunk = (BATCH, SCAN_T, SSM_WIDTH)
    fwd = pl.BlockSpec(chunk, lambda i: (0, i, 0))
    bwd = pl.BlockSpec(chunk, lambda i: (0, _backward_chunk(i), 0))
    whole = lambda a: pl.BlockSpec(a.shape, lambda i: (0,) * a.ndim, pipeline_mode=pl.Buffered(1))
    perm, perm_t = _scan_permutation()
    y_shape = jax.ShapeDtypeStruct((BATCH, TOKENS, SSM_WIDTH), _BF16)
    return pl.pallas_call(
        _s5_kernel,
        grid=(N_SCAN,),
        in_specs=[fwd, bwd, whole(perm), whole(perm_t), whole(bmat), whole(cmat), whole(a_bar)],
        out_specs=[fwd, bwd],
        out_shape=[y_shape, y_shape],
        scratch_shapes=[
            pltpu.VMEM((SCAN_ROWS, 2 * STATE_W), _F32),
            pltpu.VMEM((2, BATCH, 2 * STATE_W), _F32),
        ],
        compiler_params=pltpu.CompilerParams(
            dimension_semantics=("arbitrary",), vmem_limit_bytes=VMEM_LIMIT),
        name="s5_scan",
    )(u, u, perm, perm_t, bmat, cmat, a_bar)


def _out_kernel(x_ref, gt_ref, gpost_ref, attn_ref, u_ref, yf_ref, yb_ref, gb_ref, p_ref,
                pprev_ref, pnext_ref, d_ref, wglu_ref, bglu_ref, cw_ref, wout_ref, o_ref, *, off):
    i = pl.program_id(1) + off
    y = d_ref[...] * u_ref[...] + yf_ref[...] + yb_ref[...]
    g = jax.nn.gelu(y)
    ssm = g * jax.nn.sigmoid(
        jnp.dot(g.astype(_BF16), wglu_ref[...], preferred_element_type=_F32) + bglu_ref[...])

    p = p_ref[...]
    row = lax.broadcasted_iota(jnp.int32, (TILE, CONV_WIDTH), 0)
    has_prev = i >= 2
    has_next = (i >= 1) & (i <= N_TILES - 2)
    prev_row = jnp.where(has_prev, pprev_ref[7:8, :], 0.0)
    next_row = jnp.where(has_next, pnext_ref[0:1, :], 0.0)
    p_before = jnp.where(row == 0, prev_row, pltpu.roll(p, 1, 0))
    p_after = jnp.where(row == TILE - 1, next_row, pltpu.roll(p, TILE - 1, 0))
    cw = cw_ref[...]
    conv = gb_ref[...] * (p_before * cw[0:1, :] + p * cw[1:2, :] + p_after * cw[2:3, :])

    mix = jnp.concatenate([attn_ref[...], ssm.astype(_BF16), conv.astype(_BF16)], axis=1)
    yv = jnp.dot(mix, wout_ref[...], preferred_element_type=_F32)
    o_ref[...] = x_ref[...] + gt_ref[...] * _rms(yv, gpost_ref[...])


def _mix_out(x_all, mods, norm_post, attn, u_ssm, yf, yb, gb, p, ssm_d, w_glu, b_glu, conv_w, w_out,
             layer, off):
    n_tiles = N_TILES - off
    rows8 = TILE // 8
    prev_spec = pl.BlockSpec((None, 8, CONV_WIDTH),
                             lambda b, i: (b, jnp.maximum((i + off) * rows8 - 1, 0), 0))
    next_spec = pl.BlockSpec((None, 8, CONV_WIDTH),
                             lambda b, i: (b, jnp.minimum((i + off + 1) * rows8, TOKENS // 8 - 1), 0))
    return pl.pallas_call(
        functools.partial(_out_kernel, off=off),
        grid=(BATCH, n_tiles),
        in_specs=[
            _tile_spec(D_MODEL, off),
            _mod_spec(layer, 5, off),
            _const_spec((None, None, 1, D_MODEL), (layer, 1, 0, 0)),
            _tile_spec(ATTN_WIDTH, off),
            _tile_spec(SSM_WIDTH, off), _tile_spec(SSM_WIDTH, off), _tile_spec(SSM_WIDTH, off),
            _tile_spec(CONV_WIDTH, off), _tile_spec(CONV_WIDTH, off),
            prev_spec, next_spec,
            _const_spec((None, 1, SSM_WIDTH), (layer, 0, 0)),
            _const_spec((None, SSM_WIDTH, SSM_WIDTH), (layer, 0, 0)),
            _const_spec((None, 1, SSM_WIDTH), (layer, 0, 0)),
            _const_spec((None, 3, CONV_WIDTH), (layer, 0, 0)),
            _const_spec((None, D_MODEL, D_MODEL), (layer, 0, 0)),
        ],
        out_specs=_tile_spec(D_MODEL),
        out_shape=jax.ShapeDtypeStruct((BATCH, n_tiles * TILE, D_MODEL), _F32),
        compiler_params=_TOKEN_PARAMS,
        name="mix_out",
    )(x_all, mods, norm_post, attn, u_ssm, yf, yb, gb, p, p, p, ssm_d, w_glu, b_glu, conv_w, w_out)


def _rope_tables():
    pos = jnp.arange(SEQ)
    row = (pos // GRID_W).astype(_F32)
    col = (pos % GRID_W).astype(_F32)
    inv_freq = ROPE_BASE ** (-jnp.arange(ROPE_PAIRS, dtype=_F32) / ROPE_PAIRS)
    ang = jnp.stack([row[:, None] * inv_freq, col[:, None] * inv_freq], axis=1)
    cos, sin = jnp.cos(ang), jnp.sin(ang)
    cos_h = jnp.concatenate([cos, cos], axis=-1).reshape(SEQ, HEAD_DIM)
    sin_h = jnp.concatenate([-sin, sin], axis=-1).reshape(SEQ, HEAD_DIM)
    cos_t = jnp.concatenate([jnp.ones((CTX_LEN, HEAD_DIM), _F32), cos_h], axis=0)
    sin_t = jnp.concatenate([jnp.zeros((CTX_LEN, HEAD_DIM), _F32), sin_h], axis=0)
    return jnp.tile(cos_t, (1, 128 // HEAD_DIM)), jnp.tile(sin_t, (1, 128 // HEAD_DIM))


def _zoh(lam_re, lam_im, log_step, b_re, b_im):
    dt = jnp.exp(log_step)[..., None]
    mag = jnp.exp(lam_re * dt)
    ar, ai = mag * jnp.cos(lam_im * dt), mag * jnp.sin(lam_im * dt)
    den = lam_re * lam_re + lam_im * lam_im
    gr = ((ar - 1) * lam_re + ai * lam_im) / den
    gi = (ai * lam_re - (ar - 1) * lam_im) / den
    bbr = gr[..., None] * b_re - gi[..., None] * b_im
    bbi = gr[..., None] * b_im + gi[..., None] * b_re
    return ar, ai, bbr, bbi


def _s5_matrices(lam_re, lam_im, log_step, b_re, b_im, c_re, c_im):
    ar, ai, bbr, bbi = _zoh(lam_re, lam_im, log_step, b_re, b_im)
    eye = jnp.eye(SSM_GROUPS, dtype=_F32)
    lead = (DEPTH, 2)
    blk_in = lambda m: jnp.einsum('ldgph,gk->ldghkp', m, eye).reshape(lead + (SSM_WIDTH, STATE_W))
    blk_out = lambda m: jnp.einsum('ldghp,gk->ldkpgh', m, eye).reshape(lead + (STATE_W, SSM_WIDTH))
    bmat = jnp.concatenate([blk_in(bbr), blk_in(bbi)], axis=-1).astype(_BF16)
    cmat = jnp.concatenate([blk_out(c_re), -blk_out(c_im)], axis=-2).astype(_BF16)
    a_bar = jnp.stack([ar.reshape(lead + (STATE_W,)), ai.reshape(lead + (STATE_W,))], axis=2)
    return bmat, cmat, a_bar


def kernel(x, c, ctx, c_ctx, w_ada, b_ada, norm_pre, norm_post, ffn_w_gate, ffn_w_up, ffn_w_down,
           w_in, w_out, attn_sink, ssm_lambda_re, ssm_lambda_im, ssm_log_step, ssm_b_re, ssm_b_im,
           ssm_c_re, ssm_c_im, ssm_d, ssm_w_glu, ssm_b_glu, conv_w):
    cc = jnp.concatenate([c, c_ctx[None, :], jnp.zeros((MOD_ROWS - BATCH - 1, D_MODEL), _F32)], axis=0)
    mods = _ada_table(cc, w_ada, b_ada)
    cos_t, sin_t = _rope_tables()
    bmat, cmat, a_bar = _s5_matrices(ssm_lambda_re, ssm_lambda_im, ssm_log_step,
                                     ssm_b_re, ssm_b_im, ssm_c_re, ssm_c_im)
    npre = norm_pre.reshape(DEPTH, 3, 1, D_MODEL)
    npost = norm_post.reshape(DEPTH, 3, 1, D_MODEL)
    wg, wu, wd = (w.astype(_BF16) for w in (ffn_w_gate, ffn_w_up, ffn_w_down))
    w_in_b, w_out_b, w_glu_b = (w.astype(_BF16) for w in (w_in, w_out, ssm_w_glu))
    d_skip = ssm_d.reshape(DEPTH, 1, SSM_WIDTH)
    b_glu = ssm_b_glu.reshape(DEPTH, 1, SSM_WIDTH)

    xa = jnp.concatenate([ctx, x], axis=1)
    for l in range(DEPTH):
        last = l == DEPTH - 1
        xa = _ffn(xa, mods, npre, npost, wg, wu, wd, l, 0, 0)
        q, k, v, u_ssm, gb, p = _proj(xa, mods, npre, w_in_b, cos_t, sin_t, l)
        attn = _attention(attn_sink[l], q, k, v)
        yf, yb = _s5_scan(u_ssm, bmat[l], cmat[l], a_bar[l])
        off = 1 if last else 0
        xa = _mix_out(xa, mods, npost, attn, u_ssm, yf, yb, gb, p, d_skip, w_glu_b, b_glu, conv_w,
                      w_out_b, l, off)
        xa = _ffn(xa, mods, npre, npost, wg, wu, wd, l, 2, 1)
    return xa
```

```python
import functools
import math

import jax
import jax.numpy as jnp
import numpy as np
from jax import lax
from jax.experimental import pallas as pl
from jax.experimental.pallas import tpu as pltpu

D_MODEL = 1024
BATCH = 16
SEQ = 4096
DEPTH = 4
GRID_W = 64
CTX_LEN = 256
HEAD_DIM = 64
ATTN_WIDTH = D_MODEL // 2
N_HEADS = ATTN_WIDTH // HEAD_DIM
KV_HEADS = N_HEADS // 4
Q_PER_KV = N_HEADS // KV_HEADS
KV_WIDTH = KV_HEADS * HEAD_DIM
WINDOW = 128
ATTN_SCALE = HEAD_DIM ** -0.5
LOG2_E = math.log2(math.e)
ROPE_BASE = 10000.0
ROPE_PAIRS = HEAD_DIM // 4
SSM_WIDTH = D_MODEL // 4
SSM_GROUP = 16
SSM_GROUPS = SSM_WIDTH // SSM_GROUP
SSM_STATE = 64
CONV_WIDTH = D_MODEL // 4
IN_COLS = ATTN_WIDTH + 2 * KV_WIDTH + SSM_WIDTH + 3 * CONV_WIDTH
D_FF = ((8 * D_MODEL // 3 + 127) // 128) * 128
MACARON = 0.5
N_MOD = 9
EPS = 1e-6
NEG_INF = -1e30

TOKENS = CTX_LEN + SEQ
TILE = 256
N_TILES = TOKENS // TILE
MOD_ROWS = 24
CTX_ROW = BATCH
N_LOCAL = TILE + 2 * WINDOW
STATE_W = SSM_GROUPS * SSM_STATE
SCAN_T = 32
SCAN_ROWS = SCAN_T * BATCH
SCAN_LANES = 512
N_SCAN = TOKENS // SCAN_T
N_SCAN_CTX = CTX_LEN // SCAN_T
VMEM_LIMIT = 48 * 1024 * 1024

_F32 = jnp.float32
_BF16 = jnp.bfloat16


def _rms(x, g):
    return (x * lax.rsqrt(jnp.mean(x * x, axis=-1, keepdims=True) + EPS)) * g


def _modulated(x, g, shift, scale):
    return _rms(x, g) * (1 + scale) + shift


def _ada_kernel(cc_ref, w_ref, b_ref, o_ref):
    cc = cc_ref[...]
    s = cc * jax.nn.sigmoid(cc)
    o_ref[...] = jnp.dot(s, w_ref[...], precision=lax.Precision.HIGHEST,
                         preferred_element_type=_F32) + b_ref[...]


def _ada_table(cc, w_ada, b_ada):
    out = pl.pallas_call(
        _ada_kernel,
        grid=(DEPTH, N_MOD),
        in_specs=[
            pl.BlockSpec((MOD_ROWS, D_MODEL), lambda l, j: (0, 0)),
            pl.BlockSpec((None, D_MODEL, D_MODEL), lambda l, j: (l, 0, j)),
            pl.BlockSpec((None, None, 1, D_MODEL), lambda l, j: (l, j, 0, 0)),
        ],
        out_specs=pl.BlockSpec((None, None, MOD_ROWS, D_MODEL), lambda l, j: (l, j, 0, 0)),
        out_shape=jax.ShapeDtypeStruct((DEPTH, N_MOD, MOD_ROWS, D_MODEL), _F32),
        name="ada_table",
    )(cc, w_ada, b_ada.reshape(DEPTH, N_MOD, 1, D_MODEL))
    return out.reshape(DEPTH, N_MOD, MOD_ROWS, 1, D_MODEL)


def _mod_spec(layer, j, off):
    def index(b, i):
        return (layer, j, jnp.where(i + off == 0, CTX_ROW, b), 0, 0)
    return pl.BlockSpec((None, None, None, 1, D_MODEL), index)


def _const_spec(shape, index):
    return pl.BlockSpec(shape, lambda b, i: index, pipeline_mode=pl.Buffered(1))


def _tile_spec(width, off=0):
    return pl.BlockSpec((None, TILE, width), lambda b, i: (b, i + off, 0))


_TOKEN_PARAMS = pltpu.CompilerParams(
    dimension_semantics=("parallel", "parallel"), vmem_limit_bytes=VMEM_LIMIT)


def _ffn_value(x, sh, sc, gt, gpre, gpost, wg_ref, wu_ref, wd_ref):
    h = _modulated(x, gpre, sh, sc).astype(_BF16)
    a = jnp.dot(h, wg_ref[...], preferred_element_type=_F32)
    b = jnp.dot(h, wu_ref[...], preferred_element_type=_F32)
    s = ((a * jax.nn.sigmoid(a)) * b).astype(_BF16)
    y = jnp.dot(s, wd_ref[...], preferred_element_type=_F32)
    return x + MACARON * (gt * _rms(y, gpost))


def _norm_spec(layer, sub):
    return _const_spec((None, None, 1, D_MODEL), (layer, sub, 0, 0))


def _ffn_weight_specs(layer, which):
    return [
        _const_spec((None, None, D_MODEL, D_FF), (layer, which, 0, 0)),
        _const_spec((None, None, D_MODEL, D_FF), (layer, which, 0, 0)),
        _const_spec((None, None, D_FF, D_MODEL), (layer, which, 0, 0)),
    ]


def _pre_kernel(*refs, split_input):
    if split_input:
        ctx_ref, x_ref = refs[:2]
        x = jnp.where(pl.program_id(1) == 0, ctx_ref[...], x_ref[...])
        refs = refs[2:]
    else:
        x = refs[0][...]
        refs = refs[1:]
    (sh0_ref, sc0_ref, gt0_ref, sh1_ref, sc1_ref, gpre0_ref, gpost0_ref, gpre1_ref,
     wg_ref, wu_ref, wd_ref, w_ref, cos_ref, sin_ref,
     xo_ref, q_ref, k_ref, v_ref, u_ref, gb_ref, p_ref) = refs
    x = _ffn_value(x, sh0_ref[...], sc0_ref[...], gt0_ref[...], gpre0_ref[...], gpost0_ref[...],
                   wg_ref, wu_ref, wd_ref)
    xo_ref[...] = x
    h = _modulated(x, gpre1_ref[...], sh1_ref[...], sc1_ref[...]).astype(_BF16)
    pr = jnp.dot(h, w_ref[...], preferred_element_type=_F32)
    cosv = cos_ref[...]
    sinv = sin_ref[...]
    lane = lax.broadcasted_iota(jnp.int32, (TILE, 128), 1)
    first_half = (lane % (2 * ROPE_PAIRS)) < ROPE_PAIRS

    def rope(t):
        partner = jnp.where(first_half, pltpu.roll(t, 128 - ROPE_PAIRS, 1), pltpu.roll(t, ROPE_PAIRS, 1))
        return t * cosv + partner * sinv

    for j in range(ATTN_WIDTH // 128):
        q_ref[:, j * 128:(j + 1) * 128] = (
            rope(pr[:, j * 128:(j + 1) * 128]) * (ATTN_SCALE * LOG2_E)).astype(_BF16)
    c = ATTN_WIDTH
    k_ref[...] = rope(pr[:, c:c + KV_WIDTH]).astype(_BF16)
    c += KV_WIDTH
    v_ref[...] = pr[:, c:c + KV_WIDTH].astype(_BF16)
    c += KV_WIDTH
    u_ref[...] = pr[:, c:c + SSM_WIDTH]
    c += SSM_WIDTH
    gb_ref[...] = pr[:, c:c + CONV_WIDTH]
    c += CONV_WIDTH
    p_ref[...] = pr[:, c:c + CONV_WIDTH] * pr[:, c + CONV_WIDTH:c + 2 * CONV_WIDTH]


def _pre(x_in, mods, norm_pre, norm_post, wg, wu, wd, w_in, cos_t, sin_t, layer):
    split_input = isinstance(x_in, tuple)
    if split_input:
        x_specs = [pl.BlockSpec((None, CTX_LEN, D_MODEL), lambda b, i: (b, 0, 0)),
                   pl.BlockSpec((None, TILE, D_MODEL), lambda b, i: (b, jnp.maximum(i - 1, 0), 0))]
        x_args = list(x_in)
    else:
        x_specs = [_tile_spec(D_MODEL)]
        x_args = [x_in]
    tok = lambda w, dt: jax.ShapeDtypeStruct((BATCH, TOKENS, w), dt)
    return pl.pallas_call(
        functools.partial(_pre_kernel, split_input=split_input),
        grid=(BATCH, N_TILES),
        in_specs=x_specs + [_mod_spec(layer, j, 0) for j in range(5)] + [
            _norm_spec(layer, 0), _norm_spec(layer, 0), _norm_spec(layer, 1),
            *_ffn_weight_specs(layer, 0),
            _const_spec((None, D_MODEL, IN_COLS), (layer, 0, 0)),
            pl.BlockSpec((TILE, 128), lambda b, i: (i, 0)),
            pl.BlockSpec((TILE, 128), lambda b, i: (i, 0)),
        ],
        out_specs=[
            _tile_spec(D_MODEL),
            _tile_spec(ATTN_WIDTH), _tile_spec(KV_WIDTH), _tile_spec(KV_WIDTH),
            _tile_spec(SSM_WIDTH), _tile_spec(CONV_WIDTH), _tile_spec(CONV_WIDTH),
        ],
        out_shape=[
            tok(D_MODEL, _F32),
            tok(ATTN_WIDTH, _BF16), tok(KV_WIDTH, _BF16), tok(KV_WIDTH, _BF16),
            tok(SSM_WIDTH, _F32), tok(CONV_WIDTH, _F32), tok(CONV_WIDTH, _F32),
        ],
        compiler_params=_TOKEN_PARAMS,
        name="ffn_in_proj",
    )(*x_args, *([mods] * 5), norm_pre, norm_post, norm_pre, wg, wu, wd, w_in, cos_t, sin_t)


def _dot_nt(a, b):
    return lax.dot_general(a, b, (((1,), (1,)), ((), ())), preferred_element_type=_F32)


def _dot_tn(a, b):
    return lax.dot_general(a, b, (((0,), (0,)), ((), ())), preferred_element_type=_F32)


def _attn_kernel(sink_ref, q_ref, k_ref, v_ref, o_ref):
    i = pl.program_id(1)
    q0 = i * TILE
    start = pl.multiple_of(jnp.clip(q0 - WINDOW, CTX_LEN, TOKENS - N_LOCAL), WINDOW)
    k_loc = k_ref[pl.ds(start, N_LOCAL), :]
    v_loc = v_ref[pl.ds(start, N_LOCAL), :]
    k_ctx = k_ref[0:CTX_LEN, :]
    v_ctx = v_ref[0:CTX_LEN, :]
    kj = start + lax.broadcasted_iota(jnp.int32, (N_LOCAL, TILE), 0)
    qi = q0 + lax.broadcasted_iota(jnp.int32, (N_LOCAL, TILE), 1)
    valid = (jnp.abs(kj - qi) <= WINDOW) & (i >= 1)
    bias = jnp.where(valid, 0.0, NEG_INF).astype(_F32)
    q = q_ref[...]
    outs = []

    def scores(h):
        lo = (h // Q_PER_KV) * HEAD_DIM
        qh = q[:, h * HEAD_DIM:(h + 1) * HEAD_DIM]
        return (_dot_nt(k_loc[:, lo:lo + HEAD_DIM], qh) + bias,
                _dot_nt(k_ctx[:, lo:lo + HEAD_DIM], qh))

    nxt = scores(0)
    for h in range(N_HEADS):
        lo = (h // Q_PER_KV) * HEAD_DIM
        s_loc, s_ctx = nxt
        if h + 1 < N_HEADS:
            nxt = scores(h + 1)
        sink = sink_ref[h] * LOG2_E
        m = jnp.maximum(jnp.maximum(jnp.max(s_loc, axis=0, keepdims=True),
                                    jnp.max(s_ctx, axis=0, keepdims=True)), sink)
        p_loc = jnp.exp2(s_loc - m)
        p_ctx = jnp.exp2(s_ctx - m)
        den = (jnp.sum(p_loc, axis=0, keepdims=True) + jnp.sum(p_ctx, axis=0, keepdims=True)
               + jnp.exp2(sink - m))
        o_t = (_dot_tn(v_loc[:, lo:lo + HEAD_DIM], p_loc.astype(_BF16))
               + _dot_tn(v_ctx[:, lo:lo + HEAD_DIM], p_ctx.astype(_BF16)))
        outs.append(o_t / den)
        if h % 2 == 1:
            pair = jnp.concatenate(outs, axis=0).T.astype(_BF16)
            o_ref[:, (h - 1) * HEAD_DIM:(h + 1) * HEAD_DIM] = pair
            outs = []


def _attention(sink, q, k, v):
    seq_spec = pl.BlockSpec((None, TOKENS, KV_WIDTH), lambda b, i: (b, 0, 0))
    return pl.pallas_call(
        _attn_kernel,
        grid=(BATCH, N_TILES),
        in_specs=[
            pl.BlockSpec(memory_space=pltpu.SMEM),
            _tile_spec(ATTN_WIDTH), seq_spec, seq_spec,
        ],
        out_specs=_tile_spec(ATTN_WIDTH),
        out_shape=jax.ShapeDtypeStruct((BATCH, TOKENS, ATTN_WIDTH), _BF16),
        compiler_params=_TOKEN_PARAMS,
        name="window_attn",
    )(sink, q, k, v)


def _s5_kernel(uf_ref, ub_ref, perm_ref, perm_t_ref, bmat_ref, cmat_ref, a_ref, yf_ref, yb_ref,
               buf_f_ref, buf_b_ref, state_ref):
    @pl.when(pl.program_id(0) == 0)
    def _():
        state_ref[...] = jnp.zeros_like(state_ref)

    dirs = ((uf_ref, yf_ref, buf_f_ref), (ub_ref, yb_ref, buf_b_ref))
    half = SCAN_ROWS // 2
    for d, (u_ref, _, buf_ref) in enumerate(dirs):
        u = u_ref[...].reshape(SCAN_ROWS, SSM_WIDTH).astype(_BF16)
        u_tb = jnp.dot(perm_ref[...], u, preferred_element_type=_F32).astype(_BF16)
        buf_ref[...] = jnp.dot(u_tb, bmat_ref[d], preferred_element_type=_F32)
    for d, (_, y_ref, buf_ref) in enumerate(dirs):
        for c in range(STATE_W // SCAN_LANES):
            re = slice(c * SCAN_LANES, (c + 1) * SCAN_LANES)
            im = slice(STATE_W + c * SCAN_LANES, STATE_W + (c + 1) * SCAN_LANES)
            a_r = jnp.broadcast_to(a_ref[d, 0:1, re], (BATCH, SCAN_LANES))
            a_i = jnp.broadcast_to(a_ref[d, 1:2, re], (BATCH, SCAN_LANES))
            s_r = state_ref[d, :, re]
            s_i = state_ref[d, :, im]
            for t in (range(SCAN_T) if d == 0 else reversed(range(SCAN_T))):
                rows = slice(t * BATCH, (t + 1) * BATCH)
                s_r, s_i = (a_r * s_r - a_i * s_i + buf_ref[rows, re],
                            a_r * s_i + a_i * s_r + buf_ref[rows, im])
                buf_ref[rows, re] = s_r
                buf_ref[rows, im] = s_i
            state_ref[d, :, re] = s_r
            state_ref[d, :, im] = s_i
        y_tb = jnp.concatenate(
            [jnp.dot(buf_ref[r:r + half, :].astype(_BF16), cmat_ref[d], preferred_element_type=_F32)
             for r in (0, half)], axis=0).astype(_BF16)
        y = jnp.concatenate(
            [jnp.dot(perm_t_ref[r:r + half, :], y_tb, preferred_element_type=_F32) for r in (0, half)],
            axis=0)
        y_ref[...] = y.astype(_BF16).reshape(BATCH, SCAN_T, SSM_WIDTH)


def _backward_chunk(i):
    return jnp.where(i < N_SCAN_CTX, N_SCAN_CTX - 1 - i, N_SCAN - 1 - i + N_SCAN_CTX)


def _scan_permutation():
    r = np.arange(SCAN_ROWS)
    perm = np.zeros((SCAN_ROWS, SCAN_ROWS), np.float32)
    perm[r, (r % BATCH) * SCAN_T + r // BATCH] = 1.0
    return jnp.asarray(perm, _BF16), jnp.asarray(perm.T, _BF16)


def _s5_scan(u, bmat, cmat, a_bar):
    chunk = (BATCH, SCAN_T, SSM_WIDTH)
    fwd = pl.BlockSpec(chunk, lambda i: (0, i, 0))
    bwd = pl.BlockSpec(chunk, lambda i: (0, _backward_chunk(i), 0))
    whole = lambda a: pl.BlockSpec(a.shape, lambda i: (0,) * a.ndim, pipeline_mode=pl.Buffered(1))
    perm, perm_t = _scan_permutation()
    y_shape = jax.ShapeDtypeStruct((BATCH, TOKENS, SSM_WIDTH), _BF16)
    return pl.pallas_call(
        _s5_kernel,
        grid=(N_SCAN,),
        in_specs=[fwd, bwd, whole(perm), whole(perm_t), whole(bmat), whole(cmat), whole(a_bar)],
        out_specs=[fwd, bwd],
        out_shape=[y_shape, y_shape],
        scratch_shapes=[
            pltpu.VMEM((SCAN_ROWS, 2 * STATE_W), _F32),
            pltpu.VMEM((SCAN_ROWS, 2 * STATE_W), _F32),
            pltpu.VMEM((2, BATCH, 2 * STATE_W), _F32),
        ],
        compiler_params=pltpu.CompilerParams(
            dimension_semantics=("arbitrary",), vmem_limit_bytes=VMEM_LIMIT),
        name="s5_scan",
    )(u, u, perm, perm_t, bmat, cmat, a_bar)


def _post_kernel(x_ref, gt1_ref, sh2_ref, sc2_ref, gt2_ref, gpost1_ref, gpre2_ref, gpost2_ref,
                 attn_ref, u_ref, yf_ref, yb_ref, gb_ref, p_ref, pprev_ref, pnext_ref,
                 d_ref, wglu_ref, bglu_ref, cw_ref, wout_ref, wg_ref, wu_ref, wd_ref, o_ref, *, off):
    i = pl.program_id(1) + off
    y = d_ref[...] * u_ref[...] + yf_ref[...] + yb_ref[...]
    g = jax.nn.gelu(y)
    ssm = g * jax.nn.sigmoid(
        jnp.dot(g.astype(_BF16), wglu_ref[...], preferred_element_type=_F32) + bglu_ref[...])

    p = p_ref[...]
    row = lax.broadcasted_iota(jnp.int32, (TILE, CONV_WIDTH), 0)
    has_prev = i >= 2
    has_next = (i >= 1) & (i <= N_TILES - 2)
    prev_row = jnp.where(has_prev, pprev_ref[7:8, :], 0.0)
    next_row = jnp.where(has_next, pnext_ref[0:1, :], 0.0)
    p_before = jnp.where(row == 0, prev_row, pltpu.roll(p, 1, 0))
    p_after = jnp.where(row == TILE - 1, next_row, pltpu.roll(p, TILE - 1, 0))
    cw = cw_ref[...]
    conv = gb_ref[...] * (p_before * cw[0:1, :] + p * cw[1:2, :] + p_after * cw[2:3, :])

    mix = jnp.concatenate([attn_ref[...], ssm.astype(_BF16), conv.astype(_BF16)], axis=1)
    yv = jnp.dot(mix, wout_ref[...], preferred_element_type=_F32)
    x = x_ref[...] + gt1_ref[...] * _rms(yv, gpost1_ref[...])
    o_ref[...] = _ffn_value(x, sh2_ref[...], sc2_ref[...], gt2_ref[...], gpre2_ref[...], gpost2_ref[...],
                            wg_ref, wu_ref, wd_ref)


def _post(x_all, mods, norm_pre, norm_post, attn, u_ssm, yf, yb, gb, p, ssm_d, w_glu, b_glu, conv_w,
          w_out, wg, wu, wd, layer, off):
    n_tiles = N_TILES - off
    rows8 = TILE // 8
    prev_spec = pl.BlockSpec((None, 8, CONV_WIDTH),
                             lambda b, i: (b, jnp.maximum((i + off) * rows8 - 1, 0), 0))
    next_spec = pl.BlockSpec((None, 8, CONV_WIDTH),
                             lambda b, i: (b, jnp.minimum((i + off + 1) * rows8, TOKENS // 8 - 1), 0))
    return pl.pallas_call(
        functools.partial(_post_kernel, off=off),
        grid=(BATCH, n_tiles),
        in_specs=[
            _tile_spec(D_MODEL, off),
            *[_mod_spec(layer, j, off) for j in (5, 6, 7, 8)],
            _norm_spec(layer, 1), _norm_spec(layer, 2), _norm_spec(layer, 2),
            _tile_spec(ATTN_WIDTH, off),
            _tile_spec(SSM_WIDTH, off), _tile_spec(SSM_WIDTH, off), _tile_spec(SSM_WIDTH, off),
            _tile_spec(CONV_WIDTH, off), _tile_spec(CONV_WIDTH, off),
            prev_spec, next_spec,
            _const_spec((None, 1, SSM_WIDTH), (layer, 0, 0)),
            _const_spec((None, SSM_WIDTH, SSM_WIDTH), (layer, 0, 0)),
            _const_spec((None, 1, SSM_WIDTH), (layer, 0, 0)),
            _const_spec((None, 3, CONV_WIDTH), (layer, 0, 0)),
            _const_spec((None, D_MODEL, D_MODEL), (layer, 0, 0)),
            *_ffn_weight_specs(layer, 1),
        ],
        out_specs=_tile_spec(D_MODEL),
        out_shape=jax.ShapeDtypeStruct((BATCH, n_tiles * TILE, D_MODEL), _F32),
        compiler_params=_TOKEN_PARAMS,
        name="mix_out_ffn",
    )(x_all, *([mods] * 4), norm_post, norm_pre, norm_post, attn, u_ssm, yf, yb, gb, p, p, p,
      ssm_d, w_glu, b_glu, conv_w, w_out, wg, wu, wd)


def _rope_tables():
    pos = jnp.arange(SEQ)
    row = (pos // GRID_W).astype(_F32)
    col = (pos % GRID_W).astype(_F32)
    inv_freq = ROPE_BASE ** (-jnp.arange(ROPE_PAIRS, dtype=_F32) / ROPE_PAIRS)
    ang = jnp.stack([row[:, None] * inv_freq, col[:, None] * inv_freq], axis=1)
    cos, sin = jnp.cos(ang), jnp.sin(ang)
    cos_h = jnp.concatenate([cos, cos], axis=-1).reshape(SEQ, HEAD_DIM)
    sin_h = jnp.concatenate([-sin, sin], axis=-1).reshape(SEQ, HEAD_DIM)
    cos_t = jnp.concatenate([jnp.ones((CTX_LEN, HEAD_DIM), _F32), cos_h], axis=0)
    sin_t = jnp.concatenate([jnp.zeros((CTX_LEN, HEAD_DIM), _F32), sin_h], axis=0)
    return jnp.tile(cos_t, (1, 128 // HEAD_DIM)), jnp.tile(sin_t, (1, 128 // HEAD_DIM))


def _zoh(lam_re, lam_im, log_step, b_re, b_im):
    dt = jnp.exp(log_step)[..., None]
    mag = jnp.exp(lam_re * dt)
    ar, ai = mag * jnp.cos(lam_im * dt), mag * jnp.sin(lam_im * dt)
    den = lam_re * lam_re + lam_im * lam_im
    gr = ((ar - 1) * lam_re + ai * lam_im) / den
    gi = (ai * lam_re - (ar - 1) * lam_im) / den
    bbr = gr[..., None] * b_re - gi[..., None] * b_im
    bbi = gr[..., None] * b_im + gi[..., None] * b_re
    return ar, ai, bbr, bbi


def _s5_matrices(lam_re, lam_im, log_step, b_re, b_im, c_re, c_im):
    ar, ai, bbr, bbi = _zoh(lam_re, lam_im, log_step, b_re, b_im)
    eye = jnp.eye(SSM_GROUPS, dtype=_F32)
    lead = (DEPTH, 2)
    blk_in = lambda m: jnp.einsum('ldgph,gk->ldghkp', m, eye).reshape(lead + (SSM_WIDTH, STATE_W))
    blk_out = lambda m: jnp.einsum('ldghp,gk->ldkpgh', m, eye).reshape(lead + (STATE_W, SSM_WIDTH))
    bmat = jnp.concatenate([blk_in(bbr), blk_in(bbi)], axis=-1).astype(_BF16)
    cmat = jnp.concatenate([blk_out(c_re), -blk_out(c_im)], axis=-2).astype(_BF16)
    a_bar = jnp.stack([ar.reshape(lead + (STATE_W,)), ai.reshape(lead + (STATE_W,))], axis=2)
    return bmat, cmat, a_bar


def kernel(x, c, ctx, c_ctx, w_ada, b_ada, norm_pre, norm_post, ffn_w_gate, ffn_w_up, ffn_w_down,
           w_in, w_out, attn_sink, ssm_lambda_re, ssm_lambda_im, ssm_log_step, ssm_b_re, ssm_b_im,
           ssm_c_re, ssm_c_im, ssm_d, ssm_w_glu, ssm_b_glu, conv_w):
    cc = jnp.concatenate([c, c_ctx[None, :], jnp.zeros((MOD_ROWS - BATCH - 1, D_MODEL), _F32)], axis=0)
    mods = _ada_table(cc, w_ada, b_ada)
    cos_t, sin_t = _rope_tables()
    bmat, cmat, a_bar = _s5_matrices(ssm_lambda_re, ssm_lambda_im, ssm_log_step,
                                     ssm_b_re, ssm_b_im, ssm_c_re, ssm_c_im)
    npre = norm_pre.reshape(DEPTH, 3, 1, D_MODEL)
    npost = norm_post.reshape(DEPTH, 3, 1, D_MODEL)
    wg, wu, wd = (w.astype(_BF16) for w in (ffn_w_gate, ffn_w_up, ffn_w_down))
    w_in_b, w_out_b, w_glu_b = (w.astype(_BF16) for w in (w_in, w_out, ssm_w_glu))
    d_skip = ssm_d.reshape(DEPTH, 1, SSM_WIDTH)
    b_glu = ssm_b_glu.reshape(DEPTH, 1, SSM_WIDTH)

    xa = (ctx, x)
    for l in range(DEPTH):
        xa, q, k, v, u_ssm, gb, p = _pre(xa, mods, npre, npost, wg, wu, wd, w_in_b, cos_t, sin_t, l)
        attn = _attention(attn_sink[l], q, k, v)
        yf, yb = _s5_scan(u_ssm, bmat[l], cmat[l], a_bar[l])
        off = 1 if l == DEPTH - 1 else 0
        xa = _post(xa, mods, npre, npost, attn, u_ssm, yf, yb, gb, p, d_skip, w_glu_b, b_glu, conv_w,
                   w_out_b, wg, wu, wd, l, off)
    return xa
```

```python
import functools
import math

import jax
import jax.numpy as jnp
import numpy as np
from jax import lax
from jax.experimental import pallas as pl
from jax.experimental.pallas import tpu as pltpu

D_MODEL = 1024
BATCH = 16
SEQ = 4096
DEPTH = 4
GRID_W = 64
CTX_LEN = 256
HEAD_DIM = 64
ATTN_WIDTH = D_MODEL // 2
N_HEADS = ATTN_WIDTH // HEAD_DIM
KV_HEADS = N_HEADS // 4
Q_PER_KV = N_HEADS // KV_HEADS
KV_WIDTH = KV_HEADS * HEAD_DIM
WINDOW = 128
ATTN_SCALE = HEAD_DIM ** -0.5
LOG2_E = math.log2(math.e)
ROPE_BASE = 10000.0
ROPE_PAIRS = HEAD_DIM // 4
SSM_WIDTH = D_MODEL // 4
SSM_GROUP = 16
SSM_GROUPS = SSM_WIDTH // SSM_GROUP
SSM_STATE = 64
CONV_WIDTH = D_MODEL // 4
IN_COLS = ATTN_WIDTH + 2 * KV_WIDTH + SSM_WIDTH + 3 * CONV_WIDTH
D_FF = ((8 * D_MODEL // 3 + 127) // 128) * 128
MACARON = 0.5
N_MOD = 9
EPS = 1e-6
NEG_INF = -1e30

TOKENS = CTX_LEN + SEQ
TILE = 256
N_TILES = TOKENS // TILE
MOD_ROWS = 24
CTX_ROW = BATCH
N_LOCAL = TILE + 2 * WINDOW
STATE_W = SSM_GROUPS * SSM_STATE
SCAN_T = 32
SCAN_ROWS = SCAN_T * BATCH
SCAN_LANES = 512
N_SCAN = TOKENS // SCAN_T
N_SCAN_CTX = CTX_LEN // SCAN_T
VMEM_LIMIT = 48 * 1024 * 1024

_F32 = jnp.float32
_BF16 = jnp.bfloat16


def _rms(x, g):
    return (x * lax.rsqrt(jnp.mean(x * x, axis=-1, keepdims=True) + EPS)) * g


def _modulated(x, g, shift, scale):
    return _rms(x, g) * (1 + scale) + shift


def _ada_kernel(cc_ref, w_ref, b_ref, o_ref):
    cc = cc_ref[...]
    s = cc * jax.nn.sigmoid(cc)
    o_ref[...] = jnp.dot(s, w_ref[...], precision=lax.Precision.HIGHEST,
                         preferred_element_type=_F32) + b_ref[...]


def _ada_table(cc, w_ada, b_ada):
    out = pl.pallas_call(
        _ada_kernel,
        grid=(DEPTH, N_MOD),
        in_specs=[
            pl.BlockSpec((MOD_ROWS, D_MODEL), lambda l, j: (0, 0)),
            pl.BlockSpec((None, D_MODEL, D_MODEL), lambda l, j: (l, 0, j)),
            pl.BlockSpec((None, None, 1, D_MODEL), lambda l, j: (l, j, 0, 0)),
        ],
        out_specs=pl.BlockSpec((None, None, MOD_ROWS, D_MODEL), lambda l, j: (l, j, 0, 0)),
        out_shape=jax.ShapeDtypeStruct((DEPTH, N_MOD, MOD_ROWS, D_MODEL), _F32),
        name="ada_table",
    )(cc, w_ada, b_ada.reshape(DEPTH, N_MOD, 1, D_MODEL))
    return out.reshape(DEPTH, N_MOD, MOD_ROWS, 1, D_MODEL)


def _stream_tile(p, h):
    return 2 * p + h


def _latent_tile(p, h):
    per_batch = (N_TILES - 1) // 2
    return (p // per_batch) * N_TILES + 1 + 2 * (p % per_batch) + h


def _row_spec(width, tile_fn, h):
    return pl.BlockSpec((TILE, width), lambda p: (tile_fn(p, h), 0))


def _mod_spec(layer, j, tile_fn, h):
    def index(p):
        t = tile_fn(p, h)
        return (layer, j, jnp.where(t % N_TILES == 0, CTX_ROW, t // N_TILES), 0, 0)
    return pl.BlockSpec((None, None, None, 1, D_MODEL), index)


def _const_spec(shape, index):
    return pl.BlockSpec(shape, lambda p: index, pipeline_mode=pl.Buffered(1))


def _norm_spec(layer, sub):
    return _const_spec((None, None, 1, D_MODEL), (layer, sub, 0, 0))


def _ffn_weight_specs(layer, which):
    return [
        _const_spec((None, None, D_MODEL, D_FF), (layer, which, 0, 0)),
        _const_spec((None, None, D_MODEL, D_FF), (layer, which, 0, 0)),
        _const_spec((None, None, D_FF, D_MODEL), (layer, which, 0, 0)),
    ]


_TOKEN_PARAMS = pltpu.CompilerParams(dimension_semantics=("parallel",), vmem_limit_bytes=VMEM_LIMIT)


def _swiglu(h, wg_ref, wu_ref, wd_ref):
    a = jnp.dot(h, wg_ref[...], preferred_element_type=_F32)
    b = jnp.dot(h, wu_ref[...], preferred_element_type=_F32)
    s = ((a * jax.nn.sigmoid(a)) * b).astype(_BF16)
    return jnp.dot(s, wd_ref[...], preferred_element_type=_F32)


def _ffn_pair(xs, mods, gpre, gpost, wg_ref, wu_ref, wd_ref):
    hs = [_modulated(x, gpre, sh[...], sc[...]).astype(_BF16) for x, (sh, sc, _) in zip(xs, mods)]
    ys = [_swiglu(h, wg_ref, wu_ref, wd_ref) for h in hs]
    return [x + MACARON * (gt[...] * _rms(y, gpost)) for x, y, (_, _, gt) in zip(xs, ys, mods)]


def _pre_kernel(*refs, split_input):
    n_x = 2 if split_input else 1
    halves = []
    for h in range(2):
        group, refs = refs[:n_x + 7], refs[n_x + 7:]
        halves.append(group)
    (gpre0_ref, gpost0_ref, gpre1_ref, wg_ref, wu_ref, wd_ref, w_ref,
     xo_ref, q_ref, k_ref, v_ref, u_ref, gb_ref, p_ref) = refs
    xs = []
    for h, group in enumerate(halves):
        if split_input:
            is_ctx = _stream_tile(pl.program_id(0), h) % N_TILES == 0
            xs.append(jnp.where(is_ctx, group[0][...], group[1][...]))
        else:
            xs.append(group[0][...])
    mods0 = [group[n_x:n_x + 3] for group in halves]
    xs = _ffn_pair(xs, mods0, gpre0_ref[...], gpost0_ref[...], wg_ref, wu_ref, wd_ref)
    for h, x in enumerate(xs):
        xo_ref[h * TILE:(h + 1) * TILE, :] = x
    hs = [_modulated(x, gpre1_ref[...], group[n_x + 3][...], group[n_x + 4][...]).astype(_BF16)
          for x, group in zip(xs, halves)]
    prs = [jnp.dot(h, w_ref[...], preferred_element_type=_F32) for h in hs]
    lane = lax.broadcasted_iota(jnp.int32, (TILE, 128), 1)
    first_half = (lane % (2 * ROPE_PAIRS)) < ROPE_PAIRS
    for h, (pr, group) in enumerate(zip(prs, halves)):
        rows = slice(h * TILE, (h + 1) * TILE)
        cosv = group[n_x + 5][...]
        sinv = group[n_x + 6][...]

        def rope(t, cosv=cosv, sinv=sinv):
            partner = jnp.where(first_half, pltpu.roll(t, 128 - ROPE_PAIRS, 1), pltpu.roll(t, ROPE_PAIRS, 1))
            return t * cosv + partner * sinv

        for j in range(ATTN_WIDTH // 128):
            q_ref[rows, j * 128:(j + 1) * 128] = (
                rope(pr[:, j * 128:(j + 1) * 128]) * (ATTN_SCALE * LOG2_E)).astype(_BF16)
        c = ATTN_WIDTH
        k_ref[rows, :] = rope(pr[:, c:c + KV_WIDTH]).astype(_BF16)
        c += KV_WIDTH
        v_ref[rows, :] = pr[:, c:c + KV_WIDTH].astype(_BF16)
        c += KV_WIDTH
        u_ref[rows, :] = pr[:, c:c + SSM_WIDTH]
        c += SSM_WIDTH
        gb_ref[rows, :] = pr[:, c:c + CONV_WIDTH]
        c += CONV_WIDTH
        p_ref[rows, :] = pr[:, c:c + CONV_WIDTH] * pr[:, c + CONV_WIDTH:c + 2 * CONV_WIDTH]


def _pre(x_in, mods, norm_pre, norm_post, wg, wu, wd, w_in, cos_t, sin_t, layer):
    split_input = isinstance(x_in, tuple)
    tile = _stream_tile
    in_specs, args = [], []
    for h in range(2):
        if split_input:
            in_specs += [
                pl.BlockSpec((None, CTX_LEN, D_MODEL), lambda p, h=h: (tile(p, h) // N_TILES, 0, 0)),
                pl.BlockSpec((None, TILE, D_MODEL),
                             lambda p, h=h: (tile(p, h) // N_TILES, jnp.maximum(tile(p, h) % N_TILES - 1, 0), 0)),
            ]
            args += list(x_in)
        else:
            in_specs.append(_row_spec(D_MODEL, tile, h))
            args.append(x_in)
        in_specs += [_mod_spec(layer, j, tile, h) for j in range(5)]
        args += [mods] * 5
        in_specs += [pl.BlockSpec((TILE, 128), lambda p, h=h: (tile(p, h) % N_TILES, 0)) for _ in range(2)]
        args += [cos_t, sin_t]
    in_specs += [_norm_spec(layer, 0), _norm_spec(layer, 0), _norm_spec(layer, 1),
                 *_ffn_weight_specs(layer, 0), _const_spec((None, D_MODEL, IN_COLS), (layer, 0, 0))]
    args += [norm_pre, norm_post, norm_pre, wg, wu, wd, w_in]
    pair = lambda w: pl.BlockSpec((2 * TILE, w), lambda p: (p, 0))
    tok = lambda w, dt: jax.ShapeDtypeStruct((BATCH * TOKENS, w), dt)
    return pl.pallas_call(
        functools.partial(_pre_kernel, split_input=split_input),
        grid=(BATCH * N_TILES // 2,),
        in_specs=in_specs,
        out_specs=[pair(D_MODEL), pair(ATTN_WIDTH), pair(KV_WIDTH), pair(KV_WIDTH),
                   pair(SSM_WIDTH), pair(CONV_WIDTH), pair(CONV_WIDTH)],
        out_shape=[
            tok(D_MODEL, _F32),
            tok(ATTN_WIDTH, _BF16), tok(KV_WIDTH, _BF16), tok(KV_WIDTH, _BF16),
            tok(SSM_WIDTH, _F32), tok(CONV_WIDTH, _F32), tok(CONV_WIDTH, _F32),
        ],
        compiler_params=_TOKEN_PARAMS,
        name="ffn_in_proj",
    )(*args)


def _dot_nt(a, b):
    return lax.dot_general(a, b, (((1,), (1,)), ((), ())), preferred_element_type=_F32)


def _dot_tn(a, b):
    return lax.dot_general(a, b, (((0,), (0,)), ((), ())), preferred_element_type=_F32)


def _attn_kernel(sink_ref, q_ref, k_ref, v_ref, o_ref):
    i = pl.program_id(1)
    q0 = i * TILE
    start = pl.multiple_of(jnp.clip(q0 - WINDOW, CTX_LEN, TOKENS - N_LOCAL), WINDOW)
    k_loc = k_ref[pl.ds(start, N_LOCAL), :]
    v_loc = v_ref[pl.ds(start, N_LOCAL), :]
    k_ctx = k_ref[0:CTX_LEN, :]
    v_ctx = v_ref[0:CTX_LEN, :]
    kj = start + lax.broadcasted_iota(jnp.int32, (N_LOCAL, TILE), 0)
    qi = q0 + lax.broadcasted_iota(jnp.int32, (N_LOCAL, TILE), 1)
    valid = (jnp.abs(kj - qi) <= WINDOW) & (i >= 1)
    bias = jnp.where(valid, 0.0, NEG_INF).astype(_F32)
    q = q_ref[...]
    outs = []

    def scores(h):
        lo = (h // Q_PER_KV) * HEAD_DIM
        qh = q[:, h * HEAD_DIM:(h + 1) * HEAD_DIM]
        return (_dot_nt(k_loc[:, lo:lo + HEAD_DIM], qh) + bias,
                _dot_nt(k_ctx[:, lo:lo + HEAD_DIM], qh))

    nxt = scores(0)
    for h in range(N_HEADS):
        lo = (h // Q_PER_KV) * HEAD_DIM
        s_loc, s_ctx = nxt
        if h + 1 < N_HEADS:
            nxt = scores(h + 1)
        sink = sink_ref[h] * LOG2_E
        m = jnp.maximum(jnp.maximum(jnp.max(s_loc, axis=0, keepdims=True),
                                    jnp.max(s_ctx, axis=0, keepdims=True)), sink)
        p_loc = jnp.exp2(s_loc - m)
        p_ctx = jnp.exp2(s_ctx - m)
        den = (jnp.sum(p_loc, axis=0, keepdims=True) + jnp.sum(p_ctx, axis=0, keepdims=True)
               + jnp.exp2(sink - m))
        o_t = (_dot_tn(v_loc[:, lo:lo + HEAD_DIM], p_loc.astype(_BF16))
               + _dot_tn(v_ctx[:, lo:lo + HEAD_DIM], p_ctx.astype(_BF16)))
        outs.append(o_t / den)
        if h % 2 == 1:
            pair = jnp.concatenate(outs, axis=0).T.astype(_BF16)
            o_ref[:, (h - 1) * HEAD_DIM:(h + 1) * HEAD_DIM] = pair
            outs = []


def _attention(sink, q, k, v):
    seq_spec = pl.BlockSpec((None, TOKENS, KV_WIDTH), lambda b, i: (b, 0, 0))
    tile_spec = pl.BlockSpec((None, TILE, ATTN_WIDTH), lambda b, i: (b, i, 0))
    return pl.pallas_call(
        _attn_kernel,
        grid=(BATCH, N_TILES),
        in_specs=[
            pl.BlockSpec(memory_space=pltpu.SMEM),
            tile_spec, seq_spec, seq_spec,
        ],
        out_specs=tile_spec,
        out_shape=jax.ShapeDtypeStruct((BATCH, TOKENS, ATTN_WIDTH), _BF16),
        compiler_params=pltpu.CompilerParams(
            dimension_semantics=("parallel", "parallel"), vmem_limit_bytes=VMEM_LIMIT),
        name="window_attn",
    )(sink, q, k, v)


def _s5_kernel(uf_ref, ub_ref, perm_ref, perm_t_ref, bmat_ref, cmat_ref, a_ref, yf_ref, yb_ref,
               buf_f_ref, buf_b_ref, state_ref):
    @pl.when(pl.program_id(0) == 0)
    def _():
        state_ref[...] = jnp.zeros_like(state_ref)

    dirs = ((uf_ref, yf_ref, buf_f_ref), (ub_ref, yb_ref, buf_b_ref))
    half = SCAN_ROWS // 2
    for d, (u_ref, _, buf_ref) in enumerate(dirs):
        u = u_ref[...].reshape(SCAN_ROWS, SSM_WIDTH).astype(_BF16)
        u_tb = jnp.dot(perm_ref[...], u, preferred_element_type=_F32).astype(_BF16)
        buf_ref[...] = jnp.dot(u_tb, bmat_ref[d], preferred_element_type=_F32)
    for d, (_, y_ref, buf_ref) in enumerate(dirs):
        for c in range(STATE_W // SCAN_LANES):
            re = slice(c * SCAN_LANES, (c + 1) * SCAN_LANES)
            im = slice(STATE_W + c * SCAN_LANES, STATE_W + (c + 1) * SCAN_LANES)
            a_r = jnp.broadcast_to(a_ref[d, 0:1, re], (BATCH, SCAN_LANES))
            a_i = jnp.broadcast_to(a_ref[d, 1:2, re], (BATCH, SCAN_LANES))
            s_r = state_ref[d, :, re]
            s_i = state_ref[d, :, im]
            for t in (range(SCAN_T) if d == 0 else reversed(range(SCAN_T))):
                rows = slice(t * BATCH, (t + 1) * BATCH)
                s_r, s_i = (a_r * s_r - a_i * s_i + buf_ref[rows, re],
                            a_r * s_i + a_i * s_r + buf_ref[rows, im])
                buf_ref[rows, re] = s_r
                buf_ref[rows, im] = s_i
            state_ref[d, :, re] = s_r
            state_ref[d, :, im] = s_i
        y_tb = jnp.concatenate(
            [jnp.dot(buf_ref[r:r + half, :].astype(_BF16), cmat_ref[d], preferred_element_type=_F32)
             for r in (0, half)], axis=0).astype(_BF16)
        y = jnp.concatenate(
            [jnp.dot(perm_t_ref[r:r + half, :], y_tb, preferred_element_type=_F32) for r in (0, half)],
            axis=0)
        y_ref[...] = y.astype(_BF16).reshape(BATCH, SCAN_T, SSM_WIDTH)


def _backward_chunk(i):
    return jnp.where(i < N_SCAN_CTX, N_SCAN_CTX - 1 - i, N_SCAN - 1 - i + N_SCAN_CTX)


def _scan_permutation():
    r = np.arange(SCAN_ROWS)
    perm = np.zeros((SCAN_ROWS, SCAN_ROWS), np.float32)
    perm[r, (r % BATCH) * SCAN_T + r // BATCH] = 1.0
    return jnp.asarray(perm, _BF16), jnp.asarray(perm.T, _BF16)


def _s5_scan(u, bmat, cmat, a_bar):
    chunk = (BATCH, SCAN_T, SSM_WIDTH)
    fwd = pl.BlockSpec(chunk, lambda i: (0, i, 0))
    bwd = pl.BlockSpec(chunk, lambda i: (0, _backward_chunk(i), 0))
    whole = lambda a: pl.BlockSpec(a.shape, lambda i: (0,) * a.ndim, pipeline_mode=pl.Buffered(1))
    perm, perm_t = _scan_permutation()
    y_shape = jax.ShapeDtypeStruct((BATCH, TOKENS, SSM_WIDTH), _BF16)
    return pl.pallas_call(
        _s5_kernel,
        grid=(N_SCAN,),
        in_specs=[fwd, bwd, whole(perm), whole(perm_t), whole(bmat), whole(cmat), whole(a_bar)],
        out_specs=[fwd, bwd],
        out_shape=[y_shape, y_shape],
        scratch_shapes=[
            pltpu.VMEM((SCAN_ROWS, 2 * STATE_W), _F32),
            pltpu.VMEM((SCAN_ROWS, 2 * STATE_W), _F32),
            pltpu.VMEM((2, BATCH, 2 * STATE_W), _F32),
        ],
        compiler_params=pltpu.CompilerParams(
            dimension_semantics=("arbitrary",), vmem_limit_bytes=VMEM_LIMIT),
        name="s5_scan",
    )(u, u, perm, perm_t, bmat, cmat, a_bar)


N_ROW_INPUTS = 7


def _post_kernel(*refs, tile_fn):
    halves = []
    for h in range(2):
        group, refs = refs[:N_ROW_INPUTS + 4], refs[N_ROW_INPUTS + 4:]
        halves.append(group)
    (pprev_ref, pnext_ref, gpost1_ref, gpre2_ref, gpost2_ref, d_ref, wglu_ref, bglu_ref, cw_ref,
     wout_ref, wg_ref, wu_ref, wd_ref, o_ref) = refs
    tiles = [tile_fn(pl.program_id(0), h) % N_TILES for h in range(2)]
    ps = [group[6][...] for group in halves]
    edge_before = [jnp.where(tiles[0] >= 2, pprev_ref[7:8, :], 0.0),
                   jnp.where(tiles[1] >= 2, ps[0][TILE - 1:TILE, :], 0.0)]
    edge_after = [jnp.where((tiles[0] >= 1) & (tiles[0] <= N_TILES - 2), ps[1][0:1, :], 0.0),
                  jnp.where((tiles[1] >= 1) & (tiles[1] <= N_TILES - 2), pnext_ref[0:1, :], 0.0)]
    row = lax.broadcasted_iota(jnp.int32, (TILE, CONV_WIDTH), 0)
    cw = cw_ref[...]

    def mixed_and_modulated(h):
        group = halves[h]
        x_ref, attn_ref, u_ref, yf_ref, yb_ref, gb_ref, _ = group[:N_ROW_INPUTS]
        gt1_ref, sh2_ref, sc2_ref, _ = group[N_ROW_INPUTS:]
        y = d_ref[...] * u_ref[...] + yf_ref[...] + yb_ref[...]
        g = jax.nn.gelu(y)
        ssm = g * jax.nn.sigmoid(
            jnp.dot(g.astype(_BF16), wglu_ref[...], preferred_element_type=_F32) + bglu_ref[...])
        p = ps[h]
        p_before = jnp.where(row == 0, edge_before[h], pltpu.roll(p, 1, 0))
        p_after = jnp.where(row == TILE - 1, edge_after[h], pltpu.roll(p, TILE - 1, 0))
        conv = gb_ref[...] * (p_before * cw[0:1, :] + p * cw[1:2, :] + p_after * cw[2:3, :])
        mix = jnp.concatenate([attn_ref[...], ssm.astype(_BF16), conv.astype(_BF16)], axis=1)
        yv = jnp.dot(mix, wout_ref[...], preferred_element_type=_F32)
        x = x_ref[...] + gt1_ref[...] * _rms(yv, gpost1_ref[...])
        return x, _modulated(x, gpre2_ref[...], sh2_ref[...], sc2_ref[...]).astype(_BF16)

    (x0, h0), (x1, h1) = mixed_and_modulated(0), mixed_and_modulated(1)
    y0 = _swiglu(h0, wg_ref, wu_ref, wd_ref)
    y1 = _swiglu(h1, wg_ref, wu_ref, wd_ref)
    for h, (x, y) in enumerate(((x0, y0), (x1, y1))):
        gt2_ref = halves[h][N_ROW_INPUTS + 3]
        o_ref[h * TILE:(h + 1) * TILE, :] = x + MACARON * (gt2_ref[...] * _rms(y, gpost2_ref[...]))


def _post(x_all, mods, norm_pre, norm_post, attn, u_ssm, yf, yb, gb, p, ssm_d, w_glu, b_glu, conv_w,
          w_out, wg, wu, wd, layer, latent_only):
    tile = _latent_tile if latent_only else _stream_tile
    n_pairs = BATCH * (N_TILES - 1) // 2 if latent_only else BATCH * N_TILES // 2
    rows8 = TILE // 8
    in_specs, args = [], []
    for h in range(2):
        for arr in (x_all, attn, u_ssm, yf, yb, gb, p):
            in_specs.append(_row_spec(arr.shape[1], tile, h))
            args.append(arr)
        in_specs += [_mod_spec(layer, j, tile, h) for j in (5, 6, 7, 8)]
        args += [mods] * 4
    in_specs += [
        pl.BlockSpec((8, CONV_WIDTH), lambda p_: (jnp.maximum(tile(p_, 0) * rows8 - 1, 0), 0)),
        pl.BlockSpec((8, CONV_WIDTH),
                     lambda p_: (jnp.minimum((tile(p_, 1) + 1) * rows8, BATCH * TOKENS // 8 - 1), 0)),
        _norm_spec(layer, 1), _norm_spec(layer, 2), _norm_spec(layer, 2),
        _const_spec((None, 1, SSM_WIDTH), (layer, 0, 0)),
        _const_spec((None, SSM_WIDTH, SSM_WIDTH), (layer, 0, 0)),
        _const_spec((None, 1, SSM_WIDTH), (layer, 0, 0)),
        _const_spec((None, 3, CONV_WIDTH), (layer, 0, 0)),
        _const_spec((None, D_MODEL, D_MODEL), (layer, 0, 0)),
        *_ffn_weight_specs(layer, 1),
    ]
    args += [p, p, norm_post, norm_pre, norm_post, ssm_d, w_glu, b_glu, conv_w, w_out, wg, wu, wd]
    return pl.pallas_call(
        functools.partial(_post_kernel, tile_fn=tile),
        grid=(n_pairs,),
        in_specs=in_specs,
        out_specs=pl.BlockSpec((2 * TILE, D_MODEL), lambda p_: (p_, 0)),
        out_shape=jax.ShapeDtypeStruct((n_pairs * 2 * TILE, D_MODEL), _F32),
        compiler_params=_TOKEN_PARAMS,
        name="mix_out_ffn",
    )(*args)


def _rope_tables():
    pos = jnp.arange(SEQ)
    row = (pos // GRID_W).astype(_F32)
    col = (pos % GRID_W).astype(_F32)
    inv_freq = ROPE_BASE ** (-jnp.arange(ROPE_PAIRS, dtype=_F32) / ROPE_PAIRS)
    ang = jnp.stack([row[:, None] * inv_freq, col[:, None] * inv_freq], axis=1)
    cos, sin = jnp.cos(ang), jnp.sin(ang)
    cos_h = jnp.concatenate([cos, cos], axis=-1).reshape(SEQ, HEAD_DIM)
    sin_h = jnp.concatenate([-sin, sin], axis=-1).reshape(SEQ, HEAD_DIM)
    cos_t = jnp.concatenate([jnp.ones((CTX_LEN, HEAD_DIM), _F32), cos_h], axis=0)
    sin_t = jnp.concatenate([jnp.zeros((CTX_LEN, HEAD_DIM), _F32), sin_h], axis=0)
    return jnp.tile(cos_t, (1, 128 // HEAD_DIM)), jnp.tile(sin_t, (1, 128 // HEAD_DIM))


def _zoh(lam_re, lam_im, log_step, b_re, b_im):
    dt = jnp.exp(log_step)[..., None]
    mag = jnp.exp(lam_re * dt)
    ar, ai = mag * jnp.cos(lam_im * dt), mag * jnp.sin(lam_im * dt)
    den = lam_re * lam_re + lam_im * lam_im
    gr = ((ar - 1) * lam_re + ai * lam_im) / den
    gi = (ai * lam_re - (ar - 1) * lam_im) / den
    bbr = gr[..., None] * b_re - gi[..., None] * b_im
    bbi = gr[..., None] * b_im + gi[..., None] * b_re
    return ar, ai, bbr, bbi


def _s5_matrices(lam_re, lam_im, log_step, b_re, b_im, c_re, c_im):
    ar, ai, bbr, bbi = _zoh(lam_re, lam_im, log_step, b_re, b_im)
    eye = jnp.eye(SSM_GROUPS, dtype=_F32)
    lead = (DEPTH, 2)
    blk_in = lambda m: jnp.einsum('ldgph,gk->ldghkp', m, eye).reshape(lead + (SSM_WIDTH, STATE_W))
    blk_out = lambda m: jnp.einsum('ldghp,gk->ldkpgh', m, eye).reshape(lead + (STATE_W, SSM_WIDTH))
    bmat = jnp.concatenate([blk_in(bbr), blk_in(bbi)], axis=-1).astype(_BF16)
    cmat = jnp.concatenate([blk_out(c_re), -blk_out(c_im)], axis=-2).astype(_BF16)
    a_bar = jnp.stack([ar.reshape(lead + (STATE_W,)), ai.reshape(lead + (STATE_W,))], axis=2)
    return bmat, cmat, a_bar


def kernel(x, c, ctx, c_ctx, w_ada, b_ada, norm_pre, norm_post, ffn_w_gate, ffn_w_up, ffn_w_down,
           w_in, w_out, attn_sink, ssm_lambda_re, ssm_lambda_im, ssm_log_step, ssm_b_re, ssm_b_im,
           ssm_c_re, ssm_c_im, ssm_d, ssm_w_glu, ssm_b_glu, conv_w):
    cc = jnp.concatenate([c, c_ctx[None, :], jnp.zeros((MOD_ROWS - BATCH - 1, D_MODEL), _F32)], axis=0)
    mods = _ada_table(cc, w_ada, b_ada)
    cos_t, sin_t = _rope_tables()
    bmat, cmat, a_bar = _s5_matrices(ssm_lambda_re, ssm_lambda_im, ssm_log_step,
                                     ssm_b_re, ssm_b_im, ssm_c_re, ssm_c_im)
    npre = norm_pre.reshape(DEPTH, 3, 1, D_MODEL)
    npost = norm_post.reshape(DEPTH, 3, 1, D_MODEL)
    wg, wu, wd = (w.astype(_BF16) for w in (ffn_w_gate, ffn_w_up, ffn_w_down))
    w_in_b, w_out_b, w_glu_b = (w.astype(_BF16) for w in (w_in, w_out, ssm_w_glu))
    d_skip = ssm_d.reshape(DEPTH, 1, SSM_WIDTH)
    b_glu = ssm_b_glu.reshape(DEPTH, 1, SSM_WIDTH)

    xa = (ctx, x)
    seq = lambda t: t.reshape(BATCH, TOKENS, t.shape[-1])
    flat = lambda t: t.reshape(BATCH * TOKENS, t.shape[-1])
    for l in range(DEPTH):
        xa, q, k, v, u_ssm, gb, p = _pre(xa, mods, npre, npost, wg, wu, wd, w_in_b, cos_t, sin_t, l)
        attn = _attention(attn_sink[l], seq(q), seq(k), seq(v))
        yf, yb = _s5_scan(seq(u_ssm), bmat[l], cmat[l], a_bar[l])
        xa = _post(xa, mods, npre, npost, flat(attn), u_ssm, flat(yf), flat(yb), gb, p, d_skip, w_glu_b,
                   b_glu, conv_w, w_out_b, wg, wu, wd, l, latent_only=l == DEPTH - 1)
    return xa.reshape(BATCH, SEQ, D_MODEL)
```

```python
import functools
import math

import jax
import jax.numpy as jnp
import numpy as np
from jax import lax
from jax.experimental import pallas as pl
from jax.experimental.pallas import tpu as pltpu

D_MODEL = 1024
BATCH = 16
SEQ = 4096
DEPTH = 4
GRID_W = 64
CTX_LEN = 256
HEAD_DIM = 64
ATTN_WIDTH = D_MODEL // 2
N_HEADS = ATTN_WIDTH // HEAD_DIM
KV_HEADS = N_HEADS // 4
Q_PER_KV = N_HEADS // KV_HEADS
KV_WIDTH = KV_HEADS * HEAD_DIM
WINDOW = 128
ATTN_SCALE = HEAD_DIM ** -0.5
LOG2_E = math.log2(math.e)
ROPE_BASE = 10000.0
ROPE_PAIRS = HEAD_DIM // 4
SSM_WIDTH = D_MODEL // 4
SSM_GROUP = 16
SSM_GROUPS = SSM_WIDTH // SSM_GROUP
SSM_STATE = 64
CONV_WIDTH = D_MODEL // 4
IN_COLS = ATTN_WIDTH + 2 * KV_WIDTH + SSM_WIDTH + 3 * CONV_WIDTH
D_FF = ((8 * D_MODEL // 3 + 127) // 128) * 128
MACARON = 0.5
N_MOD = 9
EPS = 1e-6
NEG_INF = -1e30

TOKENS = CTX_LEN + SEQ
TILE = 256
N_TILES = TOKENS // TILE
MOD_ROWS = 24
CTX_ROW = BATCH
N_LOCAL = TILE + 2 * WINDOW
STATE_W = SSM_GROUPS * SSM_STATE
SCAN_T = 32
SCAN_ROWS = SCAN_T * BATCH
SCAN_LANES = 512
N_SCAN = TOKENS // SCAN_T
N_SCAN_CTX = CTX_LEN // SCAN_T
VMEM_LIMIT = 48 * 1024 * 1024

_F32 = jnp.float32
_BF16 = jnp.bfloat16


def _rms(x, g):
    return (x * lax.rsqrt(jnp.mean(x * x, axis=-1, keepdims=True) + EPS)) * g


def _modulated(x, g, shift, scale):
    return _rms(x, g) * (1 + scale) + shift


def _ada_kernel(cc_ref, w_ref, b_ref, o_ref):
    cc = cc_ref[...]
    s = cc * jax.nn.sigmoid(cc)
    o_ref[...] = jnp.dot(s, w_ref[...], precision=lax.Precision.HIGHEST,
                         preferred_element_type=_F32) + b_ref[...]


def _ada_table(cc, w_ada, b_ada):
    out = pl.pallas_call(
        _ada_kernel,
        grid=(DEPTH, N_MOD),
        in_specs=[
            pl.BlockSpec((MOD_ROWS, D_MODEL), lambda l, j: (0, 0)),
            pl.BlockSpec((None, D_MODEL, D_MODEL), lambda l, j: (l, 0, j)),
            pl.BlockSpec((None, None, 1, D_MODEL), lambda l, j: (l, j, 0, 0)),
        ],
        out_specs=pl.BlockSpec((None, None, MOD_ROWS, D_MODEL), lambda l, j: (l, j, 0, 0)),
        out_shape=jax.ShapeDtypeStruct((DEPTH, N_MOD, MOD_ROWS, D_MODEL), _F32),
        name="ada_table",
    )(cc, w_ada, b_ada.reshape(DEPTH, N_MOD, 1, D_MODEL))
    return out.reshape(DEPTH, N_MOD, MOD_ROWS, 1, D_MODEL)


def _stream_tile(p, h):
    return 2 * p + h


def _latent_tile(p, h):
    per_batch = (N_TILES - 1) // 2
    return (p // per_batch) * N_TILES + 1 + 2 * (p % per_batch) + h


def _row_spec(width, tile_fn, h):
    return pl.BlockSpec((TILE, width), lambda p: (tile_fn(p, h), 0))


def _mod_spec(layer, j, tile_fn, h):
    def index(p):
        t = tile_fn(p, h)
        return (layer, j, jnp.where(t % N_TILES == 0, CTX_ROW, t // N_TILES), 0, 0)
    return pl.BlockSpec((None, None, None, 1, D_MODEL), index)


def _const_spec(shape, index):
    return pl.BlockSpec(shape, lambda p: index, pipeline_mode=pl.Buffered(1))


def _norm_spec(layer, sub):
    return _const_spec((None, None, 1, D_MODEL), (layer, sub, 0, 0))


def _ffn_weight_specs(layer, which):
    return [
        _const_spec((None, None, D_MODEL, D_FF), (layer, which, 0, 0)),
        _const_spec((None, None, D_MODEL, D_FF), (layer, which, 0, 0)),
        _const_spec((None, None, D_FF, D_MODEL), (layer, which, 0, 0)),
    ]


_TOKEN_PARAMS = pltpu.CompilerParams(dimension_semantics=("parallel",), vmem_limit_bytes=VMEM_LIMIT)


def _swiglu(h, wg_ref, wu_ref, wd_ref):
    a = jnp.dot(h, wg_ref[...], preferred_element_type=_F32)
    b = jnp.dot(h, wu_ref[...], preferred_element_type=_F32)
    s = ((a * jax.nn.sigmoid(a)) * b).astype(_BF16)
    return jnp.dot(s, wd_ref[...], preferred_element_type=_F32)


def _ffn_pair(xs, mods, gpre, gpost, wg_ref, wu_ref, wd_ref):
    hs = [_modulated(x, gpre, sh[...], sc[...]).astype(_BF16) for x, (sh, sc, _) in zip(xs, mods)]
    ys = [_swiglu(h, wg_ref, wu_ref, wd_ref) for h in hs]
    return [x + MACARON * (gt[...] * _rms(y, gpost)) for x, y, (_, _, gt) in zip(xs, ys, mods)]


def _pre_kernel(*refs, split_input):
    n_x = 2 if split_input else 1
    halves = []
    for h in range(2):
        group, refs = refs[:n_x + 7], refs[n_x + 7:]
        halves.append(group)
    (gpre0_ref, gpost0_ref, gpre1_ref, wg_ref, wu_ref, wd_ref, w_ref,
     xo_ref, q_ref, k_ref, v_ref, u_ref, gb_ref, p_ref) = refs
    xs = []
    for h, group in enumerate(halves):
        if split_input:
            is_ctx = _stream_tile(pl.program_id(0), h) % N_TILES == 0
            xs.append(jnp.where(is_ctx, group[0][...], group[1][...]))
        else:
            xs.append(group[0][...])
    mods0 = [group[n_x:n_x + 3] for group in halves]
    xs = _ffn_pair(xs, mods0, gpre0_ref[...], gpost0_ref[...], wg_ref, wu_ref, wd_ref)
    for h, x in enumerate(xs):
        xo_ref[h * TILE:(h + 1) * TILE, :] = x
    hs = [_modulated(x, gpre1_ref[...], group[n_x + 3][...], group[n_x + 4][...]).astype(_BF16)
          for x, group in zip(xs, halves)]
    prs = [jnp.dot(h, w_ref[...], preferred_element_type=_F32) for h in hs]
    lane = lax.broadcasted_iota(jnp.int32, (TILE, 128), 1)
    first_half = (lane % (2 * ROPE_PAIRS)) < ROPE_PAIRS
    for h, (pr, group) in enumerate(zip(prs, halves)):
        rows = slice(h * TILE, (h + 1) * TILE)
        cosv = group[n_x + 5][...]
        sinv = group[n_x + 6][...]

        def rope(t, cosv=cosv, sinv=sinv):
            partner = jnp.where(first_half, pltpu.roll(t, 128 - ROPE_PAIRS, 1), pltpu.roll(t, ROPE_PAIRS, 1))
            return t * cosv + partner * sinv

        for j in range(ATTN_WIDTH // 128):
            q_ref[rows, j * 128:(j + 1) * 128] = (
                rope(pr[:, j * 128:(j + 1) * 128]) * (ATTN_SCALE * LOG2_E)).astype(_BF16)
        c = ATTN_WIDTH
        k_ref[rows, :] = rope(pr[:, c:c + KV_WIDTH]).astype(_BF16)
        c += KV_WIDTH
        ones = jnp.ones((TILE, HEAD_DIM), _F32)
        v_ref[rows, :] = jnp.concatenate(
            [piece for kv in range(KV_HEADS)
             for piece in (pr[:, c + kv * HEAD_DIM:c + (kv + 1) * HEAD_DIM], ones)], axis=1).astype(_BF16)
        c += KV_WIDTH
        u_ref[rows, :] = pr[:, c:c + SSM_WIDTH]
        c += SSM_WIDTH
        gb_ref[rows, :] = pr[:, c:c + CONV_WIDTH]
        c += CONV_WIDTH
        p_ref[rows, :] = pr[:, c:c + CONV_WIDTH] * pr[:, c + CONV_WIDTH:c + 2 * CONV_WIDTH]


def _pre(x_in, mods, norm_pre, norm_post, wg, wu, wd, w_in, cos_t, sin_t, layer):
    split_input = isinstance(x_in, tuple)
    tile = _stream_tile
    in_specs, args = [], []
    for h in range(2):
        if split_input:
            in_specs += [
                pl.BlockSpec((None, CTX_LEN, D_MODEL), lambda p, h=h: (tile(p, h) // N_TILES, 0, 0)),
                pl.BlockSpec((None, TILE, D_MODEL),
                             lambda p, h=h: (tile(p, h) // N_TILES, jnp.maximum(tile(p, h) % N_TILES - 1, 0), 0)),
            ]
            args += list(x_in)
        else:
            in_specs.append(_row_spec(D_MODEL, tile, h))
            args.append(x_in)
        in_specs += [_mod_spec(layer, j, tile, h) for j in range(5)]
        args += [mods] * 5
        in_specs += [pl.BlockSpec((TILE, 128), lambda p, h=h: (tile(p, h) % N_TILES, 0)) for _ in range(2)]
        args += [cos_t, sin_t]
    in_specs += [_norm_spec(layer, 0), _norm_spec(layer, 0), _norm_spec(layer, 1),
                 *_ffn_weight_specs(layer, 0), _const_spec((None, D_MODEL, IN_COLS), (layer, 0, 0))]
    args += [norm_pre, norm_post, norm_pre, wg, wu, wd, w_in]
    pair = lambda w: pl.BlockSpec((2 * TILE, w), lambda p: (p, 0))
    tok = lambda w, dt: jax.ShapeDtypeStruct((BATCH * TOKENS, w), dt)
    return pl.pallas_call(
        functools.partial(_pre_kernel, split_input=split_input),
        grid=(BATCH * N_TILES // 2,),
        in_specs=in_specs,
        out_specs=[pair(D_MODEL), pair(ATTN_WIDTH), pair(KV_WIDTH), pair(2 * KV_WIDTH),
                   pair(SSM_WIDTH), pair(CONV_WIDTH), pair(CONV_WIDTH)],
        out_shape=[
            tok(D_MODEL, _F32),
            tok(ATTN_WIDTH, _BF16), tok(KV_WIDTH, _BF16), tok(2 * KV_WIDTH, _BF16),
            tok(SSM_WIDTH, _F32), tok(CONV_WIDTH, _F32), tok(CONV_WIDTH, _F32),
        ],
        compiler_params=_TOKEN_PARAMS,
        name="ffn_in_proj",
    )(*args)


def _dot_nt(a, b):
    return lax.dot_general(a, b, (((1,), (1,)), ((), ())), preferred_element_type=_F32)


def _dot_tn(a, b):
    return lax.dot_general(a, b, (((0,), (0,)), ((), ())), preferred_element_type=_F32)


def _attn_kernel(sink_ref, q_ref, k_ref, v_ref, o_ref):
    i = pl.program_id(1)
    q0 = i * TILE
    start = pl.multiple_of(jnp.clip(q0 - WINDOW, CTX_LEN, TOKENS - N_LOCAL), WINDOW)
    k_loc = k_ref[pl.ds(start, N_LOCAL), :]
    v_loc = v_ref[pl.ds(start, N_LOCAL), :]
    k_ctx = k_ref[0:CTX_LEN, :]
    v_ctx = v_ref[0:CTX_LEN, :]
    kj = start + lax.broadcasted_iota(jnp.int32, (N_LOCAL, TILE), 0)
    qi = q0 + lax.broadcasted_iota(jnp.int32, (N_LOCAL, TILE), 1)
    valid = (jnp.abs(kj - qi) <= WINDOW) & (i >= 1)
    bias = jnp.where(valid, 0.0, NEG_INF).astype(_F32)
    q = q_ref[...]
    outs = []

    def scores(h):
        lo = (h // Q_PER_KV) * HEAD_DIM
        qh = q[:, h * HEAD_DIM:(h + 1) * HEAD_DIM]
        return (_dot_nt(k_loc[:, lo:lo + HEAD_DIM], qh) + bias,
                _dot_nt(k_ctx[:, lo:lo + HEAD_DIM], qh))

    nxt = scores(0)
    for h in range(N_HEADS):
        lo = (h // Q_PER_KV) * HEAD_DIM
        s_loc, s_ctx = nxt
        if h + 1 < N_HEADS:
            nxt = scores(h + 1)
        sink = sink_ref[h] * LOG2_E
        m = jnp.maximum(jnp.maximum(jnp.max(s_loc, axis=0, keepdims=True),
                                    jnp.max(s_ctx, axis=0, keepdims=True)), sink)
        p_loc = jnp.exp2((s_loc - m).astype(_BF16))
        p_ctx = jnp.exp2((s_ctx - m).astype(_BF16))
        vlo = 2 * lo
        ov = (_dot_tn(v_loc[:, vlo:vlo + 2 * HEAD_DIM], p_loc)
              + _dot_tn(v_ctx[:, vlo:vlo + 2 * HEAD_DIM], p_ctx))
        den = ov[HEAD_DIM:HEAD_DIM + 1, :] + jnp.exp2(sink - m)
        outs.append(ov[0:HEAD_DIM, :] / den)
        if h % 2 == 1:
            pair = jnp.concatenate(outs, axis=0).T.astype(_BF16)
            o_ref[:, (h - 1) * HEAD_DIM:(h + 1) * HEAD_DIM] = pair
            outs = []


def _attention(sink, q, k, v):
    seq_spec = pl.BlockSpec((None, TOKENS, KV_WIDTH), lambda b, i: (b, 0, 0))
    val_spec = pl.BlockSpec((None, TOKENS, 2 * KV_WIDTH), lambda b, i: (b, 0, 0))
    tile_spec = pl.BlockSpec((None, TILE, ATTN_WIDTH), lambda b, i: (b, i, 0))
    return pl.pallas_call(
        _attn_kernel,
        grid=(BATCH, N_TILES),
        in_specs=[
            pl.BlockSpec(memory_space=pltpu.SMEM),
            tile_spec, seq_spec, val_spec,
        ],
        out_specs=tile_spec,
        out_shape=jax.ShapeDtypeStruct((BATCH, TOKENS, ATTN_WIDTH), _BF16),
        compiler_params=pltpu.CompilerParams(
            dimension_semantics=("parallel", "parallel"), vmem_limit_bytes=VMEM_LIMIT),
        name="window_attn",
    )(sink, q, k, v)


def _s5_kernel(uf_ref, ub_ref, perm_ref, perm_t_ref, bmat_ref, cmat_ref, a_ref, yf_ref, yb_ref,
               buf_f_ref, buf_b_ref, state_ref):
    @pl.when(pl.program_id(0) == 0)
    def _():
        state_ref[...] = jnp.zeros_like(state_ref)

    dirs = ((uf_ref, yf_ref, buf_f_ref), (ub_ref, yb_ref, buf_b_ref))
    half = SCAN_ROWS // 2
    for d, (u_ref, _, buf_ref) in enumerate(dirs):
        u = u_ref[...].reshape(SCAN_ROWS, SSM_WIDTH).astype(_BF16)
        u_tb = jnp.dot(perm_ref[...], u, preferred_element_type=_F32).astype(_BF16)
        buf_ref[...] = jnp.dot(u_tb, bmat_ref[d], preferred_element_type=_F32)
    for d, (_, y_ref, buf_ref) in enumerate(dirs):
        for c in range(STATE_W // SCAN_LANES):
            re = slice(c * SCAN_LANES, (c + 1) * SCAN_LANES)
            im = slice(STATE_W + c * SCAN_LANES, STATE_W + (c + 1) * SCAN_LANES)
            a_r = jnp.broadcast_to(a_ref[d, 0:1, re], (BATCH, SCAN_LANES))
            a_i = jnp.broadcast_to(a_ref[d, 1:2, re], (BATCH, SCAN_LANES))
            s_r = state_ref[d, :, re]
            s_i = state_ref[d, :, im]
            for t in (range(SCAN_T) if d == 0 else reversed(range(SCAN_T))):
                rows = slice(t * BATCH, (t + 1) * BATCH)
                s_r, s_i = (a_r * s_r - a_i * s_i + buf_ref[rows, re],
                            a_r * s_i + a_i * s_r + buf_ref[rows, im])
                buf_ref[rows, re] = s_r
                buf_ref[rows, im] = s_i
            state_ref[d, :, re] = s_r
            state_ref[d, :, im] = s_i
        y_tb = jnp.concatenate(
            [jnp.dot(buf_ref[r:r + half, :].astype(_BF16), cmat_ref[d], preferred_element_type=_F32)
             for r in (0, half)], axis=0).astype(_BF16)
        y = jnp.concatenate(
            [jnp.dot(perm_t_ref[r:r + half, :], y_tb, preferred_element_type=_F32) for r in (0, half)],
            axis=0)
        y_ref[...] = y.astype(_BF16).reshape(BATCH, SCAN_T, SSM_WIDTH)


def _backward_chunk(i):
    return jnp.where(i < N_SCAN_CTX, N_SCAN_CTX - 1 - i, N_SCAN - 1 - i + N_SCAN_CTX)


def _scan_permutation():
    r = np.arange(SCAN_ROWS)
    perm = np.zeros((SCAN_ROWS, SCAN_ROWS), np.float32)
    perm[r, (r % BATCH) * SCAN_T + r // BATCH] = 1.0
    return jnp.asarray(perm, _BF16), jnp.asarray(perm.T, _BF16)


def _s5_scan(u, bmat, cmat, a_bar):
    chunk = (BATCH, SCAN_T, SSM_WIDTH)
    fwd = pl.BlockSpec(chunk, lambda i: (0, i, 0))
    bwd = pl.BlockSpec(chunk, lambda i: (0, _backward_chunk(i), 0))
    whole = lambda a: pl.BlockSpec(a.shape, lambda i: (0,) * a.ndim, pipeline_mode=pl.Buffered(1))
    perm, perm_t = _scan_permutation()
    y_shape = jax.ShapeDtypeStruct((BATCH, TOKENS, SSM_WIDTH), _BF16)
    return pl.pallas_call(
        _s5_kernel,
        grid=(N_SCAN,),
        in_specs=[fwd, bwd, whole(perm), whole(perm_t), whole(bmat), whole(cmat), whole(a_bar)],
        out_specs=[fwd, bwd],
        out_shape=[y_shape, y_shape],
        scratch_shapes=[
            pltpu.VMEM((SCAN_ROWS, 2 * STATE_W), _F32),
            pltpu.VMEM((SCAN_ROWS, 2 * STATE_W), _F32),
            pltpu.VMEM((2, BATCH, 2 * STATE_W), _F32),
        ],
        compiler_params=pltpu.CompilerParams(
            dimension_semantics=("arbitrary",), vmem_limit_bytes=VMEM_LIMIT),
        name="s5_scan",
    )(u, u, perm, perm_t, bmat, cmat, a_bar)


N_ROW_INPUTS = 7


def _post_kernel(*refs, tile_fn):
    halves = []
    for h in range(2):
        group, refs = refs[:N_ROW_INPUTS + 4], refs[N_ROW_INPUTS + 4:]
        halves.append(group)
    (pprev_ref, pnext_ref, gpost1_ref, gpre2_ref, gpost2_ref, d_ref, wglu_ref, bglu_ref, cw_ref,
     wout_ref, wg_ref, wu_ref, wd_ref, o_ref) = refs
    tiles = [tile_fn(pl.program_id(0), h) % N_TILES for h in range(2)]
    ps = [group[6][...] for group in halves]
    edge_before = [jnp.where(tiles[0] >= 2, pprev_ref[7:8, :], 0.0),
                   jnp.where(tiles[1] >= 2, ps[0][TILE - 1:TILE, :], 0.0)]
    edge_after = [jnp.where((tiles[0] >= 1) & (tiles[0] <= N_TILES - 2), ps[1][0:1, :], 0.0),
                  jnp.where((tiles[1] >= 1) & (tiles[1] <= N_TILES - 2), pnext_ref[0:1, :], 0.0)]
    row = lax.broadcasted_iota(jnp.int32, (TILE, CONV_WIDTH), 0)
    cw = cw_ref[...]

    def mixed_and_modulated(h):
        group = halves[h]
        x_ref, attn_ref, u_ref, yf_ref, yb_ref, gb_ref, _ = group[:N_ROW_INPUTS]
        gt1_ref, sh2_ref, sc2_ref, _ = group[N_ROW_INPUTS:]
        y = d_ref[...] * u_ref[...] + yf_ref[...] + yb_ref[...]
        g = jax.nn.gelu(y)
        ssm = g * jax.nn.sigmoid(
            jnp.dot(g.astype(_BF16), wglu_ref[...], preferred_element_type=_F32) + bglu_ref[...])
        p = ps[h]
        p_before = jnp.where(row == 0, edge_before[h], pltpu.roll(p, 1, 0))
        p_after = jnp.where(row == TILE - 1, edge_after[h], pltpu.roll(p, TILE - 1, 0))
        conv = gb_ref[...] * (p_before * cw[0:1, :] + p * cw[1:2, :] + p_after * cw[2:3, :])
        mix = jnp.concatenate([attn_ref[...], ssm.astype(_BF16), conv.astype(_BF16)], axis=1)
        yv = jnp.dot(mix, wout_ref[...], preferred_element_type=_F32)
        x = x_ref[...] + gt1_ref[...] * _rms(yv, gpost1_ref[...])
        return x, _modulated(x, gpre2_ref[...], sh2_ref[...], sc2_ref[...]).astype(_BF16)

    (x0, h0), (x1, h1) = mixed_and_modulated(0), mixed_and_modulated(1)
    y0 = _swiglu(h0, wg_ref, wu_ref, wd_ref)
    y1 = _swiglu(h1, wg_ref, wu_ref, wd_ref)
    for h, (x, y) in enumerate(((x0, y0), (x1, y1))):
        gt2_ref = halves[h][N_ROW_INPUTS + 3]
        o_ref[h * TILE:(h + 1) * TILE, :] = x + MACARON * (gt2_ref[...] * _rms(y, gpost2_ref[...]))


def _post(x_all, mods, norm_pre, norm_post, attn, u_ssm, yf, yb, gb, p, ssm_d, w_glu, b_glu, conv_w,
          w_out, wg, wu, wd, layer, latent_only):
    tile = _latent_tile if latent_only else _stream_tile
    n_pairs = BATCH * (N_TILES - 1) // 2 if latent_only else BATCH * N_TILES // 2
    rows8 = TILE // 8
    in_specs, args = [], []
    for h in range(2):
        for arr in (x_all, attn, u_ssm, yf, yb, gb, p):
            in_specs.append(_row_spec(arr.shape[1], tile, h))
            args.append(arr)
        in_specs += [_mod_spec(layer, j, tile, h) for j in (5, 6, 7, 8)]
        args += [mods] * 4
    in_specs += [
        pl.BlockSpec((8, CONV_WIDTH), lambda p_: (jnp.maximum(tile(p_, 0) * rows8 - 1, 0), 0)),
        pl.BlockSpec((8, CONV_WIDTH),
                     lambda p_: (jnp.minimum((tile(p_, 1) + 1) * rows8, BATCH * TOKENS // 8 - 1), 0)),
        _norm_spec(layer, 1), _norm_spec(layer, 2), _norm_spec(layer, 2),
        _const_spec((None, 1, SSM_WIDTH), (layer, 0, 0)),
        _const_spec((None, SSM_WIDTH, SSM_WIDTH), (layer, 0, 0)),
        _const_spec((None, 1, SSM_WIDTH), (layer, 0, 0)),
        _const_spec((None, 3, CONV_WIDTH), (layer, 0, 0)),
        _const_spec((None, D_MODEL, D_MODEL), (layer, 0, 0)),
        *_ffn_weight_specs(layer, 1),
    ]
    args += [p, p, norm_post, norm_pre, norm_post, ssm_d, w_glu, b_glu, conv_w, w_out, wg, wu, wd]
    return pl.pallas_call(
        functools.partial(_post_kernel, tile_fn=tile),
        grid=(n_pairs,),
        in_specs=in_specs,
        out_specs=pl.BlockSpec((2 * TILE, D_MODEL), lambda p_: (p_, 0)),
        out_shape=jax.ShapeDtypeStruct((n_pairs * 2 * TILE, D_MODEL), _F32),
        compiler_params=_TOKEN_PARAMS,
        name="mix_out_ffn",
    )(*args)


def _rope_tables():
    pos = jnp.arange(SEQ)
    row = (pos // GRID_W).astype(_F32)
    col = (pos % GRID_W).astype(_F32)
    inv_freq = ROPE_BASE ** (-jnp.arange(ROPE_PAIRS, dtype=_F32) / ROPE_PAIRS)
    ang = jnp.stack([row[:, None] * inv_freq, col[:, None] * inv_freq], axis=1)
    cos, sin = jnp.cos(ang), jnp.sin(ang)
    cos_h = jnp.concatenate([cos, cos], axis=-1).reshape(SEQ, HEAD_DIM)
    sin_h = jnp.concatenate([-sin, sin], axis=-1).reshape(SEQ, HEAD_DIM)
    cos_t = jnp.concatenate([jnp.ones((CTX_LEN, HEAD_DIM), _F32), cos_h], axis=0)
    sin_t = jnp.concatenate([jnp.zeros((CTX_LEN, HEAD_DIM), _F32), sin_h], axis=0)
    return jnp.tile(cos_t, (1, 128 // HEAD_DIM)), jnp.tile(sin_t, (1, 128 // HEAD_DIM))


def _zoh(lam_re, lam_im, log_step, b_re, b_im):
    dt = jnp.exp(log_step)[..., None]
    mag = jnp.exp(lam_re * dt)
    ar, ai = mag * jnp.cos(lam_im * dt), mag * jnp.sin(lam_im * dt)
    den = lam_re * lam_re + lam_im * lam_im
    gr = ((ar - 1) * lam_re + ai * lam_im) / den
    gi = (ai * lam_re - (ar - 1) * lam_im) / den
    bbr = gr[..., None] * b_re - gi[..., None] * b_im
    bbi = gr[..., None] * b_im + gi[..., None] * b_re
    return ar, ai, bbr, bbi


def _s5_matrices(lam_re, lam_im, log_step, b_re, b_im, c_re, c_im):
    ar, ai, bbr, bbi = _zoh(lam_re, lam_im, log_step, b_re, b_im)
    eye = jnp.eye(SSM_GROUPS, dtype=_F32)
    lead = (DEPTH, 2)
    blk_in = lambda m: jnp.einsum('ldgph,gk->ldghkp', m, eye).reshape(lead + (SSM_WIDTH, STATE_W))
    blk_out = lambda m: jnp.einsum('ldghp,gk->ldkpgh', m, eye).reshape(lead + (STATE_W, SSM_WIDTH))
    bmat = jnp.concatenate([blk_in(bbr), blk_in(bbi)], axis=-1).astype(_BF16)
    cmat = jnp.concatenate([blk_out(c_re), -blk_out(c_im)], axis=-2).astype(_BF16)
    a_bar = jnp.stack([ar.reshape(lead + (STATE_W,)), ai.reshape(lead + (STATE_W,))], axis=2)
    return bmat, cmat, a_bar


def kernel(x, c, ctx, c_ctx, w_ada, b_ada, norm_pre, norm_post, ffn_w_gate, ffn_w_up, ffn_w_down,
           w_in, w_out, attn_sink, ssm_lambda_re, ssm_lambda_im, ssm_log_step, ssm_b_re, ssm_b_im,
           ssm_c_re, ssm_c_im, ssm_d, ssm_w_glu, ssm_b_glu, conv_w):
    cc = jnp.concatenate([c, c_ctx[None, :], jnp.zeros((MOD_ROWS - BATCH - 1, D_MODEL), _F32)], axis=0)
    mods = _ada_table(cc, w_ada, b_ada)
    cos_t, sin_t = _rope_tables()
    bmat, cmat, a_bar = _s5_matrices(ssm_lambda_re, ssm_lambda_im, ssm_log_step,
                                     ssm_b_re, ssm_b_im, ssm_c_re, ssm_c_im)
    npre = norm_pre.reshape(DEPTH, 3, 1, D_MODEL)
    npost = norm_post.reshape(DEPTH, 3, 1, D_MODEL)
    wg, wu, wd = (w.astype(_BF16) for w in (ffn_w_gate, ffn_w_up, ffn_w_down))
    w_in_b, w_out_b, w_glu_b = (w.astype(_BF16) for w in (w_in, w_out, ssm_w_glu))
    d_skip = ssm_d.reshape(DEPTH, 1, SSM_WIDTH)
    b_glu = ssm_b_glu.reshape(DEPTH, 1, SSM_WIDTH)

    xa = (ctx, x)
    seq = lambda t: t.reshape(BATCH, TOKENS, t.shape[-1])
    flat = lambda t: t.reshape(BATCH * TOKENS, t.shape[-1])
    for l in range(DEPTH):
        xa, q, k, v, u_ssm, gb, p = _pre(xa, mods, npre, npost, wg, wu, wd, w_in_b, cos_t, sin_t, l)
        attn = _attention(attn_sink[l], seq(q), seq(k), seq(v))
        yf, yb = _s5_scan(seq(u_ssm), bmat[l], cmat[l], a_bar[l])
        xa = _post(xa, mods, npre, npost, flat(attn), u_ssm, flat(yf), flat(yb), gb, p, d_skip, w_glu_b,
                   b_glu, conv_w, w_out_b, wg, wu, wd, l, latent_only=l == DEPTH - 1)
    return xa.reshape(BATCH, SEQ, D_MODEL)
```

```python
import functools
import math

import jax
import jax.numpy as jnp
from jax import lax
from jax.experimental import pallas as pl
from jax.experimental.pallas import tpu as pltpu

D_MODEL = 1024
BATCH = 16
SEQ = 4096
DEPTH = 4
GRID_W = 64
CTX_LEN = 256
HEAD_DIM = 64
ATTN_WIDTH = D_MODEL // 2
N_HEADS = ATTN_WIDTH // HEAD_DIM
KV_HEADS = N_HEADS // 4
Q_PER_KV = N_HEADS // KV_HEADS
KV_WIDTH = KV_HEADS * HEAD_DIM
WINDOW = 128
ATTN_SCALE = HEAD_DIM ** -0.5
LOG2_E = math.log2(math.e)
ROPE_BASE = 10000.0
ROPE_PAIRS = HEAD_DIM // 4
SSM_WIDTH = D_MODEL // 4
SSM_GROUP = 16
SSM_GROUPS = SSM_WIDTH // SSM_GROUP
SSM_STATE = 64
CONV_WIDTH = D_MODEL // 4
IN_COLS = ATTN_WIDTH + 2 * KV_WIDTH + SSM_WIDTH + 3 * CONV_WIDTH
D_FF = ((8 * D_MODEL // 3 + 127) // 128) * 128
MACARON = 0.5
N_MOD = 9
EPS = 1e-6
NEG_INF = -1e30

TOKENS = CTX_LEN + SEQ
TILE = 256
N_TILES = TOKENS // TILE
MOD_ROWS = 24
CTX_ROW = BATCH
N_LOCAL = TILE + 2 * WINDOW
VMEM_LIMIT = 48 * 1024 * 1024

_F32 = jnp.float32
_BF16 = jnp.bfloat16


def _rms(x, g):
    return (x * lax.rsqrt(jnp.mean(x * x, axis=-1, keepdims=True) + EPS)) * g


def _modulated(x, g, shift, scale):
    return _rms(x, g) * (1 + scale) + shift


def _ada_kernel(cc_ref, w_ref, b_ref, o_ref):
    cc = cc_ref[...]
    s = cc * jax.nn.sigmoid(cc)
    o_ref[...] = jnp.dot(s, w_ref[...], precision=lax.Precision.HIGHEST,
                         preferred_element_type=_F32) + b_ref[...]


def _ada_table(cc, w_ada, b_ada):
    out = pl.pallas_call(
        _ada_kernel,
        grid=(DEPTH, N_MOD),
        in_specs=[
            pl.BlockSpec((MOD_ROWS, D_MODEL), lambda l, j: (0, 0)),
            pl.BlockSpec((None, D_MODEL, D_MODEL), lambda l, j: (l, 0, j)),
            pl.BlockSpec((None, None, 1, D_MODEL), lambda l, j: (l, j, 0, 0)),
        ],
        out_specs=pl.BlockSpec((None, None, MOD_ROWS, D_MODEL), lambda l, j: (l, j, 0, 0)),
        out_shape=jax.ShapeDtypeStruct((DEPTH, N_MOD, MOD_ROWS, D_MODEL), _F32),
        name="ada_table",
    )(cc, w_ada, b_ada.reshape(DEPTH, N_MOD, 1, D_MODEL))
    return out.reshape(DEPTH, N_MOD, MOD_ROWS, 1, D_MODEL)


def _stream_tile(p, h):
    return 2 * p + h


def _latent_tile(p, h):
    per_batch = (N_TILES - 1) // 2
    return (p // per_batch) * N_TILES + 1 + 2 * (p % per_batch) + h


def _row_spec(width, tile_fn, h):
    return pl.BlockSpec((TILE, width), lambda p: (tile_fn(p, h), 0))


def _mod_spec(layer, j, tile_fn, h):
    def index(p):
        t = tile_fn(p, h)
        return (layer, j, jnp.where(t % N_TILES == 0, CTX_ROW, t // N_TILES), 0, 0)
    return pl.BlockSpec((None, None, None, 1, D_MODEL), index)


def _const_spec(shape, index):
    return pl.BlockSpec(shape, lambda p: index, pipeline_mode=pl.Buffered(1))


def _norm_spec(layer, sub):
    return _const_spec((None, None, 1, D_MODEL), (layer, sub, 0, 0))


def _ffn_weight_specs(layer, which):
    return [
        _const_spec((None, None, D_MODEL, D_FF), (layer, which, 0, 0)),
        _const_spec((None, None, D_MODEL, D_FF), (layer, which, 0, 0)),
        _const_spec((None, None, D_FF, D_MODEL), (layer, which, 0, 0)),
    ]


_TOKEN_PARAMS = pltpu.CompilerParams(dimension_semantics=("parallel",), vmem_limit_bytes=VMEM_LIMIT)


def _swiglu(h, wg_ref, wu_ref, wd_ref):
    a = jnp.dot(h, wg_ref[...], preferred_element_type=_F32)
    b = jnp.dot(h, wu_ref[...], preferred_element_type=_F32)
    s = ((a * jax.nn.sigmoid(a)) * b).astype(_BF16)
    return jnp.dot(s, wd_ref[...], preferred_element_type=_F32)


def _ffn_pair(xs, mods, gpre, gpost, wg_ref, wu_ref, wd_ref):
    hs = [_modulated(x, gpre, sh[...], sc[...]).astype(_BF16) for x, (sh, sc, _) in zip(xs, mods)]
    ys = [_swiglu(h, wg_ref, wu_ref, wd_ref) for h in hs]
    return [x + MACARON * (gt[...] * _rms(y, gpost)) for x, y, (_, _, gt) in zip(xs, ys, mods)]


def _pre_kernel(*refs, split_input):
    n_x = 2 if split_input else 1
    halves = []
    for h in range(2):
        group, refs = refs[:n_x + 7], refs[n_x + 7:]
        halves.append(group)
    (gpre0_ref, gpost0_ref, gpre1_ref, wg_ref, wu_ref, wd_ref, w_ref,
     xo_ref, q_ref, k_ref, v_ref, u_ref, gb_ref, p_ref) = refs
    xs = []
    for h, group in enumerate(halves):
        if split_input:
            is_ctx = _stream_tile(pl.program_id(0), h) % N_TILES == 0
            xs.append(jnp.where(is_ctx, group[0][...], group[1][...]))
        else:
            xs.append(group[0][...])
    mods0 = [group[n_x:n_x + 3] for group in halves]
    xs = _ffn_pair(xs, mods0, gpre0_ref[...], gpost0_ref[...], wg_ref, wu_ref, wd_ref)
    for h, x in enumerate(xs):
        xo_ref[h * TILE:(h + 1) * TILE, :] = x
    hs = [_modulated(x, gpre1_ref[...], group[n_x + 3][...], group[n_x + 4][...]).astype(_BF16)
          for x, group in zip(xs, halves)]
    prs = [jnp.dot(h, w_ref[...], preferred_element_type=_F32) for h in hs]
    lane = lax.broadcasted_iota(jnp.int32, (TILE, 128), 1)
    first_half = (lane % (2 * ROPE_PAIRS)) < ROPE_PAIRS
    for h, (pr, group) in enumerate(zip(prs, halves)):
        rows = slice(h * TILE, (h + 1) * TILE)
        cosv = group[n_x + 5][...]
        sinv = group[n_x + 6][...]

        def rope(t, cosv=cosv, sinv=sinv):
            partner = jnp.where(first_half, pltpu.roll(t, 128 - ROPE_PAIRS, 1), pltpu.roll(t, ROPE_PAIRS, 1))
            return t * cosv + partner * sinv

        for j in range(ATTN_WIDTH // 128):
            q_ref[rows, j * 128:(j + 1) * 128] = (
                rope(pr[:, j * 128:(j + 1) * 128]) * (ATTN_SCALE * LOG2_E)).astype(_BF16)
        c = ATTN_WIDTH
        k_ref[rows, :] = rope(pr[:, c:c + KV_WIDTH]).astype(_BF16)
        c += KV_WIDTH
        ones = jnp.ones((TILE, HEAD_DIM), _F32)
        v_ref[rows, :] = jnp.concatenate(
            [piece for kv in range(KV_HEADS)
             for piece in (pr[:, c + kv * HEAD_DIM:c + (kv + 1) * HEAD_DIM], ones)], axis=1).astype(_BF16)
        c += KV_WIDTH
        u_ref[rows, :] = pr[:, c:c + SSM_WIDTH]
        c += SSM_WIDTH
        gb_ref[rows, :] = pr[:, c:c + CONV_WIDTH]
        c += CONV_WIDTH
        p_ref[rows, :] = pr[:, c:c + CONV_WIDTH] * pr[:, c + CONV_WIDTH:c + 2 * CONV_WIDTH]


def _pre(x_in, mods, norm_pre, norm_post, wg, wu, wd, w_in, cos_t, sin_t, layer):
    split_input = isinstance(x_in, tuple)
    tile = _stream_tile
    in_specs, args = [], []
    for h in range(2):
        if split_input:
            in_specs += [
                pl.BlockSpec((None, CTX_LEN, D_MODEL), lambda p, h=h: (tile(p, h) // N_TILES, 0, 0)),
                pl.BlockSpec((None, TILE, D_MODEL),
                             lambda p, h=h: (tile(p, h) // N_TILES, jnp.maximum(tile(p, h) % N_TILES - 1, 0), 0)),
            ]
            args += list(x_in)
        else:
            in_specs.append(_row_spec(D_MODEL, tile, h))
            args.append(x_in)
        in_specs += [_mod_spec(layer, j, tile, h) for j in range(5)]
        args += [mods] * 5
        in_specs += [pl.BlockSpec((TILE, 128), lambda p, h=h: (tile(p, h) % N_TILES, 0)) for _ in range(2)]
        args += [cos_t, sin_t]
    in_specs += [_norm_spec(layer, 0), _norm_spec(layer, 0), _norm_spec(layer, 1),
                 *_ffn_weight_specs(layer, 0), _const_spec((None, D_MODEL, IN_COLS), (layer, 0, 0))]
    args += [norm_pre, norm_post, norm_pre, wg, wu, wd, w_in]
    pair = lambda w: pl.BlockSpec((2 * TILE, w), lambda p: (p, 0))
    tok = lambda w, dt: jax.ShapeDtypeStruct((BATCH * TOKENS, w), dt)
    return pl.pallas_call(
        functools.partial(_pre_kernel, split_input=split_input),
        grid=(BATCH * N_TILES // 2,),
        in_specs=in_specs,
        out_specs=[pair(D_MODEL), pair(ATTN_WIDTH), pair(KV_WIDTH), pair(2 * KV_WIDTH),
                   pair(SSM_WIDTH), pair(CONV_WIDTH), pair(CONV_WIDTH)],
        out_shape=[
            tok(D_MODEL, _F32),
            tok(ATTN_WIDTH, _BF16), tok(KV_WIDTH, _BF16), tok(2 * KV_WIDTH, _BF16),
            tok(SSM_WIDTH, _F32), tok(CONV_WIDTH, _F32), tok(CONV_WIDTH, _F32),
        ],
        compiler_params=_TOKEN_PARAMS,
        name="ffn_in_proj",
    )(*args)


def _dot_nt(a, b):
    return lax.dot_general(a, b, (((1,), (1,)), ((), ())), preferred_element_type=_F32)


def _dot_tn(a, b):
    return lax.dot_general(a, b, (((0,), (0,)), ((), ())), preferred_element_type=_F32)


def _attn_kernel(sink_ref, q_ref, k_ref, v_ref, o_ref):
    i = pl.program_id(1)
    q0 = i * TILE
    start = pl.multiple_of(jnp.clip(q0 - WINDOW, CTX_LEN, TOKENS - N_LOCAL), WINDOW)
    k_loc = k_ref[pl.ds(start, N_LOCAL), :]
    v_loc = v_ref[pl.ds(start, N_LOCAL), :]
    k_ctx = k_ref[0:CTX_LEN, :]
    v_ctx = v_ref[0:CTX_LEN, :]
    kj = start + lax.broadcasted_iota(jnp.int32, (N_LOCAL, TILE), 0)
    qi = q0 + lax.broadcasted_iota(jnp.int32, (N_LOCAL, TILE), 1)
    valid = (jnp.abs(kj - qi) <= WINDOW) & (i >= 1)
    bias = jnp.where(valid, 0.0, NEG_INF).astype(_F32)
    q = q_ref[...]
    outs = []

    def scores(h):
        lo = (h // Q_PER_KV) * HEAD_DIM
        qh = q[:, h * HEAD_DIM:(h + 1) * HEAD_DIM]
        return (_dot_nt(k_loc[:, lo:lo + HEAD_DIM], qh) + bias,
                _dot_nt(k_ctx[:, lo:lo + HEAD_DIM], qh))

    nxt = scores(0)
    for h in range(N_HEADS):
        lo = (h // Q_PER_KV) * HEAD_DIM
        s_loc, s_ctx = nxt
        if h + 1 < N_HEADS:
            nxt = scores(h + 1)
        sink = sink_ref[h] * LOG2_E
        m = jnp.maximum(jnp.maximum(jnp.max(s_loc, axis=0, keepdims=True),
                                    jnp.max(s_ctx, axis=0, keepdims=True)), sink)
        p_loc = jnp.exp2((s_loc - m).astype(_BF16))
        p_ctx = jnp.exp2((s_ctx - m).astype(_BF16))
        vlo = 2 * lo
        ov = (_dot_tn(v_loc[:, vlo:vlo + 2 * HEAD_DIM], p_loc)
              + _dot_tn(v_ctx[:, vlo:vlo + 2 * HEAD_DIM], p_ctx))
        den = ov[HEAD_DIM:HEAD_DIM + 1, :] + jnp.exp2(sink - m)
        outs.append(ov[0:HEAD_DIM, :] / den)
        if h % 2 == 1:
            pair = jnp.concatenate(outs, axis=0).T.astype(_BF16)
            o_ref[:, (h - 1) * HEAD_DIM:(h + 1) * HEAD_DIM] = pair
            outs = []


def _attention(sink, q, k, v):
    seq_spec = pl.BlockSpec((None, TOKENS, KV_WIDTH), lambda b, i: (b, 0, 0))
    val_spec = pl.BlockSpec((None, TOKENS, 2 * KV_WIDTH), lambda b, i: (b, 0, 0))
    tile_spec = pl.BlockSpec((None, TILE, ATTN_WIDTH), lambda b, i: (b, i, 0))
    return pl.pallas_call(
        _attn_kernel,
        grid=(BATCH, N_TILES),
        in_specs=[
            pl.BlockSpec(memory_space=pltpu.SMEM),
            tile_spec, seq_spec, val_spec,
        ],
        out_specs=tile_spec,
        out_shape=jax.ShapeDtypeStruct((BATCH, TOKENS, ATTN_WIDTH), _BF16),
        compiler_params=pltpu.CompilerParams(
            dimension_semantics=("parallel", "parallel"), vmem_limit_bytes=VMEM_LIMIT),
        name="window_attn",
    )(sink, q, k, v)


SSM_CHUNK = 256 // SSM_GROUP
N_CHUNKS = TOKENS // SSM_CHUNK
N_CTX_CHUNKS = CTX_LEN // SSM_CHUNK
CHUNK_ROWS = N_CHUNKS * BATCH
STATE_LANES = 2 * SSM_STATE


def _s5_kernel(u_ref, w1_ref, w2_ref, a_ref, y_ref, xw_ref, s_ref):
    xw_ref[...] = jnp.dot(u_ref[...], w1_ref[...], preferred_element_type=_F32)
    a_r = jnp.broadcast_to(a_ref[0:1, :], (BATCH, STATE_LANES))
    a_i = jnp.broadcast_to(a_ref[1:2, :], (BATCH, STATE_LANES))
    fwd_lanes = lax.broadcasted_iota(jnp.int32, (BATCH, STATE_LANES), 1) < SSM_STATE
    x_re, x_im = SSM_WIDTH, SSM_WIDTH + STATE_LANES

    def body(i, carry):
        s_r, s_i = carry
        cb = jnp.where(i < N_CTX_CHUNKS, N_CTX_CHUNKS - 1 - i, N_CHUNKS + N_CTX_CHUNKS - 1 - i)
        rf = pl.ds(pl.multiple_of(i * BATCH, BATCH), BATCH)
        rb = pl.ds(pl.multiple_of(cb * BATCH, BATCH), BATCH)
        s_ref[rf, 0:SSM_STATE] = s_r[:, 0:SSM_STATE]
        s_ref[rb, SSM_STATE:STATE_LANES] = s_r[:, SSM_STATE:STATE_LANES]
        s_ref[rf, STATE_LANES:STATE_LANES + SSM_STATE] = s_i[:, 0:SSM_STATE]
        s_ref[rb, STATE_LANES + SSM_STATE:2 * STATE_LANES] = s_i[:, SSM_STATE:STATE_LANES]
        xr = jnp.where(fwd_lanes, xw_ref[rf, x_re:x_re + STATE_LANES], xw_ref[rb, x_re:x_re + STATE_LANES])
        xi = jnp.where(fwd_lanes, xw_ref[rf, x_im:x_im + STATE_LANES], xw_ref[rb, x_im:x_im + STATE_LANES])
        return a_r * s_r - a_i * s_i + xr, a_r * s_i + a_i * s_r + xi

    zero = jnp.zeros((BATCH, STATE_LANES), _F32)
    lax.fori_loop(0, N_CHUNKS, body, (zero, zero), unroll=8)
    half = CHUNK_ROWS // 2
    for r in (0, half):
        y = xw_ref[r:r + half, 0:SSM_WIDTH] + jnp.dot(
            s_ref[r:r + half, :].astype(_BF16), w2_ref[...], preferred_element_type=_F32)
        y_ref[r:r + half, :] = y.astype(_BF16)


def _s5_mixer(u, w1, w2, a_pow):
    u5 = u.astype(_BF16).reshape(BATCH, N_CHUNKS, SSM_CHUNK, SSM_GROUPS, SSM_GROUP)
    ug = u5.transpose(3, 1, 0, 2, 4).reshape(SSM_GROUPS, CHUNK_ROWS, SSM_WIDTH)
    per_group = lambda a: pl.BlockSpec((None,) + a.shape[1:], lambda g: (g,) + (0,) * (a.ndim - 1))
    yg = pl.pallas_call(
        _s5_kernel,
        grid=(SSM_GROUPS,),
        in_specs=[per_group(ug), per_group(w1), per_group(w2), per_group(a_pow)],
        out_specs=per_group(ug),
        out_shape=jax.ShapeDtypeStruct(ug.shape, _BF16),
        scratch_shapes=[
            pltpu.VMEM((CHUNK_ROWS, SSM_WIDTH + 2 * STATE_LANES), _F32),
            pltpu.VMEM((CHUNK_ROWS, 2 * STATE_LANES), _F32),
        ],
        compiler_params=pltpu.CompilerParams(
            dimension_semantics=("parallel",), vmem_limit_bytes=VMEM_LIMIT),
        name="s5_chunked",
    )(ug, w1, w2, a_pow)
    y5 = yg.reshape(SSM_GROUPS, N_CHUNKS, BATCH, SSM_CHUNK, SSM_GROUP)
    return y5.transpose(2, 1, 3, 0, 4).reshape(BATCH * TOKENS, SSM_WIDTH)


N_ROW_INPUTS = 6


def _post_kernel(*refs, tile_fn):
    halves = []
    for h in range(2):
        group, refs = refs[:N_ROW_INPUTS + 4], refs[N_ROW_INPUTS + 4:]
        halves.append(group)
    (pprev_ref, pnext_ref, gpost1_ref, gpre2_ref, gpost2_ref, d_ref, wglu_ref, bglu_ref, cw_ref,
     wout_ref, wg_ref, wu_ref, wd_ref, o_ref) = refs
    tiles = [tile_fn(pl.program_id(0), h) % N_TILES for h in range(2)]
    ps = [group[5][...] for group in halves]
    edge_before = [jnp.where(tiles[0] >= 2, pprev_ref[7:8, :], 0.0),
                   jnp.where(tiles[1] >= 2, ps[0][TILE - 1:TILE, :], 0.0)]
    edge_after = [jnp.where((tiles[0] >= 1) & (tiles[0] <= N_TILES - 2), ps[1][0:1, :], 0.0),
                  jnp.where((tiles[1] >= 1) & (tiles[1] <= N_TILES - 2), pnext_ref[0:1, :], 0.0)]
    row = lax.broadcasted_iota(jnp.int32, (TILE, CONV_WIDTH), 0)
    cw = cw_ref[...]

    def mixed_and_modulated(h):
        group = halves[h]
        x_ref, attn_ref, u_ref, ys_ref, gb_ref, _ = group[:N_ROW_INPUTS]
        gt1_ref, sh2_ref, sc2_ref, _ = group[N_ROW_INPUTS:]
        y = d_ref[...] * u_ref[...] + ys_ref[...]
        g = jax.nn.gelu(y)
        ssm = g * jax.nn.sigmoid(
            jnp.dot(g.astype(_BF16), wglu_ref[...], preferred_element_type=_F32) + bglu_ref[...])
        p = ps[h]
        p_before = jnp.where(row == 0, edge_before[h], pltpu.roll(p, 1, 0))
        p_after = jnp.where(row == TILE - 1, edge_after[h], pltpu.roll(p, TILE - 1, 0))
        conv = gb_ref[...] * (p_before * cw[0:1, :] + p * cw[1:2, :] + p_after * cw[2:3, :])
        mix = jnp.concatenate([attn_ref[...], ssm.astype(_BF16), conv.astype(_BF16)], axis=1)
        yv = jnp.dot(mix, wout_ref[...], preferred_element_type=_F32)
        x = x_ref[...] + gt1_ref[...] * _rms(yv, gpost1_ref[...])
        return x, _modulated(x, gpre2_ref[...], sh2_ref[...], sc2_ref[...]).astype(_BF16)

    (x0, h0), (x1, h1) = mixed_and_modulated(0), mixed_and_modulated(1)
    y0 = _swiglu(h0, wg_ref, wu_ref, wd_ref)
    y1 = _swiglu(h1, wg_ref, wu_ref, wd_ref)
    for h, (x, y) in enumerate(((x0, y0), (x1, y1))):
        gt2_ref = halves[h][N_ROW_INPUTS + 3]
        o_ref[h * TILE:(h + 1) * TILE, :] = x + MACARON * (gt2_ref[...] * _rms(y, gpost2_ref[...]))


def _post(x_all, mods, norm_pre, norm_post, attn, u_ssm, y_ssm, gb, p, ssm_d, w_glu, b_glu, conv_w,
          w_out, wg, wu, wd, layer, latent_only):
    tile = _latent_tile if latent_only else _stream_tile
    n_pairs = BATCH * (N_TILES - 1) // 2 if latent_only else BATCH * N_TILES // 2
    rows8 = TILE // 8
    in_specs, args = [], []
    for h in range(2):
        for arr in (x_all, attn, u_ssm, y_ssm, gb, p):
            in_specs.append(_row_spec(arr.shape[1], tile, h))
            args.append(arr)
        in_specs += [_mod_spec(layer, j, tile, h) for j in (5, 6, 7, 8)]
        args += [mods] * 4
    in_specs += [
        pl.BlockSpec((8, CONV_WIDTH), lambda p_: (jnp.maximum(tile(p_, 0) * rows8 - 1, 0), 0)),
        pl.BlockSpec((8, CONV_WIDTH),
                     lambda p_: (jnp.minimum((tile(p_, 1) + 1) * rows8, BATCH * TOKENS // 8 - 1), 0)),
        _norm_spec(layer, 1), _norm_spec(layer, 2), _norm_spec(layer, 2),
        _const_spec((None, 1, SSM_WIDTH), (layer, 0, 0)),
        _const_spec((None, SSM_WIDTH, SSM_WIDTH), (layer, 0, 0)),
        _const_spec((None, 1, SSM_WIDTH), (layer, 0, 0)),
        _const_spec((None, 3, CONV_WIDTH), (layer, 0, 0)),
        _const_spec((None, D_MODEL, D_MODEL), (layer, 0, 0)),
        *_ffn_weight_specs(layer, 1),
    ]
    args += [p, p, norm_post, norm_pre, norm_post, ssm_d, w_glu, b_glu, conv_w, w_out, wg, wu, wd]
    return pl.pallas_call(
        functools.partial(_post_kernel, tile_fn=tile),
        grid=(n_pairs,),
        in_specs=in_specs,
        out_specs=pl.BlockSpec((2 * TILE, D_MODEL), lambda p_: (p_, 0)),
        out_shape=jax.ShapeDtypeStruct((n_pairs * 2 * TILE, D_MODEL), _F32),
        compiler_params=_TOKEN_PARAMS,
        name="mix_out_ffn",
    )(*args)


def _rope_tables():
    pos = jnp.arange(SEQ)
    row = (pos // GRID_W).astype(_F32)
    col = (pos % GRID_W).astype(_F32)
    inv_freq = ROPE_BASE ** (-jnp.arange(ROPE_PAIRS, dtype=_F32) / ROPE_PAIRS)
    ang = jnp.stack([row[:, None] * inv_freq, col[:, None] * inv_freq], axis=1)
    cos, sin = jnp.cos(ang), jnp.sin(ang)
    cos_h = jnp.concatenate([cos, cos], axis=-1).reshape(SEQ, HEAD_DIM)
    sin_h = jnp.concatenate([-sin, sin], axis=-1).reshape(SEQ, HEAD_DIM)
    cos_t = jnp.concatenate([jnp.ones((CTX_LEN, HEAD_DIM), _F32), cos_h], axis=0)
    sin_t = jnp.concatenate([jnp.zeros((CTX_LEN, HEAD_DIM), _F32), sin_h], axis=0)
    return jnp.tile(cos_t, (1, 128 // HEAD_DIM)), jnp.tile(sin_t, (1, 128 // HEAD_DIM))


def _s5_matrices(lam_re, lam_im, log_step, b_re, b_im, c_re, c_im):
    n = SSM_CHUNK
    k = jnp.arange(n + 1, dtype=_F32)[:, None, None, None, None]
    dt = jnp.exp(log_step)[None, ..., None]
    mag = jnp.exp(k * lam_re[None] * dt)
    pr, pi = mag * jnp.cos(k * lam_im[None] * dt), mag * jnp.sin(k * lam_im[None] * dt)
    ar, ai = pr[1], pi[1]
    den = lam_re * lam_re + lam_im * lam_im
    gr = ((ar - 1) * lam_re + ai * lam_im) / den
    gi = (ai * lam_re - (ar - 1) * lam_im) / den
    bbr = gr[..., None] * b_re - gi[..., None] * b_im
    bbi = gr[..., None] * b_im + gi[..., None] * b_re
    car = c_re[None] * pr[..., None, :] - c_im[None] * pi[..., None, :]
    cai = c_re[None] * pi[..., None, :] + c_im[None] * pr[..., None, :]
    kern = jnp.einsum('kldghp,ldgpj->kldghj', car, bbr) - jnp.einsum('kldghp,ldgpj->kldghj', cai, bbi)
    step = jnp.arange(n)
    lag = step[None, :] - step[:, None]

    def toeplitz(kd, lag_d):
        m = jnp.where((lag_d >= 0)[..., None, None, None, None], kd[jnp.clip(lag_d, 0, n)], 0.0)
        return m.transpose(2, 3, 0, 5, 1, 4).reshape(DEPTH, SSM_GROUPS, n * SSM_GROUP, n * SSM_GROUP)

    intra = toeplitz(kern[:, :, 0], lag) + toeplitz(kern[:, :, 1], -lag)

    def state_in(d, power):
        qr, qi = pr[power, :, d], pi[power, :, d]
        re = qr[..., None] * bbr[None, :, d] - qi[..., None] * bbi[None, :, d]
        im = qr[..., None] * bbi[None, :, d] + qi[..., None] * bbr[None, :, d]
        flat = lambda m: m.transpose(1, 2, 0, 4, 3).reshape(DEPTH, SSM_GROUPS, n * SSM_GROUP, SSM_STATE)
        return flat(re), flat(im)

    f_re, f_im = state_in(0, n - 1 - step)
    b_re_, b_im_ = state_in(1, step)
    w1 = jnp.concatenate([intra, f_re, b_re_, f_im, b_im_], axis=-1)

    def state_out(d, power):
        flat = lambda m: m.transpose(1, 2, 4, 0, 3).reshape(DEPTH, SSM_GROUPS, SSM_STATE, n * SSM_GROUP)
        return flat(car[power, :, d]), -flat(cai[power, :, d])

    fo_re, fo_im = state_out(0, step + 1)
    bo_re, bo_im = state_out(1, n - step)
    w2 = jnp.concatenate([fo_re, bo_re, fo_im, bo_im], axis=-2)
    a_pow = jnp.stack([jnp.concatenate([pr[n, :, 0], pr[n, :, 1]], axis=-1),
                       jnp.concatenate([pi[n, :, 0], pi[n, :, 1]], axis=-1)], axis=-2)
    return w1.astype(_BF16), w2.astype(_BF16), a_pow


def kernel(x, c, ctx, c_ctx, w_ada, b_ada, norm_pre, norm_post, ffn_w_gate, ffn_w_up, ffn_w_down,
           w_in, w_out, attn_sink, ssm_lambda_re, ssm_lambda_im, ssm_log_step, ssm_b_re, ssm_b_im,
           ssm_c_re, ssm_c_im, ssm_d, ssm_w_glu, ssm_b_glu, conv_w):
    cc = jnp.concatenate([c, c_ctx[None, :], jnp.zeros((MOD_ROWS - BATCH - 1, D_MODEL), _F32)], axis=0)
    mods = _ada_table(cc, w_ada, b_ada)
    cos_t, sin_t = _rope_tables()
    w1, w2, a_pow = _s5_matrices(ssm_lambda_re, ssm_lambda_im, ssm_log_step,
                                 ssm_b_re, ssm_b_im, ssm_c_re, ssm_c_im)
    npre = norm_pre.reshape(DEPTH, 3, 1, D_MODEL)
    npost = norm_post.reshape(DEPTH, 3, 1, D_MODEL)
    wg, wu, wd = (w.astype(_BF16) for w in (ffn_w_gate, ffn_w_up, ffn_w_down))
    w_in_b, w_out_b, w_glu_b = (w.astype(_BF16) for w in (w_in, w_out, ssm_w_glu))
    d_skip = ssm_d.reshape(DEPTH, 1, SSM_WIDTH)
    b_glu = ssm_b_glu.reshape(DEPTH, 1, SSM_WIDTH)

    xa = (ctx, x)
    seq = lambda t: t.reshape(BATCH, TOKENS, t.shape[-1])
    flat = lambda t: t.reshape(BATCH * TOKENS, t.shape[-1])
    for l in range(DEPTH):
        xa, q, k, v, u_ssm, gb, p = _pre(xa, mods, npre, npost, wg, wu, wd, w_in_b, cos_t, sin_t, l)
        attn = _attention(attn_sink[l], seq(q), seq(k), seq(v))
        y_ssm = _s5_mixer(u_ssm, w1[l], w2[l], a_pow[l])
        xa = _post(xa, mods, npre, npost, flat(attn), u_ssm, y_ssm, gb, p, d_skip, w_glu_b,
                   b_glu, conv_w, w_out_b, wg, wu, wd, l, latent_only=l == DEPTH - 1)
    return xa.reshape(BATCH, SEQ, D_MODEL)
```

```python
import functools
import math

import jax
import jax.numpy as jnp
import numpy as np
from jax import lax
from jax.experimental import pallas as pl
from jax.experimental.pallas import tpu as pltpu

D_MODEL = 1024
BATCH = 16
SEQ = 4096
DEPTH = 4
GRID_W = 64
CTX_LEN = 256
HEAD_DIM = 64
ATTN_WIDTH = D_MODEL // 2
N_HEADS = ATTN_WIDTH // HEAD_DIM
KV_HEADS = N_HEADS // 4
Q_PER_KV = N_HEADS // KV_HEADS
KV_WIDTH = KV_HEADS * HEAD_DIM
WINDOW = 128
ATTN_SCALE = HEAD_DIM ** -0.5
LOG2_E = math.log2(math.e)
ROPE_BASE = 10000.0
ROPE_PAIRS = HEAD_DIM // 4
SSM_WIDTH = D_MODEL // 4
SSM_GROUP = 16
SSM_GROUPS = SSM_WIDTH // SSM_GROUP
SSM_STATE = 64
CONV_WIDTH = D_MODEL // 4
IN_COLS = ATTN_WIDTH + 2 * KV_WIDTH + SSM_WIDTH + 3 * CONV_WIDTH
D_FF = ((8 * D_MODEL // 3 + 127) // 128) * 128
MACARON = 0.5
N_MOD = 9
EPS = 1e-6
NEG_INF = -1e30

TOKENS = CTX_LEN + SEQ
TILE = 256
N_TILES = TOKENS // TILE
MOD_ROWS = 24
CTX_ROW = BATCH
N_LOCAL = TILE + 2 * WINDOW
VMEM_LIMIT = 48 * 1024 * 1024

_F32 = jnp.float32
_BF16 = jnp.bfloat16


def _rms(x, g):
    return (x * lax.rsqrt(jnp.mean(x * x, axis=-1, keepdims=True) + EPS)) * g


def _modulated(x, g, shift, scale):
    return _rms(x, g) * (1 + scale) + shift


def _ada_kernel(cc_ref, w_ref, b_ref, o_ref):
    cc = cc_ref[...]
    s = cc * jax.nn.sigmoid(cc)
    o_ref[...] = jnp.dot(s, w_ref[...], precision=lax.Precision.HIGHEST,
                         preferred_element_type=_F32) + b_ref[...]


def _ada_table(cc, w_ada, b_ada):
    out = pl.pallas_call(
        _ada_kernel,
        grid=(DEPTH, N_MOD),
        in_specs=[
            pl.BlockSpec((MOD_ROWS, D_MODEL), lambda l, j: (0, 0)),
            pl.BlockSpec((None, D_MODEL, D_MODEL), lambda l, j: (l, 0, j)),
            pl.BlockSpec((None, None, 1, D_MODEL), lambda l, j: (l, j, 0, 0)),
        ],
        out_specs=pl.BlockSpec((None, None, MOD_ROWS, D_MODEL), lambda l, j: (l, j, 0, 0)),
        out_shape=jax.ShapeDtypeStruct((DEPTH, N_MOD, MOD_ROWS, D_MODEL), _F32),
        name="ada_table",
    )(cc, w_ada, b_ada.reshape(DEPTH, N_MOD, 1, D_MODEL))
    return out.reshape(DEPTH, N_MOD, MOD_ROWS, 1, D_MODEL)


def _stream_tile(p, h):
    return 2 * p + h


def _latent_tile(p, h):
    per_batch = (N_TILES - 1) // 2
    return (p // per_batch) * N_TILES + 1 + 2 * (p % per_batch) + h


def _row_spec(width, tile_fn, h):
    return pl.BlockSpec((TILE, width), lambda p: (tile_fn(p, h), 0))


def _mod_spec(layer, j, tile_fn, h):
    def index(p):
        t = tile_fn(p, h)
        return (layer, j, jnp.where(t % N_TILES == 0, CTX_ROW, t // N_TILES), 0, 0)
    return pl.BlockSpec((None, None, None, 1, D_MODEL), index)


def _const_spec(shape, index):
    return pl.BlockSpec(shape, lambda p: index, pipeline_mode=pl.Buffered(1))


def _norm_spec(layer, sub):
    return _const_spec((None, None, 1, D_MODEL), (layer, sub, 0, 0))


def _ffn_weight_specs(layer, which):
    return [
        _const_spec((None, None, D_MODEL, D_FF), (layer, which, 0, 0)),
        _const_spec((None, None, D_MODEL, D_FF), (layer, which, 0, 0)),
        _const_spec((None, None, D_FF, D_MODEL), (layer, which, 0, 0)),
    ]


_TOKEN_PARAMS = pltpu.CompilerParams(dimension_semantics=("parallel",), vmem_limit_bytes=VMEM_LIMIT)


SSM_CHUNK = 256 // SSM_GROUP
TILE_CHUNKS = TILE // SSM_CHUNK
ATOM = 16
assert SSM_CHUNK == SSM_GROUPS == TILE_CHUNKS == SSM_GROUP == BATCH == ATOM


def _swap_rows_matrix():
    r = np.arange(ATOM * ATOM)
    perm = np.zeros((ATOM * ATOM, ATOM * ATOM), np.float32)
    perm[r, (r % ATOM) * ATOM + r // ATOM] = 1.0
    return jnp.asarray(perm, _BF16)


def _atom_transpose(blocks):
    out = list(blocks)
    for i in range(ATOM // 2):
        (a_lo, a_hi), (b_lo, b_hi) = blocks[i], blocks[i + ATOM // 2]
        out[i], out[i + ATOM // 2] = (a_lo, b_lo), (a_hi, b_hi)
    blocks = out
    lane_atom = lax.broadcasted_iota(jnp.int32, (ATOM, 128), 1) // ATOM
    for s in (4, 2, 1):
        keep = (lane_atom & s) == 0
        out = list(blocks)
        for i in range(ATOM):
            if i & s:
                continue
            a, b = blocks[i], blocks[i + s]
            out[i] = tuple(jnp.where(keep, x, pltpu.roll(y, s * ATOM, 1)) for x, y in zip(a, b))
            out[i + s] = tuple(jnp.where(keep, pltpu.roll(x, 128 - s * ATOM, 1), y) for x, y in zip(a, b))
        blocks = out
    return blocks


def _tile_to_chunk_layout(u_tile, swap_ref):
    z = jnp.dot(swap_ref[...], u_tile.astype(_BF16), preferred_element_type=_F32)
    blocks = [(z[i * ATOM:(i + 1) * ATOM, 0:128], z[i * ATOM:(i + 1) * ATOM, 128:256]) for i in range(ATOM)]
    return [jnp.concatenate(pair, axis=1) for pair in _atom_transpose(blocks)]


def _chunk_layout_to_tile(y_groups, swap_ref):
    blocks = [(y_groups[g, :, 0:128].astype(_F32), y_groups[g, :, 128:256].astype(_F32)) for g in range(ATOM)]
    z = jnp.concatenate([jnp.concatenate(pair, axis=1) for pair in _atom_transpose(blocks)], axis=0)
    return jnp.dot(swap_ref[...], z.astype(_BF16), preferred_element_type=_F32)


def _swiglu(h, wg_ref, wu_ref, wd_ref):
    a = jnp.dot(h, wg_ref[...], preferred_element_type=_F32)
    b = jnp.dot(h, wu_ref[...], preferred_element_type=_F32)
    s = ((a * jax.nn.sigmoid(a)) * b).astype(_BF16)
    return jnp.dot(s, wd_ref[...], preferred_element_type=_F32)


def _ffn_pair(xs, mods, gpre, gpost, wg_ref, wu_ref, wd_ref):
    hs = [_modulated(x, gpre, sh[...], sc[...]).astype(_BF16) for x, (sh, sc, _) in zip(xs, mods)]
    ys = [_swiglu(h, wg_ref, wu_ref, wd_ref) for h in hs]
    return [x + MACARON * (gt[...] * _rms(y, gpost)) for x, y, (_, _, gt) in zip(xs, ys, mods)]


def _pre_kernel(*refs, split_input):
    n_x = 2 if split_input else 1
    halves = []
    for h in range(2):
        group, refs = refs[:n_x + 7], refs[n_x + 7:]
        halves.append(group)
    (gpre0_ref, gpost0_ref, gpre1_ref, wg_ref, wu_ref, wd_ref, w_ref, swap_ref,
     xo_ref, q_ref, k_ref, v_ref, u_ref, uc_ref, gb_ref, p_ref) = refs
    xs = []
    for h, group in enumerate(halves):
        if split_input:
            is_ctx = _stream_tile(pl.program_id(0), h) % N_TILES == 0
            xs.append(jnp.where(is_ctx, group[0][...], group[1][...]))
        else:
            xs.append(group[0][...])
    mods0 = [group[n_x:n_x + 3] for group in halves]
    xs = _ffn_pair(xs, mods0, gpre0_ref[...], gpost0_ref[...], wg_ref, wu_ref, wd_ref)
    for h, x in enumerate(xs):
        xo_ref[h * TILE:(h + 1) * TILE, :] = x
    hs = [_modulated(x, gpre1_ref[...], group[n_x + 3][...], group[n_x + 4][...]).astype(_BF16)
          for x, group in zip(xs, halves)]
    prs = [jnp.dot(h, w_ref[...], preferred_element_type=_F32) for h in hs]
    lane = lax.broadcasted_iota(jnp.int32, (TILE, 128), 1)
    first_half = (lane % (2 * ROPE_PAIRS)) < ROPE_PAIRS
    for h, (pr, group) in enumerate(zip(prs, halves)):
        rows = slice(h * TILE, (h + 1) * TILE)
        cosv = group[n_x + 5][...]
        sinv = group[n_x + 6][...]

        def rope(t, cosv=cosv, sinv=sinv):
            partner = jnp.where(first_half, pltpu.roll(t, 128 - ROPE_PAIRS, 1), pltpu.roll(t, ROPE_PAIRS, 1))
            return t * cosv + partner * sinv

        for j in range(ATTN_WIDTH // 128):
            q_ref[rows, j * 128:(j + 1) * 128] = (
                rope(pr[:, j * 128:(j + 1) * 128]) * (ATTN_SCALE * LOG2_E)).astype(_BF16)
        c = ATTN_WIDTH
        k_ref[rows, :] = rope(pr[:, c:c + KV_WIDTH]).astype(_BF16)
        c += KV_WIDTH
        ones = jnp.ones((TILE, HEAD_DIM), _F32)
        v_ref[rows, :] = jnp.concatenate(
            [piece for kv in range(KV_HEADS)
             for piece in (pr[:, c + kv * HEAD_DIM:c + (kv + 1) * HEAD_DIM], ones)], axis=1).astype(_BF16)
        c += KV_WIDTH
        u_ref[rows, :] = pr[:, c:c + SSM_WIDTH]
        for g, u_group in enumerate(_tile_to_chunk_layout(pr[:, c:c + SSM_WIDTH], swap_ref)):
            uc_ref[g, h, :, :] = u_group.astype(_BF16)
        c += SSM_WIDTH
        gb_ref[rows, :] = pr[:, c:c + CONV_WIDTH]
        c += CONV_WIDTH
        p_ref[rows, :] = pr[:, c:c + CONV_WIDTH] * pr[:, c + CONV_WIDTH:c + 2 * CONV_WIDTH]


def _pre(x_in, mods, norm_pre, norm_post, wg, wu, wd, w_in, cos_t, sin_t, layer):
    split_input = isinstance(x_in, tuple)
    tile = _stream_tile
    in_specs, args = [], []
    for h in range(2):
        if split_input:
            in_specs += [
                pl.BlockSpec((None, CTX_LEN, D_MODEL), lambda p, h=h: (tile(p, h) // N_TILES, 0, 0)),
                pl.BlockSpec((None, TILE, D_MODEL),
                             lambda p, h=h: (tile(p, h) // N_TILES, jnp.maximum(tile(p, h) % N_TILES - 1, 0), 0)),
            ]
            args += list(x_in)
        else:
            in_specs.append(_row_spec(D_MODEL, tile, h))
            args.append(x_in)
        in_specs += [_mod_spec(layer, j, tile, h) for j in range(5)]
        args += [mods] * 5
        in_specs += [pl.BlockSpec((TILE, 128), lambda p, h=h: (tile(p, h) % N_TILES, 0)) for _ in range(2)]
        args += [cos_t, sin_t]
    in_specs += [_norm_spec(layer, 0), _norm_spec(layer, 0), _norm_spec(layer, 1),
                 *_ffn_weight_specs(layer, 0), _const_spec((None, D_MODEL, IN_COLS), (layer, 0, 0)),
                 _const_spec((TILE, TILE), (0, 0))]
    args += [norm_pre, norm_post, norm_pre, wg, wu, wd, w_in, _swap_rows_matrix()]
    pair = lambda w: pl.BlockSpec((2 * TILE, w), lambda p: (p, 0))
    tok = lambda w, dt: jax.ShapeDtypeStruct((BATCH * TOKENS, w), dt)
    return pl.pallas_call(
        functools.partial(_pre_kernel, split_input=split_input),
        grid=(BATCH * N_TILES // 2,),
        in_specs=in_specs,
        out_specs=[pair(D_MODEL), pair(ATTN_WIDTH), pair(KV_WIDTH), pair(2 * KV_WIDTH), pair(SSM_WIDTH),
                   pl.BlockSpec((SSM_GROUPS, 2, TILE_CHUNKS, SSM_WIDTH), lambda p: (0, p, 0, 0)),
                   pair(CONV_WIDTH), pair(CONV_WIDTH)],
        out_shape=[
            tok(D_MODEL, _F32),
            tok(ATTN_WIDTH, _BF16), tok(KV_WIDTH, _BF16), tok(2 * KV_WIDTH, _BF16),
            tok(SSM_WIDTH, _F32),
            jax.ShapeDtypeStruct((SSM_GROUPS, BATCH * N_TILES, TILE_CHUNKS, SSM_WIDTH), _BF16),
            tok(CONV_WIDTH, _F32), tok(CONV_WIDTH, _F32),
        ],
        compiler_params=_TOKEN_PARAMS,
        name="ffn_in_proj",
    )(*args)


def _dot_nt(a, b):
    return lax.dot_general(a, b, (((1,), (1,)), ((), ())), preferred_element_type=_F32)


def _dot_tn(a, b):
    return lax.dot_general(a, b, (((0,), (0,)), ((), ())), preferred_element_type=_F32)


def _attn_kernel(sink_ref, q_ref, k_ref, v_ref, o_ref):
    i = pl.program_id(1)
    q0 = i * TILE
    start = pl.multiple_of(jnp.clip(q0 - WINDOW, CTX_LEN, TOKENS - N_LOCAL), WINDOW)
    k_loc = k_ref[pl.ds(start, N_LOCAL), :]
    v_loc = v_ref[pl.ds(start, N_LOCAL), :]
    k_ctx = k_ref[0:CTX_LEN, :]
    v_ctx = v_ref[0:CTX_LEN, :]
    kj = start + lax.broadcasted_iota(jnp.int32, (N_LOCAL, TILE), 0)
    qi = q0 + lax.broadcasted_iota(jnp.int32, (N_LOCAL, TILE), 1)
    valid = (jnp.abs(kj - qi) <= WINDOW) & (i >= 1)
    bias = jnp.where(valid, 0.0, NEG_INF).astype(_F32)
    q = q_ref[...]
    outs = []

    def scores(h):
        lo = (h // Q_PER_KV) * HEAD_DIM
        qh = q[:, h * HEAD_DIM:(h + 1) * HEAD_DIM]
        return (_dot_nt(k_loc[:, lo:lo + HEAD_DIM], qh) + bias,
                _dot_nt(k_ctx[:, lo:lo + HEAD_DIM], qh))

    nxt = scores(0)
    for h in range(N_HEADS):
        lo = (h // Q_PER_KV) * HEAD_DIM
        s_loc, s_ctx = nxt
        if h + 1 < N_HEADS:
            nxt = scores(h + 1)
        sink = sink_ref[h] * LOG2_E
        m = jnp.maximum(jnp.maximum(jnp.max(s_loc, axis=0, keepdims=True),
                                    jnp.max(s_ctx, axis=0, keepdims=True)), sink)
        p_loc = jnp.exp2((s_loc - m).astype(_BF16))
        p_ctx = jnp.exp2((s_ctx - m).astype(_BF16))
        vlo = 2 * lo
        ov = (_dot_tn(v_loc[:, vlo:vlo + 2 * HEAD_DIM], p_loc)
              + _dot_tn(v_ctx[:, vlo:vlo + 2 * HEAD_DIM], p_ctx))
        den = ov[HEAD_DIM:HEAD_DIM + 1, :] + jnp.exp2(sink - m)
        outs.append(ov[0:HEAD_DIM, :] / den)
        if h % 2 == 1:
            pair = jnp.concatenate(outs, axis=0).T.astype(_BF16)
            o_ref[:, (h - 1) * HEAD_DIM:(h + 1) * HEAD_DIM] = pair
            outs = []


def _attention(sink, q, k, v):
    seq_spec = pl.BlockSpec((None, TOKENS, KV_WIDTH), lambda b, i: (b, 0, 0))
    val_spec = pl.BlockSpec((None, TOKENS, 2 * KV_WIDTH), lambda b, i: (b, 0, 0))
    tile_spec = pl.BlockSpec((None, TILE, ATTN_WIDTH), lambda b, i: (b, i, 0))
    return pl.pallas_call(
        _attn_kernel,
        grid=(BATCH, N_TILES),
        in_specs=[
            pl.BlockSpec(memory_space=pltpu.SMEM),
            tile_spec, seq_spec, val_spec,
        ],
        out_specs=tile_spec,
        out_shape=jax.ShapeDtypeStruct((BATCH, TOKENS, ATTN_WIDTH), _BF16),
        compiler_params=pltpu.CompilerParams(
            dimension_semantics=("parallel", "parallel"), vmem_limit_bytes=VMEM_LIMIT),
        name="window_attn",
    )(sink, q, k, v)


N_CHUNKS = TOKENS // SSM_CHUNK
N_CTX_CHUNKS = CTX_LEN // SSM_CHUNK
CHUNK_ROWS = N_CHUNKS * BATCH
STATE_LANES = 2 * SSM_STATE


def _s5_kernel(u_ref, swap_ref, w1_ref, w2_ref, a_ref, y_ref, ucb_ref, xw_ref, s_ref):
    tile_rows = BATCH * TILE_CHUNKS
    for j in range(N_TILES):
        rows_bc = jnp.concatenate([u_ref[N_TILES * b + j] for b in range(BATCH)], axis=0)
        ucb_ref[j * tile_rows:(j + 1) * tile_rows, :] = jnp.dot(
            swap_ref[...], rows_bc, preferred_element_type=_F32).astype(_BF16)
    xw_ref[...] = jnp.dot(ucb_ref[...], w1_ref[...], preferred_element_type=_F32)
    a_r = jnp.broadcast_to(a_ref[0:1, :], (BATCH, STATE_LANES))
    a_i = jnp.broadcast_to(a_ref[1:2, :], (BATCH, STATE_LANES))
    fwd_lanes = lax.broadcasted_iota(jnp.int32, (BATCH, STATE_LANES), 1) < SSM_STATE
    x_re, x_im = SSM_WIDTH, SSM_WIDTH + STATE_LANES

    def body(i, carry):
        s_r, s_i = carry
        cb = jnp.where(i < N_CTX_CHUNKS, N_CTX_CHUNKS - 1 - i, N_CHUNKS + N_CTX_CHUNKS - 1 - i)
        rf = pl.ds(pl.multiple_of(i * BATCH, BATCH), BATCH)
        rb = pl.ds(pl.multiple_of(cb * BATCH, BATCH), BATCH)
        s_ref[rf, 0:SSM_STATE] = s_r[:, 0:SSM_STATE]
        s_ref[rb, SSM_STATE:STATE_LANES] = s_r[:, SSM_STATE:STATE_LANES]
        s_ref[rf, STATE_LANES:STATE_LANES + SSM_STATE] = s_i[:, 0:SSM_STATE]
        s_ref[rb, STATE_LANES + SSM_STATE:2 * STATE_LANES] = s_i[:, SSM_STATE:STATE_LANES]
        xr = jnp.where(fwd_lanes, xw_ref[rf, x_re:x_re + STATE_LANES], xw_ref[rb, x_re:x_re + STATE_LANES])
        xi = jnp.where(fwd_lanes, xw_ref[rf, x_im:x_im + STATE_LANES], xw_ref[rb, x_im:x_im + STATE_LANES])
        return a_r * s_r - a_i * s_i + xr, a_r * s_i + a_i * s_r + xi

    zero = jnp.zeros((BATCH, STATE_LANES), _F32)
    lax.fori_loop(0, N_CHUNKS, body, (zero, zero), unroll=8)
    half = CHUNK_ROWS // 2
    for r in (0, half):
        y = xw_ref[r:r + half, 0:SSM_WIDTH] + jnp.dot(
            s_ref[r:r + half, :].astype(_BF16), w2_ref[...], preferred_element_type=_F32)
        ucb_ref[r:r + half, :] = y.astype(_BF16)
    for j in range(N_TILES):
        rows_bc = jnp.dot(swap_ref[...], ucb_ref[j * tile_rows:(j + 1) * tile_rows, :],
                          preferred_element_type=_F32).astype(_BF16)
        for b in range(BATCH):
            y_ref[N_TILES * b + j] = rows_bc[b * TILE_CHUNKS:(b + 1) * TILE_CHUNKS, :]


def _s5_mixer(uc, w1, w2, a_pow):
    per_group = lambda a: pl.BlockSpec((None,) + a.shape[1:], lambda g: (g,) + (0,) * (a.ndim - 1))
    swap = _swap_rows_matrix()
    return pl.pallas_call(
        _s5_kernel,
        grid=(SSM_GROUPS,),
        in_specs=[per_group(uc), pl.BlockSpec(swap.shape, lambda g: (0, 0)),
                  per_group(w1), per_group(w2), per_group(a_pow)],
        out_specs=per_group(uc),
        out_shape=jax.ShapeDtypeStruct(uc.shape, _BF16),
        scratch_shapes=[
            pltpu.VMEM((CHUNK_ROWS, SSM_WIDTH), _BF16),
            pltpu.VMEM((CHUNK_ROWS, SSM_WIDTH + 2 * STATE_LANES), _F32),
            pltpu.VMEM((CHUNK_ROWS, 2 * STATE_LANES), _F32),
        ],
        compiler_params=pltpu.CompilerParams(
            dimension_semantics=("parallel",), vmem_limit_bytes=VMEM_LIMIT),
        name="s5_chunked",
    )(uc, swap, w1, w2, a_pow)


N_ROW_INPUTS = 6


def _post_kernel(*refs, tile_fn):
    halves = []
    for h in range(2):
        group, refs = refs[:N_ROW_INPUTS + 4], refs[N_ROW_INPUTS + 4:]
        halves.append(group)
    (pprev_ref, pnext_ref, gpost1_ref, gpre2_ref, gpost2_ref, d_ref, wglu_ref, bglu_ref, cw_ref,
     wout_ref, wg_ref, wu_ref, wd_ref, swap_ref, o_ref) = refs
    tiles = [tile_fn(pl.program_id(0), h) % N_TILES for h in range(2)]
    ps = [group[5][...] for group in halves]
    edge_before = [jnp.where(tiles[0] >= 2, pprev_ref[7:8, :], 0.0),
                   jnp.where(tiles[1] >= 2, ps[0][TILE - 1:TILE, :], 0.0)]
    edge_after = [jnp.where((tiles[0] >= 1) & (tiles[0] <= N_TILES - 2), ps[1][0:1, :], 0.0),
                  jnp.where((tiles[1] >= 1) & (tiles[1] <= N_TILES - 2), pnext_ref[0:1, :], 0.0)]
    row = lax.broadcasted_iota(jnp.int32, (TILE, CONV_WIDTH), 0)
    cw = cw_ref[...]

    def mixed_and_modulated(h):
        group = halves[h]
        x_ref, attn_ref, u_ref, ys_ref, gb_ref, _ = group[:N_ROW_INPUTS]
        gt1_ref, sh2_ref, sc2_ref, _ = group[N_ROW_INPUTS:]
        y = d_ref[...] * u_ref[...] + _chunk_layout_to_tile(ys_ref, swap_ref)
        g = jax.nn.gelu(y)
        ssm = g * jax.nn.sigmoid(
            jnp.dot(g.astype(_BF16), wglu_ref[...], preferred_element_type=_F32) + bglu_ref[...])
        p = ps[h]
        p_before = jnp.where(row == 0, edge_before[h], pltpu.roll(p, 1, 0))
        p_after = jnp.where(row == TILE - 1, edge_after[h], pltpu.roll(p, TILE - 1, 0))
        conv = gb_ref[...] * (p_before * cw[0:1, :] + p * cw[1:2, :] + p_after * cw[2:3, :])
        mix = jnp.concatenate([attn_ref[...], ssm.astype(_BF16), conv.astype(_BF16)], axis=1)
        yv = jnp.dot(mix, wout_ref[...], preferred_element_type=_F32)
        x = x_ref[...] + gt1_ref[...] * _rms(yv, gpost1_ref[...])
        return x, _modulated(x, gpre2_ref[...], sh2_ref[...], sc2_ref[...]).astype(_BF16)

    (x0, h0), (x1, h1) = mixed_and_modulated(0), mixed_and_modulated(1)
    y0 = _swiglu(h0, wg_ref, wu_ref, wd_ref)
    y1 = _swiglu(h1, wg_ref, wu_ref, wd_ref)
    for h, (x, y) in enumerate(((x0, y0), (x1, y1))):
        gt2_ref = halves[h][N_ROW_INPUTS + 3]
        o_ref[h * TILE:(h + 1) * TILE, :] = x + MACARON * (gt2_ref[...] * _rms(y, gpost2_ref[...]))


def _post(x_all, mods, norm_pre, norm_post, attn, u_ssm, y_ssm, gb, p, ssm_d, w_glu, b_glu, conv_w,
          w_out, wg, wu, wd, layer, latent_only):
    tile = _latent_tile if latent_only else _stream_tile
    n_pairs = BATCH * (N_TILES - 1) // 2 if latent_only else BATCH * N_TILES // 2
    rows8 = TILE // 8
    in_specs, args = [], []
    for h in range(2):
        for arr in (x_all, attn, u_ssm, y_ssm, gb, p):
            if arr is y_ssm:
                in_specs.append(pl.BlockSpec((SSM_GROUPS, None, TILE_CHUNKS, SSM_WIDTH),
                                             lambda p_, h=h: (0, tile(p_, h), 0, 0)))
            else:
                in_specs.append(_row_spec(arr.shape[1], tile, h))
            args.append(arr)
        in_specs += [_mod_spec(layer, j, tile, h) for j in (5, 6, 7, 8)]
        args += [mods] * 4
    in_specs += [
        pl.BlockSpec((8, CONV_WIDTH), lambda p_: (jnp.maximum(tile(p_, 0) * rows8 - 1, 0), 0)),
        pl.BlockSpec((8, CONV_WIDTH),
                     lambda p_: (jnp.minimum((tile(p_, 1) + 1) * rows8, BATCH * TOKENS // 8 - 1), 0)),
        _norm_spec(layer, 1), _norm_spec(layer, 2), _norm_spec(layer, 2),
        _const_spec((None, 1, SSM_WIDTH), (layer, 0, 0)),
        _const_spec((None, SSM_WIDTH, SSM_WIDTH), (layer, 0, 0)),
        _const_spec((None, 1, SSM_WIDTH), (layer, 0, 0)),
        _const_spec((None, 3, CONV_WIDTH), (layer, 0, 0)),
        _const_spec((None, D_MODEL, D_MODEL), (layer, 0, 0)),
        *_ffn_weight_specs(layer, 1),
        _const_spec((TILE, TILE), (0, 0)),
    ]
    args += [p, p, norm_post, norm_pre, norm_post, ssm_d, w_glu, b_glu, conv_w, w_out, wg, wu, wd,
             _swap_rows_matrix()]
    return pl.pallas_call(
        functools.partial(_post_kernel, tile_fn=tile),
        grid=(n_pairs,),
        in_specs=in_specs,
        out_specs=pl.BlockSpec((2 * TILE, D_MODEL), lambda p_: (p_, 0)),
        out_shape=jax.ShapeDtypeStruct((n_pairs * 2 * TILE, D_MODEL), _F32),
        compiler_params=_TOKEN_PARAMS,
        name="mix_out_ffn",
    )(*args)


def _rope_tables():
    pos = jnp.arange(SEQ)
    row = (pos // GRID_W).astype(_F32)
    col = (pos % GRID_W).astype(_F32)
    inv_freq = ROPE_BASE ** (-jnp.arange(ROPE_PAIRS, dtype=_F32) / ROPE_PAIRS)
    ang = jnp.stack([row[:, None] * inv_freq, col[:, None] * inv_freq], axis=1)
    cos, sin = jnp.cos(ang), jnp.sin(ang)
    cos_h = jnp.concatenate([cos, cos], axis=-1).reshape(SEQ, HEAD_DIM)
    sin_h = jnp.concatenate([-sin, sin], axis=-1).reshape(SEQ, HEAD_DIM)
    cos_t = jnp.concatenate([jnp.ones((CTX_LEN, HEAD_DIM), _F32), cos_h], axis=0)
    sin_t = jnp.concatenate([jnp.zeros((CTX_LEN, HEAD_DIM), _F32), sin_h], axis=0)
    return jnp.tile(cos_t, (1, 128 // HEAD_DIM)), jnp.tile(sin_t, (1, 128 // HEAD_DIM))


def _s5_matrices(lam_re, lam_im, log_step, b_re, b_im, c_re, c_im):
    n = SSM_CHUNK
    k = jnp.arange(n + 1, dtype=_F32)[:, None, None, None, None]
    dt = jnp.exp(log_step)[None, ..., None]
    mag = jnp.exp(k * lam_re[None] * dt)
    pr, pi = mag * jnp.cos(k * lam_im[None] * dt), mag * jnp.sin(k * lam_im[None] * dt)
    ar, ai = pr[1], pi[1]
    den = lam_re * lam_re + lam_im * lam_im
    gr = ((ar - 1) * lam_re + ai * lam_im) / den
    gi = (ai * lam_re - (ar - 1) * lam_im) / den
    bbr = gr[..., None] * b_re - gi[..., None] * b_im
    bbi = gr[..., None] * b_im + gi[..., None] * b_re
    car = c_re[None] * pr[..., None, :] - c_im[None] * pi[..., None, :]
    cai = c_re[None] * pi[..., None, :] + c_im[None] * pr[..., None, :]
    kern = jnp.einsum('kldghp,ldgpj->kldghj', car, bbr) - jnp.einsum('kldghp,ldgpj->kldghj', cai, bbi)
    step = jnp.arange(n)
    lag = step[None, :] - step[:, None]

    def toeplitz(kd, lag_d):
        m = jnp.where((lag_d >= 0)[..., None, None, None, None], kd[jnp.clip(lag_d, 0, n)], 0.0)
        return m.transpose(2, 3, 0, 5, 1, 4).reshape(DEPTH, SSM_GROUPS, n * SSM_GROUP, n * SSM_GROUP)

    intra = toeplitz(kern[:, :, 0], lag) + toeplitz(kern[:, :, 1], -lag)

    def state_in(d, power):
        qr, qi = pr[power, :, d], pi[power, :, d]
        re = qr[..., None] * bbr[None, :, d] - qi[..., None] * bbi[None, :, d]
        im = qr[..., None] * bbi[None, :, d] + qi[..., None] * bbr[None, :, d]
        flat = lambda m: m.transpose(1, 2, 0, 4, 3).reshape(DEPTH, SSM_GROUPS, n * SSM_GROUP, SSM_STATE)
        return flat(re), flat(im)

    f_re, f_im = state_in(0, n - 1 - step)
    b_re_, b_im_ = state_in(1, step)
    w1 = jnp.concatenate([intra, f_re, b_re_, f_im, b_im_], axis=-1)

    def state_out(d, power):
        flat = lambda m: m.transpose(1, 2, 4, 0, 3).reshape(DEPTH, SSM_GROUPS, SSM_STATE, n * SSM_GROUP)
        return flat(car[power, :, d]), -flat(cai[power, :, d])

    fo_re, fo_im = state_out(0, step + 1)
    bo_re, bo_im = state_out(1, n - step)
    w2 = jnp.concatenate([fo_re, bo_re, fo_im, bo_im], axis=-2)
    a_pow = jnp.stack([jnp.concatenate([pr[n, :, 0], pr[n, :, 1]], axis=-1),
                       jnp.concatenate([pi[n, :, 0], pi[n, :, 1]], axis=-1)], axis=-2)
    return w1.astype(_BF16), w2.astype(_BF16), a_pow


def kernel(x, c, ctx, c_ctx, w_ada, b_ada, norm_pre, norm_post, ffn_w_gate, ffn_w_up, ffn_w_down,
           w_in, w_out, attn_sink, ssm_lambda_re, ssm_lambda_im, ssm_log_step, ssm_b_re, ssm_b_im,
           ssm_c_re, ssm_c_im, ssm_d, ssm_w_glu, ssm_b_glu, conv_w):
    cc = jnp.concatenate([c, c_ctx[None, :], jnp.zeros((MOD_ROWS - BATCH - 1, D_MODEL), _F32)], axis=0)
    mods = _ada_table(cc, w_ada, b_ada)
    cos_t, sin_t = _rope_tables()
    w1, w2, a_pow = _s5_matrices(ssm_lambda_re, ssm_lambda_im, ssm_log_step,
                                 ssm_b_re, ssm_b_im, ssm_c_re, ssm_c_im)
    npre = norm_pre.reshape(DEPTH, 3, 1, D_MODEL)
    npost = norm_post.reshape(DEPTH, 3, 1, D_MODEL)
    wg, wu, wd = (w.astype(_BF16) for w in (ffn_w_gate, ffn_w_up, ffn_w_down))
    w_in_b, w_out_b, w_glu_b = (w.astype(_BF16) for w in (w_in, w_out, ssm_w_glu))
    d_skip = ssm_d.reshape(DEPTH, 1, SSM_WIDTH)
    b_glu = ssm_b_glu.reshape(DEPTH, 1, SSM_WIDTH)

    xa = (ctx, x)
    seq = lambda t: t.reshape(BATCH, TOKENS, t.shape[-1])
    flat = lambda t: t.reshape(BATCH * TOKENS, t.shape[-1])
    for l in range(DEPTH):
        xa, q, k, v, u_ssm, u_chunks, gb, p = _pre(xa, mods, npre, npost, wg, wu, wd, w_in_b, cos_t, sin_t, l)
        attn = _attention(attn_sink[l], seq(q), seq(k), seq(v))
        y_ssm = _s5_mixer(u_chunks, w1[l], w2[l], a_pow[l])
        xa = _post(xa, mods, npre, npost, flat(attn), u_ssm, y_ssm, gb, p, d_skip, w_glu_b,
                   b_glu, conv_w, w_out_b, wg, wu, wd, l, latent_only=l == DEPTH - 1)
    return xa.reshape(BATCH, SEQ, D_MODEL)
```

```python
import functools
import math

import jax
import jax.numpy as jnp
import numpy as np
from jax import lax
from jax.experimental import pallas as pl
from jax.experimental.pallas import tpu as pltpu

D_MODEL = 1024
BATCH = 16
SEQ = 4096
DEPTH = 4
GRID_W = 64
CTX_LEN = 256
HEAD_DIM = 64
ATTN_WIDTH = D_MODEL // 2
N_HEADS = ATTN_WIDTH // HEAD_DIM
KV_HEADS = N_HEADS // 4
Q_PER_KV = N_HEADS // KV_HEADS
KV_WIDTH = KV_HEADS * HEAD_DIM
WINDOW = 128
ATTN_SCALE = HEAD_DIM ** -0.5
LOG2_E = math.log2(math.e)
ROPE_BASE = 10000.0
ROPE_PAIRS = HEAD_DIM // 4
SSM_WIDTH = D_MODEL // 4
SSM_GROUP = 16
SSM_GROUPS = SSM_WIDTH // SSM_GROUP
SSM_STATE = 64
CONV_WIDTH = D_MODEL // 4
IN_COLS = ATTN_WIDTH + 2 * KV_WIDTH + SSM_WIDTH + 3 * CONV_WIDTH
D_FF = ((8 * D_MODEL // 3 + 127) // 128) * 128
MACARON = 0.5
N_MOD = 9
EPS = 1e-6
NEG_INF = -1e30

TOKENS = CTX_LEN + SEQ
TILE = 256
N_TILES = TOKENS // TILE
MOD_ROWS = 24
CTX_ROW = BATCH
N_LOCAL = TILE + 2 * WINDOW
VMEM_LIMIT = 48 * 1024 * 1024

_F32 = jnp.float32
_BF16 = jnp.bfloat16


def _rms(x, g):
    return (x * lax.rsqrt(jnp.mean(x * x, axis=-1, keepdims=True) + EPS)) * g


def _modulated(x, g, shift, scale):
    return _rms(x, g) * (1 + scale) + shift


def _ada_kernel(cc_ref, w_ref, b_ref, o_ref):
    cc = cc_ref[...]
    s = cc * jax.nn.sigmoid(cc)
    o_ref[...] = jnp.dot(s, w_ref[...], precision=lax.Precision.HIGHEST,
                         preferred_element_type=_F32) + b_ref[...]


def _ada_table(cc, w_ada, b_ada):
    out = pl.pallas_call(
        _ada_kernel,
        grid=(DEPTH, N_MOD),
        in_specs=[
            pl.BlockSpec((MOD_ROWS, D_MODEL), lambda l, j: (0, 0)),
            pl.BlockSpec((None, D_MODEL, D_MODEL), lambda l, j: (l, 0, j)),
            pl.BlockSpec((None, None, 1, D_MODEL), lambda l, j: (l, j, 0, 0)),
        ],
        out_specs=pl.BlockSpec((None, None, MOD_ROWS, D_MODEL), lambda l, j: (l, j, 0, 0)),
        out_shape=jax.ShapeDtypeStruct((DEPTH, N_MOD, MOD_ROWS, D_MODEL), _F32),
        name="ada_table",
    )(cc, w_ada, b_ada.reshape(DEPTH, N_MOD, 1, D_MODEL))
    return out.reshape(DEPTH, N_MOD, MOD_ROWS, 1, D_MODEL)


def _stream_tile(p, h):
    return 2 * p + h


def _latent_tile(p, h):
    per_batch = (N_TILES - 1) // 2
    return (p // per_batch) * N_TILES + 1 + 2 * (p % per_batch) + h


def _row_spec(width, tile_fn, h):
    return pl.BlockSpec((TILE, width), lambda p: (tile_fn(p, h), 0))


def _mod_spec(layer, j, tile_fn, h):
    def index(p):
        t = tile_fn(p, h)
        return (layer, j, jnp.where(t % N_TILES == 0, CTX_ROW, t // N_TILES), 0, 0)
    return pl.BlockSpec((None, None, None, 1, D_MODEL), index)


def _const_spec(shape, index):
    return pl.BlockSpec(shape, lambda p: index, pipeline_mode=pl.Buffered(1))


def _norm_spec(layer, sub):
    return _const_spec((None, None, 1, D_MODEL), (layer, sub, 0, 0))


def _ffn_weight_specs(layer, which):
    return [
        _const_spec((None, None, D_MODEL, D_FF), (layer, which, 0, 0)),
        _const_spec((None, None, D_MODEL, D_FF), (layer, which, 0, 0)),
        _const_spec((None, None, D_FF, D_MODEL), (layer, which, 0, 0)),
    ]


_TOKEN_PARAMS = pltpu.CompilerParams(dimension_semantics=("parallel",), vmem_limit_bytes=VMEM_LIMIT)


SSM_CHUNK = 256 // SSM_GROUP
TILE_CHUNKS = TILE // SSM_CHUNK
ATOM = 16
assert SSM_CHUNK == SSM_GROUPS == TILE_CHUNKS == SSM_GROUP == BATCH == ATOM


def _swap_rows_matrix():
    r = np.arange(ATOM * ATOM)
    perm = np.zeros((ATOM * ATOM, ATOM * ATOM), np.float32)
    perm[r, (r % ATOM) * ATOM + r // ATOM] = 1.0
    return jnp.asarray(perm, _BF16)


def _atom_transpose(blocks):
    out = list(blocks)
    for i in range(ATOM // 2):
        (a_lo, a_hi), (b_lo, b_hi) = blocks[i], blocks[i + ATOM // 2]
        out[i], out[i + ATOM // 2] = (a_lo, b_lo), (a_hi, b_hi)
    blocks = out
    lane_atom = lax.broadcasted_iota(jnp.int32, (ATOM, 128), 1) // ATOM
    for s in (4, 2, 1):
        keep = (lane_atom & s) == 0
        out = list(blocks)
        for i in range(ATOM):
            if i & s:
                continue
            a, b = blocks[i], blocks[i + s]
            out[i] = tuple(jnp.where(keep, x, pltpu.roll(y, s * ATOM, 1)) for x, y in zip(a, b))
            out[i + s] = tuple(jnp.where(keep, pltpu.roll(x, 128 - s * ATOM, 1), y) for x, y in zip(a, b))
        blocks = out
    return blocks


def _tile_to_chunk_layout(u_tile, swap_ref):
    z = jnp.dot(swap_ref[...], u_tile.astype(_BF16), preferred_element_type=_F32)
    blocks = [(z[i * ATOM:(i + 1) * ATOM, 0:128], z[i * ATOM:(i + 1) * ATOM, 128:256]) for i in range(ATOM)]
    return [jnp.concatenate(pair, axis=1) for pair in _atom_transpose(blocks)]


def _chunk_layout_to_tile(y_groups, swap_ref):
    blocks = [(y_groups[g, :, 0:128].astype(_F32), y_groups[g, :, 128:256].astype(_F32)) for g in range(ATOM)]
    z = jnp.concatenate([jnp.concatenate(pair, axis=1) for pair in _atom_transpose(blocks)], axis=0)
    return jnp.dot(swap_ref[...], z.astype(_BF16), preferred_element_type=_F32)


def _swiglu(h, wg_ref, wu_ref, wd_ref):
    a = jnp.dot(h, wg_ref[...], preferred_element_type=_F32)
    b = jnp.dot(h, wu_ref[...], preferred_element_type=_F32)
    s = ((a * jax.nn.sigmoid(a)) * b).astype(_BF16)
    return jnp.dot(s, wd_ref[...], preferred_element_type=_F32)


def _ffn_pair(xs, mods, gpre, gpost, wg_ref, wu_ref, wd_ref):
    hs = [_modulated(x, gpre, sh[...], sc[...]).astype(_BF16) for x, (sh, sc, _) in zip(xs, mods)]
    ys = [_swiglu(h, wg_ref, wu_ref, wd_ref) for h in hs]
    return [x + MACARON * (gt[...] * _rms(y, gpost)) for x, y, (_, _, gt) in zip(xs, ys, mods)]


def _pre_kernel(*refs, split_input):
    n_x = 2 if split_input else 1
    halves = []
    for h in range(2):
        group, refs = refs[:n_x + 7], refs[n_x + 7:]
        halves.append(group)
    (gpre0_ref, gpost0_ref, gpre1_ref, wg_ref, wu_ref, wd_ref, w_ref, swap_ref,
     xo_ref, q_ref, k_ref, v_ref, u_ref, uc_ref, gb_ref, p_ref) = refs
    xs = []
    for h, group in enumerate(halves):
        if split_input:
            is_ctx = _stream_tile(pl.program_id(0), h) % N_TILES == 0
            xs.append(jnp.where(is_ctx, group[0][...], group[1][...]))
        else:
            xs.append(group[0][...])
    mods0 = [group[n_x:n_x + 3] for group in halves]
    xs = _ffn_pair(xs, mods0, gpre0_ref[...], gpost0_ref[...], wg_ref, wu_ref, wd_ref)
    for h, x in enumerate(xs):
        xo_ref[h * TILE:(h + 1) * TILE, :] = x
    hs = [_modulated(x, gpre1_ref[...], group[n_x + 3][...], group[n_x + 4][...]).astype(_BF16)
          for x, group in zip(xs, halves)]
    prs = [jnp.dot(h, w_ref[...], preferred_element_type=_F32) for h in hs]
    lane = lax.broadcasted_iota(jnp.int32, (TILE, 128), 1)
    first_half = (lane % (2 * ROPE_PAIRS)) < ROPE_PAIRS
    for h, (pr, group) in enumerate(zip(prs, halves)):
        rows = slice(h * TILE, (h + 1) * TILE)
        cosv = group[n_x + 5][...]
        sinv = group[n_x + 6][...]

        def rope(t, cosv=cosv, sinv=sinv):
            partner = jnp.where(first_half, pltpu.roll(t, 128 - ROPE_PAIRS, 1), pltpu.roll(t, ROPE_PAIRS, 1))
            return t * cosv + partner * sinv

        for j in range(ATTN_WIDTH // 128):
            q_ref[rows, j * 128:(j + 1) * 128] = (
                rope(pr[:, j * 128:(j + 1) * 128]) * (ATTN_SCALE * LOG2_E)).astype(_BF16)
        c = ATTN_WIDTH
        k_ref[rows, :] = rope(pr[:, c:c + KV_WIDTH]).astype(_BF16)
        c += KV_WIDTH
        ones = jnp.ones((TILE, HEAD_DIM), _F32)
        v_ref[rows, :] = jnp.concatenate(
            [piece for kv in range(KV_HEADS)
             for piece in (pr[:, c + kv * HEAD_DIM:c + (kv + 1) * HEAD_DIM], ones)], axis=1).astype(_BF16)
        c += KV_WIDTH
        u_ref[rows, :] = pr[:, c:c + SSM_WIDTH]
        for g, u_group in enumerate(_tile_to_chunk_layout(pr[:, c:c + SSM_WIDTH], swap_ref)):
            uc_ref[g, h, :, :] = u_group.astype(_BF16)
        c += SSM_WIDTH
        gb_ref[rows, :] = pr[:, c:c + CONV_WIDTH]
        c += CONV_WIDTH
        p_ref[rows, :] = pr[:, c:c + CONV_WIDTH] * pr[:, c + CONV_WIDTH:c + 2 * CONV_WIDTH]


def _pre(x_in, mods, norm_pre, norm_post, wg, wu, wd, w_in, cos_t, sin_t, layer):
    split_input = isinstance(x_in, tuple)
    tile = _stream_tile
    in_specs, args = [], []
    for h in range(2):
        if split_input:
            in_specs += [
                pl.BlockSpec((None, CTX_LEN, D_MODEL), lambda p, h=h: (tile(p, h) // N_TILES, 0, 0)),
                pl.BlockSpec((None, TILE, D_MODEL),
                             lambda p, h=h: (tile(p, h) // N_TILES, jnp.maximum(tile(p, h) % N_TILES - 1, 0), 0)),
            ]
            args += list(x_in)
        else:
            in_specs.append(_row_spec(D_MODEL, tile, h))
            args.append(x_in)
        in_specs += [_mod_spec(layer, j, tile, h) for j in range(5)]
        args += [mods] * 5
        in_specs += [pl.BlockSpec((TILE, 128), lambda p, h=h: (tile(p, h) % N_TILES, 0)) for _ in range(2)]
        args += [cos_t, sin_t]
    in_specs += [_norm_spec(layer, 0), _norm_spec(layer, 0), _norm_spec(layer, 1),
                 *_ffn_weight_specs(layer, 0), _const_spec((None, D_MODEL, IN_COLS), (layer, 0, 0)),
                 _const_spec((TILE, TILE), (0, 0))]
    args += [norm_pre, norm_post, norm_pre, wg, wu, wd, w_in, _swap_rows_matrix()]
    pair = lambda w: pl.BlockSpec((2 * TILE, w), lambda p: (p, 0))
    tok = lambda w, dt: jax.ShapeDtypeStruct((BATCH * TOKENS, w), dt)
    return pl.pallas_call(
        functools.partial(_pre_kernel, split_input=split_input),
        grid=(BATCH * N_TILES // 2,),
        in_specs=in_specs,
        out_specs=[pair(D_MODEL), pair(ATTN_WIDTH), pair(KV_WIDTH), pair(2 * KV_WIDTH), pair(SSM_WIDTH),
                   pl.BlockSpec((SSM_GROUPS, 2, TILE_CHUNKS, SSM_WIDTH), lambda p: (0, p, 0, 0)),
                   pair(CONV_WIDTH), pair(CONV_WIDTH)],
        out_shape=[
            tok(D_MODEL, _F32),
            tok(ATTN_WIDTH, _BF16), tok(KV_WIDTH, _BF16), tok(2 * KV_WIDTH, _BF16),
            tok(SSM_WIDTH, _F32),
            jax.ShapeDtypeStruct((SSM_GROUPS, BATCH * N_TILES, TILE_CHUNKS, SSM_WIDTH), _BF16),
            tok(CONV_WIDTH, _F32), tok(CONV_WIDTH, _F32),
        ],
        compiler_params=_TOKEN_PARAMS,
        name="ffn_in_proj",
    )(*args)


def _dot_nt(a, b):
    return lax.dot_general(a, b, (((1,), (1,)), ((), ())), preferred_element_type=_F32)


def _dot_tn(a, b):
    return lax.dot_general(a, b, (((0,), (0,)), ((), ())), preferred_element_type=_F32)


def _attn_kernel(sink_ref, q0_ref, k0_ref, v0_ref, q1_ref, k1_ref, v1_ref, o_ref):
    halves = []
    for half, (q_ref, k_ref, v_ref) in enumerate(((q0_ref, k0_ref, v0_ref), (q1_ref, k1_ref, v1_ref))):
        i = _stream_tile(pl.program_id(0), half) % N_TILES
        q0 = i * TILE
        start = pl.multiple_of(jnp.clip(q0 - WINDOW, CTX_LEN, TOKENS - N_LOCAL), WINDOW)
        kj = start + lax.broadcasted_iota(jnp.int32, (N_LOCAL, TILE), 0)
        qi = q0 + lax.broadcasted_iota(jnp.int32, (N_LOCAL, TILE), 1)
        valid = (jnp.abs(kj - qi) <= WINDOW) & (i >= 1)
        halves.append(dict(
            q=q_ref[...], bias=jnp.where(valid, 0.0, NEG_INF).astype(_F32),
            k_loc=k_ref[pl.ds(start, N_LOCAL), :], v_loc=v_ref[pl.ds(start, N_LOCAL), :],
            k_ctx=k_ref[0:CTX_LEN, :], v_ctx=v_ref[0:CTX_LEN, :]))

    def scores(unit):
        t, h = halves[unit // N_HEADS], unit % N_HEADS
        lo = (h // Q_PER_KV) * HEAD_DIM
        qh = t["q"][:, h * HEAD_DIM:(h + 1) * HEAD_DIM]
        return (_dot_nt(t["k_loc"][:, lo:lo + HEAD_DIM], qh) + t["bias"],
                _dot_nt(t["k_ctx"][:, lo:lo + HEAD_DIM], qh))

    outs = []
    nxt = scores(0)
    for unit in range(2 * N_HEADS):
        half, h = unit // N_HEADS, unit % N_HEADS
        t = halves[half]
        s_loc, s_ctx = nxt
        if unit + 1 < 2 * N_HEADS:
            nxt = scores(unit + 1)
        sink = sink_ref[h] * LOG2_E
        m = jnp.maximum(jnp.maximum(jnp.max(s_loc, axis=0, keepdims=True),
                                    jnp.max(s_ctx, axis=0, keepdims=True)), sink)
        p_loc = jnp.exp2((s_loc - m).astype(_BF16))
        p_ctx = jnp.exp2((s_ctx - m).astype(_BF16))
        vlo = 2 * (h // Q_PER_KV) * HEAD_DIM
        ov = (_dot_tn(t["v_loc"][:, vlo:vlo + 2 * HEAD_DIM], p_loc)
              + _dot_tn(t["v_ctx"][:, vlo:vlo + 2 * HEAD_DIM], p_ctx))
        den = ov[HEAD_DIM:HEAD_DIM + 1, :] + jnp.exp2(sink - m)
        outs.append(ov[0:HEAD_DIM, :] / den)
        if h % 2 == 1:
            pair = jnp.concatenate(outs, axis=0).T.astype(_BF16)
            o_ref[half * TILE:(half + 1) * TILE, (h - 1) * HEAD_DIM:(h + 1) * HEAD_DIM] = pair
            outs = []


def _attention(sink, q, k, v):
    in_specs, args = [pl.BlockSpec(memory_space=pltpu.SMEM)], [sink]
    for h in range(2):
        in_specs.append(_row_spec(ATTN_WIDTH, _stream_tile, h))
        in_specs += [pl.BlockSpec((None, TOKENS, a.shape[-1]), lambda p, h=h: (_stream_tile(p, h) // N_TILES, 0, 0))
                     for a in (k, v)]
        args += [q, k, v]
    return pl.pallas_call(
        _attn_kernel,
        grid=(BATCH * N_TILES // 2,),
        in_specs=in_specs,
        out_specs=pl.BlockSpec((2 * TILE, ATTN_WIDTH), lambda p: (p, 0)),
        out_shape=jax.ShapeDtypeStruct((BATCH * TOKENS, ATTN_WIDTH), _BF16),
        compiler_params=_TOKEN_PARAMS,
        name="window_attn",
    )(*args)


N_CHUNKS = TOKENS // SSM_CHUNK
N_CTX_CHUNKS = CTX_LEN // SSM_CHUNK
CHUNK_ROWS = N_CHUNKS * BATCH
STATE_LANES = 2 * SSM_STATE


def _s5_kernel(u_ref, swap_ref, w1_ref, w2_ref, a_ref, y_ref, ucb_ref, xw_ref, s_ref):
    tile_rows = BATCH * TILE_CHUNKS
    for j in range(N_TILES):
        rows_bc = jnp.concatenate([u_ref[N_TILES * b + j] for b in range(BATCH)], axis=0)
        ucb_ref[j * tile_rows:(j + 1) * tile_rows, :] = jnp.dot(
            swap_ref[...], rows_bc, preferred_element_type=_F32).astype(_BF16)
    xw_ref[...] = jnp.dot(ucb_ref[...], w1_ref[...], preferred_element_type=_F32)
    a_r = jnp.broadcast_to(a_ref[0:1, :], (BATCH, STATE_LANES))
    a_i = jnp.broadcast_to(a_ref[1:2, :], (BATCH, STATE_LANES))
    fwd_lanes = lax.broadcasted_iota(jnp.int32, (BATCH, STATE_LANES), 1) < SSM_STATE
    x_re, x_im = SSM_WIDTH, SSM_WIDTH + STATE_LANES

    def body(i, carry):
        s_r, s_i = carry
        cb = jnp.where(i < N_CTX_CHUNKS, N_CTX_CHUNKS - 1 - i, N_CHUNKS + N_CTX_CHUNKS - 1 - i)
        rf = pl.ds(pl.multiple_of(i * BATCH, BATCH), BATCH)
        rb = pl.ds(pl.multiple_of(cb * BATCH, BATCH), BATCH)
        s_ref[rf, 0:SSM_STATE] = s_r[:, 0:SSM_STATE]
        s_ref[rb, SSM_STATE:STATE_LANES] = s_r[:, SSM_STATE:STATE_LANES]
        s_ref[rf, STATE_LANES:STATE_LANES + SSM_STATE] = s_i[:, 0:SSM_STATE]
        s_ref[rb, STATE_LANES + SSM_STATE:2 * STATE_LANES] = s_i[:, SSM_STATE:STATE_LANES]
        xr = jnp.where(fwd_lanes, xw_ref[rf, x_re:x_re + STATE_LANES], xw_ref[rb, x_re:x_re + STATE_LANES])
        xi = jnp.where(fwd_lanes, xw_ref[rf, x_im:x_im + STATE_LANES], xw_ref[rb, x_im:x_im + STATE_LANES])
        return a_r * s_r - a_i * s_i + xr, a_r * s_i + a_i * s_r + xi

    zero = jnp.zeros((BATCH, STATE_LANES), _F32)
    lax.fori_loop(0, N_CHUNKS, body, (zero, zero), unroll=8)
    half = CHUNK_ROWS // 2
    for r in (0, half):
        y = xw_ref[r:r + half, 0:SSM_WIDTH] + jnp.dot(
            s_ref[r:r + half, :].astype(_BF16), w2_ref[...], preferred_element_type=_F32)
        ucb_ref[r:r + half, :] = y.astype(_BF16)
    for j in range(N_TILES):
        rows_bc = jnp.dot(swap_ref[...], ucb_ref[j * tile_rows:(j + 1) * tile_rows, :],
                          preferred_element_type=_F32).astype(_BF16)
        for b in range(BATCH):
            y_ref[N_TILES * b + j] = rows_bc[b * TILE_CHUNKS:(b + 1) * TILE_CHUNKS, :]


def _s5_mixer(uc, w1, w2, a_pow):
    per_group = lambda a: pl.BlockSpec((None,) + a.shape[1:], lambda g: (g,) + (0,) * (a.ndim - 1))
    swap = _swap_rows_matrix()
    return pl.pallas_call(
        _s5_kernel,
        grid=(SSM_GROUPS,),
        in_specs=[per_group(uc), pl.BlockSpec(swap.shape, lambda g: (0, 0)),
                  per_group(w1), per_group(w2), per_group(a_pow)],
        out_specs=per_group(uc),
        out_shape=jax.ShapeDtypeStruct(uc.shape, _BF16),
        scratch_shapes=[
            pltpu.VMEM((CHUNK_ROWS, SSM_WIDTH), _BF16),
            pltpu.VMEM((CHUNK_ROWS, SSM_WIDTH + 2 * STATE_LANES), _F32),
            pltpu.VMEM((CHUNK_ROWS, 2 * STATE_LANES), _F32),
        ],
        compiler_params=pltpu.CompilerParams(
            dimension_semantics=("parallel",), vmem_limit_bytes=VMEM_LIMIT),
        name="s5_chunked",
    )(uc, swap, w1, w2, a_pow)


N_ROW_INPUTS = 6


def _post_kernel(*refs, tile_fn):
    halves = []
    for h in range(2):
        group, refs = refs[:N_ROW_INPUTS + 4], refs[N_ROW_INPUTS + 4:]
        halves.append(group)
    (pprev_ref, pnext_ref, gpost1_ref, gpre2_ref, gpost2_ref, d_ref, wglu_ref, bglu_ref, cw_ref,
     wout_ref, wg_ref, wu_ref, wd_ref, swap_ref, o_ref) = refs
    tiles = [tile_fn(pl.program_id(0), h) % N_TILES for h in range(2)]
    ps = [group[5][...] for group in halves]
    edge_before = [jnp.where(tiles[0] >= 2, pprev_ref[7:8, :], 0.0),
                   jnp.where(tiles[1] >= 2, ps[0][TILE - 1:TILE, :], 0.0)]
    edge_after = [jnp.where((tiles[0] >= 1) & (tiles[0] <= N_TILES - 2), ps[1][0:1, :], 0.0),
                  jnp.where((tiles[1] >= 1) & (tiles[1] <= N_TILES - 2), pnext_ref[0:1, :], 0.0)]
    row = lax.broadcasted_iota(jnp.int32, (TILE, CONV_WIDTH), 0)
    cw = cw_ref[...]

    def gated_input(h):
        _, _, u_ref, ys_ref, _, _ = halves[h][:N_ROW_INPUTS]
        y = d_ref[...] * u_ref[...] + _chunk_layout_to_tile(ys_ref, swap_ref)
        g = jax.nn.gelu(y)
        return g, jnp.dot(g.astype(_BF16), wglu_ref[...], preferred_element_type=_F32)

    def mixed_and_modulated(h, g, z):
        group = halves[h]
        x_ref, attn_ref, _, _, gb_ref, _ = group[:N_ROW_INPUTS]
        gt1_ref, sh2_ref, sc2_ref, _ = group[N_ROW_INPUTS:]
        ssm = g * jax.nn.sigmoid(z + bglu_ref[...])
        p = ps[h]
        p_before = jnp.where(row == 0, edge_before[h], pltpu.roll(p, 1, 0))
        p_after = jnp.where(row == TILE - 1, edge_after[h], pltpu.roll(p, TILE - 1, 0))
        conv = gb_ref[...] * (p_before * cw[0:1, :] + p * cw[1:2, :] + p_after * cw[2:3, :])
        mix = jnp.concatenate([attn_ref[...], ssm.astype(_BF16), conv.astype(_BF16)], axis=1)
        yv = jnp.dot(mix, wout_ref[...], preferred_element_type=_F32)
        x = x_ref[...] + gt1_ref[...] * _rms(yv, gpost1_ref[...])
        return x, _modulated(x, gpre2_ref[...], sh2_ref[...], sc2_ref[...]).astype(_BF16)

    gz0, gz1 = gated_input(0), gated_input(1)
    (x0, h0), (x1, h1) = mixed_and_modulated(0, *gz0), mixed_and_modulated(1, *gz1)
    y0 = _swiglu(h0, wg_ref, wu_ref, wd_ref)
    y1 = _swiglu(h1, wg_ref, wu_ref, wd_ref)
    for h, (x, y) in enumerate(((x0, y0), (x1, y1))):
        gt2_ref = halves[h][N_ROW_INPUTS + 3]
        o_ref[h * TILE:(h + 1) * TILE, :] = x + MACARON * (gt2_ref[...] * _rms(y, gpost2_ref[...]))


def _post(x_all, mods, norm_pre, norm_post, attn, u_ssm, y_ssm, gb, p, ssm_d, w_glu, b_glu, conv_w,
          w_out, wg, wu, wd, layer, latent_only):
    tile = _latent_tile if latent_only else _stream_tile
    n_pairs = BATCH * (N_TILES - 1) // 2 if latent_only else BATCH * N_TILES // 2
    rows8 = TILE // 8
    in_specs, args = [], []
    for h in range(2):
        for arr in (x_all, attn, u_ssm, y_ssm, gb, p):
            if arr is y_ssm:
                in_specs.append(pl.BlockSpec((SSM_GROUPS, None, TILE_CHUNKS, SSM_WIDTH),
                                             lambda p_, h=h: (0, tile(p_, h), 0, 0)))
            else:
                in_specs.append(_row_spec(arr.shape[1], tile, h))
            args.append(arr)
        in_specs += [_mod_spec(layer, j, tile, h) for j in (5, 6, 7, 8)]
        args += [mods] * 4
    in_specs += [
        pl.BlockSpec((8, CONV_WIDTH), lambda p_: (jnp.maximum(tile(p_, 0) * rows8 - 1, 0), 0)),
        pl.BlockSpec((8, CONV_WIDTH),
                     lambda p_: (jnp.minimum((tile(p_, 1) + 1) * rows8, BATCH * TOKENS // 8 - 1), 0)),
        _norm_spec(layer, 1), _norm_spec(layer, 2), _norm_spec(layer, 2),
        _const_spec((None, 1, SSM_WIDTH), (layer, 0, 0)),
        _const_spec((None, SSM_WIDTH, SSM_WIDTH), (layer, 0, 0)),
        _const_spec((None, 1, SSM_WIDTH), (layer, 0, 0)),
        _const_spec((None, 3, CONV_WIDTH), (layer, 0, 0)),
        _const_spec((None, D_MODEL, D_MODEL), (layer, 0, 0)),
        *_ffn_weight_specs(layer, 1),
        _const_spec((TILE, TILE), (0, 0)),
    ]
    args += [p, p, norm_post, norm_pre, norm_post, ssm_d, w_glu, b_glu, conv_w, w_out, wg, wu, wd,
             _swap_rows_matrix()]
    return pl.pallas_call(
        functools.partial(_post_kernel, tile_fn=tile),
        grid=(n_pairs,),
        in_specs=in_specs,
        out_specs=pl.BlockSpec((2 * TILE, D_MODEL), lambda p_: (p_, 0)),
        out_shape=jax.ShapeDtypeStruct((n_pairs * 2 * TILE, D_MODEL), _F32),
        compiler_params=_TOKEN_PARAMS,
        name="mix_out_ffn",
    )(*args)


def _rope_tables():
    pos = jnp.arange(SEQ)
    row = (pos // GRID_W).astype(_F32)
    col = (pos % GRID_W).astype(_F32)
    inv_freq = ROPE_BASE ** (-jnp.arange(ROPE_PAIRS, dtype=_F32) / ROPE_PAIRS)
    ang = jnp.stack([row[:, None] * inv_freq, col[:, None] * inv_freq], axis=1)
    cos, sin = jnp.cos(ang), jnp.sin(ang)
    cos_h = jnp.concatenate([cos, cos], axis=-1).reshape(SEQ, HEAD_DIM)
    sin_h = jnp.concatenate([-sin, sin], axis=-1).reshape(SEQ, HEAD_DIM)
    cos_t = jnp.concatenate([jnp.ones((CTX_LEN, HEAD_DIM), _F32), cos_h], axis=0)
    sin_t = jnp.concatenate([jnp.zeros((CTX_LEN, HEAD_DIM), _F32), sin_h], axis=0)
    return jnp.tile(cos_t, (1, 128 // HEAD_DIM)), jnp.tile(sin_t, (1, 128 // HEAD_DIM))


def _s5_matrices(lam_re, lam_im, log_step, b_re, b_im, c_re, c_im):
    n, grp = SSM_CHUNK, SSM_GROUP
    k = jnp.arange(n + 1, dtype=_F32)
    dt = jnp.exp(log_step)[..., None, None]
    lr, li = lam_re[..., None], lam_im[..., None]
    mag = jnp.exp(k * lr * dt)
    pr, pi = mag * jnp.cos(k * li * dt), mag * jnp.sin(k * li * dt)
    ar, ai = pr[..., 1], pi[..., 1]
    den = lam_re * lam_re + lam_im * lam_im
    gr = ((ar - 1) * lam_re + ai * lam_im) / den
    gi = (ai * lam_re - (ar - 1) * lam_im) / den
    bbr = gr[..., None] * b_re - gi[..., None] * b_im
    bbi = gr[..., None] * b_im + gi[..., None] * b_re
    ct_re, ct_im = jnp.swapaxes(c_re, -1, -2), jnp.swapaxes(c_im, -1, -2)
    car = ct_re[..., None, :] * pr[..., None] - ct_im[..., None, :] * pi[..., None]
    cai = ct_re[..., None, :] * pi[..., None] + ct_im[..., None, :] * pr[..., None]
    kt = jnp.einsum('ldgpj,ldgpkh->ldgjkh', bbr, car) - jnp.einsum('ldgpj,ldgpkh->ldgjkh', bbi, cai)
    lanes = lambda m: m.reshape(m.shape[:-2] + (m.shape[-2] * m.shape[-1],))
    kf = lanes(kt[:, 0])
    kb = lanes(kt[:, 1, :, :, ::-1])
    pad = lambda m, lo, hi: jnp.pad(m, ((0, 0),) * (m.ndim - 1) + ((lo, hi),))
    intra = jnp.stack(
        [pad(kf[..., :(n - tau) * grp], tau * grp, 0)
         + pad(kb[..., (n - tau) * grp:], 0, (n - 1 - tau) * grp) for tau in range(n)],
        axis=2).reshape(DEPTH, SSM_GROUPS, n * grp, n * grp)

    def state_in(d, qr, qi):
        qr, qi = jnp.moveaxis(qr, -1, -2)[..., None, :], jnp.moveaxis(qi, -1, -2)[..., None, :]
        br, bi = jnp.swapaxes(bbr[:, d], -1, -2)[:, :, None], jnp.swapaxes(bbi[:, d], -1, -2)[:, :, None]
        rows = lambda m: m.reshape(DEPTH, SSM_GROUPS, n * grp, SSM_STATE)
        return rows(qr * br - qi * bi), rows(qr * bi + qi * br)

    f_re, f_im = state_in(0, pr[:, 0, ..., n - 1::-1], pi[:, 0, ..., n - 1::-1])
    b_re_, b_im_ = state_in(1, pr[:, 1, ..., :n], pi[:, 1, ..., :n])
    w1 = jnp.concatenate([intra, f_re, b_re_, f_im, b_im_], axis=-1)

    fo_re, fo_im = lanes(car[:, 0, ..., 1:, :]), -lanes(cai[:, 0, ..., 1:, :])
    bo_re, bo_im = lanes(car[:, 1, ..., :0:-1, :]), -lanes(cai[:, 1, ..., :0:-1, :])
    w2 = jnp.concatenate([fo_re, bo_re, fo_im, bo_im], axis=-2)
    a_pow = jnp.stack([jnp.concatenate([pr[:, 0, ..., n], pr[:, 1, ..., n]], axis=-1),
                       jnp.concatenate([pi[:, 0, ..., n], pi[:, 1, ..., n]], axis=-1)], axis=-2)
    return w1.astype(_BF16), w2.astype(_BF16), a_pow


def kernel(x, c, ctx, c_ctx, w_ada, b_ada, norm_pre, norm_post, ffn_w_gate, ffn_w_up, ffn_w_down,
           w_in, w_out, attn_sink, ssm_lambda_re, ssm_lambda_im, ssm_log_step, ssm_b_re, ssm_b_im,
           ssm_c_re, ssm_c_im, ssm_d, ssm_w_glu, ssm_b_glu, conv_w):
    cc = jnp.concatenate([c, c_ctx[None, :], jnp.zeros((MOD_ROWS - BATCH - 1, D_MODEL), _F32)], axis=0)
    mods = _ada_table(cc, w_ada, b_ada)
    cos_t, sin_t = _rope_tables()
    w1, w2, a_pow = _s5_matrices(ssm_lambda_re, ssm_lambda_im, ssm_log_step,
                                 ssm_b_re, ssm_b_im, ssm_c_re, ssm_c_im)
    npre = norm_pre.reshape(DEPTH, 3, 1, D_MODEL)
    npost = norm_post.reshape(DEPTH, 3, 1, D_MODEL)
    wg, wu, wd = (w.astype(_BF16) for w in (ffn_w_gate, ffn_w_up, ffn_w_down))
    w_in_b, w_out_b, w_glu_b = (w.astype(_BF16) for w in (w_in, w_out, ssm_w_glu))
    d_skip = ssm_d.reshape(DEPTH, 1, SSM_WIDTH)
    b_glu = ssm_b_glu.reshape(DEPTH, 1, SSM_WIDTH)

    xa = (ctx, x)
    seq = lambda t: t.reshape(BATCH, TOKENS, t.shape[-1])
    for l in range(DEPTH):
        xa, q, k, v, u_ssm, u_chunks, gb, p = _pre(xa, mods, npre, npost, wg, wu, wd, w_in_b, cos_t, sin_t, l)
        attn = _attention(attn_sink[l], q, seq(k), seq(v))
        y_ssm = _s5_mixer(u_chunks, w1[l], w2[l], a_pow[l])
        xa = _post(xa, mods, npre, npost, attn, u_ssm, y_ssm, gb, p, d_skip, w_glu_b,
                   b_glu, conv_w, w_out_b, wg, wu, wd, l, latent_only=l == DEPTH - 1)
    return xa.reshape(BATCH, SEQ, D_MODEL)
```

```python
import functools
import math

import jax
import jax.numpy as jnp
import numpy as np
from jax import lax
from jax.experimental import pallas as pl
from jax.experimental.pallas import tpu as pltpu

D_MODEL = 1024
BATCH = 16
SEQ = 4096
DEPTH = 4
GRID_W = 64
CTX_LEN = 256
HEAD_DIM = 64
ATTN_WIDTH = D_MODEL // 2
N_HEADS = ATTN_WIDTH // HEAD_DIM
KV_HEADS = N_HEADS // 4
Q_PER_KV = N_HEADS // KV_HEADS
KV_WIDTH = KV_HEADS * HEAD_DIM
WINDOW = 128
ATTN_SCALE = HEAD_DIM ** -0.5
LOG2_E = math.log2(math.e)
ROPE_BASE = 10000.0
ROPE_PAIRS = HEAD_DIM // 4
SSM_WIDTH = D_MODEL // 4
SSM_GROUP = 16
SSM_GROUPS = SSM_WIDTH // SSM_GROUP
SSM_STATE = 64
CONV_WIDTH = D_MODEL // 4
IN_COLS = ATTN_WIDTH + 2 * KV_WIDTH + SSM_WIDTH + 3 * CONV_WIDTH
D_FF = ((8 * D_MODEL // 3 + 127) // 128) * 128
MACARON = 0.5
N_MOD = 9
EPS = 1e-6
NEG_INF = -1e30

TOKENS = CTX_LEN + SEQ
TILE = 256
N_TILES = TOKENS // TILE
MOD_ROWS = 24
CTX_ROW = BATCH
N_LOCAL = TILE + 2 * WINDOW
VMEM_LIMIT = 48 * 1024 * 1024

_F32 = jnp.float32
_BF16 = jnp.bfloat16


def _rms(x, g):
    return (x * lax.rsqrt(jnp.mean(x * x, axis=-1, keepdims=True) + EPS)) * g


def _modulated(x, g, shift, scale):
    return _rms(x, g) * (1 + scale) + shift


def _ada_kernel(cc_ref, w_ref, b_ref, o_ref):
    cc = cc_ref[...]
    s = cc * jax.nn.sigmoid(cc)
    o_ref[...] = jnp.dot(s, w_ref[...], precision=lax.Precision.HIGHEST,
                         preferred_element_type=_F32) + b_ref[...]


def _ada_table(cc, w_ada, b_ada):
    out = pl.pallas_call(
        _ada_kernel,
        grid=(DEPTH, N_MOD),
        in_specs=[
            pl.BlockSpec((MOD_ROWS, D_MODEL), lambda l, j: (0, 0)),
            pl.BlockSpec((None, D_MODEL, D_MODEL), lambda l, j: (l, 0, j)),
            pl.BlockSpec((None, None, 1, D_MODEL), lambda l, j: (l, j, 0, 0)),
        ],
        out_specs=pl.BlockSpec((None, None, MOD_ROWS, D_MODEL), lambda l, j: (l, j, 0, 0)),
        out_shape=jax.ShapeDtypeStruct((DEPTH, N_MOD, MOD_ROWS, D_MODEL), _F32),
        name="ada_table",
    )(cc, w_ada, b_ada.reshape(DEPTH, N_MOD, 1, D_MODEL))
    return out.reshape(DEPTH, N_MOD, MOD_ROWS, 1, D_MODEL)


def _stream_tile(p, h):
    return 2 * p + h


def _latent_tile(p, h):
    per_batch = (N_TILES - 1) // 2
    return (p // per_batch) * N_TILES + 1 + 2 * (p % per_batch) + h


def _row_spec(width, tile_fn, h):
    return pl.BlockSpec((TILE, width), lambda p: (tile_fn(p, h), 0))


def _mod_spec(layer, j, tile_fn, h):
    def index(p):
        t = tile_fn(p, h)
        return (layer, j, jnp.where(t % N_TILES == 0, CTX_ROW, t // N_TILES), 0, 0)
    return pl.BlockSpec((None, None, None, 1, D_MODEL), index)


def _const_spec(shape, index):
    return pl.BlockSpec(shape, lambda p: index, pipeline_mode=pl.Buffered(1))


def _norm_spec(layer, sub):
    return _const_spec((None, None, 1, D_MODEL), (layer, sub, 0, 0))


def _ffn_weight_specs(layer, which):
    return [
        _const_spec((None, None, D_MODEL, 2 * D_FF), (layer, which, 0, 0)),
        _const_spec((None, None, D_FF, D_MODEL), (layer, which, 0, 0)),
    ]


_TOKEN_PARAMS = pltpu.CompilerParams(dimension_semantics=("parallel",), vmem_limit_bytes=VMEM_LIMIT)


SSM_CHUNK = 256 // SSM_GROUP
TILE_CHUNKS = TILE // SSM_CHUNK
ATOM = 16
assert SSM_CHUNK == SSM_GROUPS == TILE_CHUNKS == SSM_GROUP == BATCH == ATOM


def _swap_rows_matrix():
    r = np.arange(ATOM * ATOM)
    perm = np.zeros((ATOM * ATOM, ATOM * ATOM), np.float32)
    perm[r, (r % ATOM) * ATOM + r // ATOM] = 1.0
    return jnp.asarray(perm, _BF16)


def _atom_transpose(blocks):
    out = list(blocks)
    for i in range(ATOM // 2):
        (a_lo, a_hi), (b_lo, b_hi) = blocks[i], blocks[i + ATOM // 2]
        out[i], out[i + ATOM // 2] = (a_lo, b_lo), (a_hi, b_hi)
    blocks = out
    lane_atom = lax.broadcasted_iota(jnp.int32, (ATOM, 128), 1) // ATOM
    for s in (4, 2, 1):
        keep = (lane_atom & s) == 0
        out = list(blocks)
        for i in range(ATOM):
            if i & s:
                continue
            a, b = blocks[i], blocks[i + s]
            out[i] = tuple(jnp.where(keep, x, pltpu.roll(y, s * ATOM, 1)) for x, y in zip(a, b))
            out[i + s] = tuple(jnp.where(keep, pltpu.roll(x, 128 - s * ATOM, 1), y) for x, y in zip(a, b))
        blocks = out
    return blocks


def _tile_to_chunk_layout(u_tile, swap_ref):
    z = jnp.dot(swap_ref[...], u_tile.astype(_BF16), preferred_element_type=_F32)
    blocks = [(z[i * ATOM:(i + 1) * ATOM, 0:128], z[i * ATOM:(i + 1) * ATOM, 128:256]) for i in range(ATOM)]
    return [jnp.concatenate(pair, axis=1) for pair in _atom_transpose(blocks)]


def _chunk_layout_to_tile(y_groups, swap_ref):
    blocks = [(y_groups[g, :, 0:128].astype(_F32), y_groups[g, :, 128:256].astype(_F32)) for g in range(ATOM)]
    z = jnp.concatenate([jnp.concatenate(pair, axis=1) for pair in _atom_transpose(blocks)], axis=0)
    return jnp.dot(swap_ref[...], z.astype(_BF16), preferred_element_type=_F32)


def _swiglu(h, wgu_ref, wd_ref):
    ab = jnp.dot(h, wgu_ref[...], preferred_element_type=_F32)
    a, b = ab[:, :D_FF], ab[:, D_FF:]
    s = ((a * jax.nn.sigmoid(a)) * b).astype(_BF16)
    return jnp.dot(s, wd_ref[...], preferred_element_type=_F32)


def _ffn_pair(xs, mods, gpre, gpost, wgu_ref, wd_ref):
    hs = [_modulated(x, gpre, sh[...], sc[...]).astype(_BF16) for x, (sh, sc, _) in zip(xs, mods)]
    ys = [_swiglu(h, wgu_ref, wd_ref) for h in hs]
    return [x + MACARON * (gt[...] * _rms(y, gpost)) for x, y, (_, _, gt) in zip(xs, ys, mods)]


def _pre_kernel(*refs, split_input):
    n_x = 2 if split_input else 1
    halves = []
    for h in range(2):
        group, refs = refs[:n_x + 7], refs[n_x + 7:]
        halves.append(group)
    (gpre0_ref, gpost0_ref, gpre1_ref, wgu_ref, wd_ref, w_ref, swap_ref,
     xo_ref, q_ref, k_ref, v_ref, u_ref, uc_ref, gb_ref, p_ref) = refs
    xs = []
    for h, group in enumerate(halves):
        if split_input:
            is_ctx = _stream_tile(pl.program_id(0), h) % N_TILES == 0
            xs.append(jnp.where(is_ctx, group[0][...], group[1][...]))
        else:
            xs.append(group[0][...])
    mods0 = [group[n_x:n_x + 3] for group in halves]
    xs = _ffn_pair(xs, mods0, gpre0_ref[...], gpost0_ref[...], wgu_ref, wd_ref)
    for h, x in enumerate(xs):
        xo_ref[h * TILE:(h + 1) * TILE, :] = x
    hs = [_modulated(x, gpre1_ref[...], group[n_x + 3][...], group[n_x + 4][...]).astype(_BF16)
          for x, group in zip(xs, halves)]
    prs = [jnp.dot(h, w_ref[...], preferred_element_type=_F32) for h in hs]
    lane = lax.broadcasted_iota(jnp.int32, (TILE, 128), 1)
    first_half = (lane % (2 * ROPE_PAIRS)) < ROPE_PAIRS
    for h, (pr, group) in enumerate(zip(prs, halves)):
        rows = slice(h * TILE, (h + 1) * TILE)
        cosv = group[n_x + 5][...]
        sinv = group[n_x + 6][...]

        def rope(t, cosv=cosv, sinv=sinv):
            partner = jnp.where(first_half, pltpu.roll(t, 128 - ROPE_PAIRS, 1), pltpu.roll(t, ROPE_PAIRS, 1))
            return t * cosv + partner * sinv

        for j in range(ATTN_WIDTH // 128):
            q_ref[rows, j * 128:(j + 1) * 128] = (
                rope(pr[:, j * 128:(j + 1) * 128]) * (ATTN_SCALE * LOG2_E)).astype(_BF16)
        c = ATTN_WIDTH
        k_ref[rows, :] = rope(pr[:, c:c + KV_WIDTH]).astype(_BF16)
        c += KV_WIDTH
        ones = jnp.ones((TILE, HEAD_DIM), _F32)
        v_ref[rows, :] = jnp.concatenate(
            [piece for kv in range(KV_HEADS)
             for piece in (pr[:, c + kv * HEAD_DIM:c + (kv + 1) * HEAD_DIM], ones)], axis=1).astype(_BF16)
        c += KV_WIDTH
        u_ref[rows, :] = pr[:, c:c + SSM_WIDTH]
        for g, u_group in enumerate(_tile_to_chunk_layout(pr[:, c:c + SSM_WIDTH], swap_ref)):
            uc_ref[g, h, :, :] = u_group.astype(_BF16)
        c += SSM_WIDTH
        gb_ref[rows, :] = pr[:, c:c + CONV_WIDTH]
        c += CONV_WIDTH
        p_ref[rows, :] = pr[:, c:c + CONV_WIDTH] * pr[:, c + CONV_WIDTH:c + 2 * CONV_WIDTH]


def _pre(x_in, mods, norm_pre, norm_post, wgu, wd, w_in, cos_t, sin_t, layer):
    split_input = isinstance(x_in, tuple)
    tile = _stream_tile
    in_specs, args = [], []
    for h in range(2):
        if split_input:
            in_specs += [
                pl.BlockSpec((None, CTX_LEN, D_MODEL), lambda p, h=h: (tile(p, h) // N_TILES, 0, 0)),
                pl.BlockSpec((None, TILE, D_MODEL),
                             lambda p, h=h: (tile(p, h) // N_TILES, jnp.maximum(tile(p, h) % N_TILES - 1, 0), 0)),
            ]
            args += list(x_in)
        else:
            in_specs.append(_row_spec(D_MODEL, tile, h))
            args.append(x_in)
        in_specs += [_mod_spec(layer, j, tile, h) for j in range(5)]
        args += [mods] * 5
        in_specs += [pl.BlockSpec((TILE, 128), lambda p, h=h: (tile(p, h) % N_TILES, 0)) for _ in range(2)]
        args += [cos_t, sin_t]
    in_specs += [_norm_spec(layer, 0), _norm_spec(layer, 0), _norm_spec(layer, 1),
                 *_ffn_weight_specs(layer, 0), _const_spec((None, D_MODEL, IN_COLS), (layer, 0, 0)),
                 _const_spec((TILE, TILE), (0, 0))]
    args += [norm_pre, norm_post, norm_pre, wgu, wd, w_in, _swap_rows_matrix()]
    pair = lambda w: pl.BlockSpec((2 * TILE, w), lambda p: (p, 0))
    tok = lambda w, dt: jax.ShapeDtypeStruct((BATCH * TOKENS, w), dt)
    return pl.pallas_call(
        functools.partial(_pre_kernel, split_input=split_input),
        grid=(BATCH * N_TILES // 2,),
        in_specs=in_specs,
        out_specs=[pair(D_MODEL), pair(ATTN_WIDTH), pair(KV_WIDTH), pair(2 * KV_WIDTH), pair(SSM_WIDTH),
                   pl.BlockSpec((SSM_GROUPS, 2, TILE_CHUNKS, SSM_WIDTH), lambda p: (0, p, 0, 0)),
                   pair(CONV_WIDTH), pair(CONV_WIDTH)],
        out_shape=[
            tok(D_MODEL, _F32),
            tok(ATTN_WIDTH, _BF16), tok(KV_WIDTH, _BF16), tok(2 * KV_WIDTH, _BF16),
            tok(SSM_WIDTH, _F32),
            jax.ShapeDtypeStruct((SSM_GROUPS, BATCH * N_TILES, TILE_CHUNKS, SSM_WIDTH), _BF16),
            tok(CONV_WIDTH, _F32), tok(CONV_WIDTH, _F32),
        ],
        compiler_params=_TOKEN_PARAMS,
        name="ffn_in_proj",
    )(*args)


def _dot_nt(a, b):
    return lax.dot_general(a, b, (((1,), (1,)), ((), ())), preferred_element_type=_F32)


def _dot_tn(a, b):
    return lax.dot_general(a, b, (((0,), (0,)), ((), ())), preferred_element_type=_F32)


SCORES_AHEAD = 2


def _attn_kernel(sink_ref, q0_ref, k0_ref, v0_ref, q1_ref, k1_ref, v1_ref, o_ref):
    halves = []
    for half, (q_ref, k_ref, v_ref) in enumerate(((q0_ref, k0_ref, v0_ref), (q1_ref, k1_ref, v1_ref))):
        i = _stream_tile(pl.program_id(0), half) % N_TILES
        q0 = i * TILE
        start = pl.multiple_of(jnp.clip(q0 - WINDOW, CTX_LEN, TOKENS - N_LOCAL), WINDOW)
        kj = start + lax.broadcasted_iota(jnp.int32, (N_LOCAL, TILE), 0)
        qi = q0 + lax.broadcasted_iota(jnp.int32, (N_LOCAL, TILE), 1)
        valid = (jnp.abs(kj - qi) <= WINDOW) & (i >= 1)
        halves.append(dict(
            q=q_ref[...], bias=jnp.where(valid, 0.0, NEG_INF).astype(_F32),
            k_loc=k_ref[pl.ds(start, N_LOCAL), :], v_loc=v_ref[pl.ds(start, N_LOCAL), :],
            k_ctx=k_ref[0:CTX_LEN, :], v_ctx=v_ref[0:CTX_LEN, :]))

    def scores(unit):
        t, h = halves[unit // N_HEADS], unit % N_HEADS
        lo = (h // Q_PER_KV) * HEAD_DIM
        qh = t["q"][:, h * HEAD_DIM:(h + 1) * HEAD_DIM]
        return (_dot_nt(t["k_loc"][:, lo:lo + HEAD_DIM], qh) + t["bias"],
                _dot_nt(t["k_ctx"][:, lo:lo + HEAD_DIM], qh))

    outs = []

    def values(unit, p_loc, p_ctx, m, sink):
        half, h = unit // N_HEADS, unit % N_HEADS
        t = halves[half]
        vlo = 2 * (h // Q_PER_KV) * HEAD_DIM
        ov = (_dot_tn(t["v_loc"][:, vlo:vlo + 2 * HEAD_DIM], p_loc)
              + _dot_tn(t["v_ctx"][:, vlo:vlo + 2 * HEAD_DIM], p_ctx))
        den = ov[HEAD_DIM:HEAD_DIM + 1, :] + jnp.exp2(sink - m)
        outs.append(ov[0:HEAD_DIM, :] / den)
        if h % 2 == 1:
            pair = jnp.concatenate(outs, axis=0).T.astype(_BF16)
            o_ref[half * TILE:(half + 1) * TILE, (h - 1) * HEAD_DIM:(h + 1) * HEAD_DIM] = pair
            outs.clear()

    ahead = [scores(u) for u in range(SCORES_AHEAD)]
    pending = None
    for unit in range(2 * N_HEADS):
        s_loc, s_ctx = ahead.pop(0)
        if unit + SCORES_AHEAD < 2 * N_HEADS:
            ahead.append(scores(unit + SCORES_AHEAD))
        if pending is not None:
            values(*pending)
        sink = sink_ref[unit % N_HEADS] * LOG2_E
        m = jnp.maximum(jnp.maximum(jnp.max(s_loc, axis=0, keepdims=True),
                                    jnp.max(s_ctx, axis=0, keepdims=True)), sink)
        pending = (unit, jnp.exp2((s_loc - m).astype(_BF16)), jnp.exp2((s_ctx - m).astype(_BF16)), m, sink)
    values(*pending)


def _attention(sink, q, k, v):
    in_specs, args = [pl.BlockSpec(memory_space=pltpu.SMEM)], [sink]
    for h in range(2):
        in_specs.append(_row_spec(ATTN_WIDTH, _stream_tile, h))
        in_specs += [pl.BlockSpec((None, TOKENS, a.shape[-1]), lambda p, h=h: (_stream_tile(p, h) // N_TILES, 0, 0))
                     for a in (k, v)]
        args += [q, k, v]
    return pl.pallas_call(
        _attn_kernel,
        grid=(BATCH * N_TILES // 2,),
        in_specs=in_specs,
        out_specs=pl.BlockSpec((2 * TILE, ATTN_WIDTH), lambda p: (p, 0)),
        out_shape=jax.ShapeDtypeStruct((BATCH * TOKENS, ATTN_WIDTH), _BF16),
        compiler_params=_TOKEN_PARAMS,
        name="window_attn",
    )(*args)


N_CHUNKS = TOKENS // SSM_CHUNK
N_CTX_CHUNKS = CTX_LEN // SSM_CHUNK
CHUNK_ROWS = N_CHUNKS * BATCH
STATE_LANES = 2 * SSM_STATE


def _s5_kernel(u_ref, swap_ref, w1_ref, w2_ref, a_ref, y_ref, ucb_ref, xw_ref, s_ref):
    tile_rows = BATCH * TILE_CHUNKS
    for j in range(N_TILES):
        rows_bc = jnp.concatenate([u_ref[N_TILES * b + j] for b in range(BATCH)], axis=0)
        ucb_ref[j * tile_rows:(j + 1) * tile_rows, :] = jnp.dot(
            swap_ref[...], rows_bc, preferred_element_type=_F32).astype(_BF16)
    xw_ref[...] = jnp.dot(ucb_ref[...], w1_ref[...], preferred_element_type=_F32)
    a_r = jnp.broadcast_to(a_ref[0:1, :], (BATCH, STATE_LANES))
    a_i = jnp.broadcast_to(a_ref[1:2, :], (BATCH, STATE_LANES))
    fwd_lanes = lax.broadcasted_iota(jnp.int32, (BATCH, STATE_LANES), 1) < SSM_STATE
    x_re, x_im = SSM_WIDTH, SSM_WIDTH + STATE_LANES

    def body(i, carry):
        s_r, s_i = carry
        cb = jnp.where(i < N_CTX_CHUNKS, N_CTX_CHUNKS - 1 - i, N_CHUNKS + N_CTX_CHUNKS - 1 - i)
        rf = pl.ds(pl.multiple_of(i * BATCH, BATCH), BATCH)
        rb = pl.ds(pl.multiple_of(cb * BATCH, BATCH), BATCH)
        s_ref[rf, 0:SSM_STATE] = s_r[:, 0:SSM_STATE]
        s_ref[rb, SSM_STATE:STATE_LANES] = s_r[:, SSM_STATE:STATE_LANES]
        s_ref[rf, STATE_LANES:STATE_LANES + SSM_STATE] = s_i[:, 0:SSM_STATE]
        s_ref[rb, STATE_LANES + SSM_STATE:2 * STATE_LANES] = s_i[:, SSM_STATE:STATE_LANES]
        xr = jnp.where(fwd_lanes, xw_ref[rf, x_re:x_re + STATE_LANES], xw_ref[rb, x_re:x_re + STATE_LANES])
        xi = jnp.where(fwd_lanes, xw_ref[rf, x_im:x_im + STATE_LANES], xw_ref[rb, x_im:x_im + STATE_LANES])
        return a_r * s_r - a_i * s_i + xr, a_r * s_i + a_i * s_r + xi

    zero = jnp.zeros((BATCH, STATE_LANES), _F32)
    lax.fori_loop(0, N_CHUNKS, body, (zero, zero), unroll=8)
    half = CHUNK_ROWS // 2
    for r in (0, half):
        y = xw_ref[r:r + half, 0:SSM_WIDTH] + jnp.dot(
            s_ref[r:r + half, :].astype(_BF16), w2_ref[...], preferred_element_type=_F32)
        ucb_ref[r:r + half, :] = y.astype(_BF16)
    for j in range(N_TILES):
        rows_bc = jnp.dot(swap_ref[...], ucb_ref[j * tile_rows:(j + 1) * tile_rows, :],
                          preferred_element_type=_F32).astype(_BF16)
        for b in range(BATCH):
            y_ref[N_TILES * b + j] = rows_bc[b * TILE_CHUNKS:(b + 1) * TILE_CHUNKS, :]


def _s5_mixer(uc, w1, w2, a_pow):
    per_group = lambda a: pl.BlockSpec((None,) + a.shape[1:], lambda g: (g,) + (0,) * (a.ndim - 1))
    swap = _swap_rows_matrix()
    return pl.pallas_call(
        _s5_kernel,
        grid=(SSM_GROUPS,),
        in_specs=[per_group(uc), pl.BlockSpec(swap.shape, lambda g: (0, 0)),
                  per_group(w1), per_group(w2), per_group(a_pow)],
        out_specs=per_group(uc),
        out_shape=jax.ShapeDtypeStruct(uc.shape, _BF16),
        scratch_shapes=[
            pltpu.VMEM((CHUNK_ROWS, SSM_WIDTH), _BF16),
            pltpu.VMEM((CHUNK_ROWS, SSM_WIDTH + 2 * STATE_LANES), _F32),
            pltpu.VMEM((CHUNK_ROWS, 2 * STATE_LANES), _F32),
        ],
        compiler_params=pltpu.CompilerParams(
            dimension_semantics=("parallel",), vmem_limit_bytes=VMEM_LIMIT),
        name="s5_chunked",
    )(uc, swap, w1, w2, a_pow)


N_ROW_INPUTS = 6


def _post_kernel(*refs, tile_fn):
    halves = []
    for h in range(2):
        group, refs = refs[:N_ROW_INPUTS + 4], refs[N_ROW_INPUTS + 4:]
        halves.append(group)
    (pprev_ref, pnext_ref, gpost1_ref, gpre2_ref, gpost2_ref, d_ref, wglu_ref, bglu_ref, cw_ref,
     wout_ref, wgu_ref, wd_ref, swap_ref, o_ref) = refs
    tiles = [tile_fn(pl.program_id(0), h) % N_TILES for h in range(2)]
    ps = [group[5][...] for group in halves]
    edge_before = [jnp.where(tiles[0] >= 2, pprev_ref[7:8, :], 0.0),
                   jnp.where(tiles[1] >= 2, ps[0][TILE - 1:TILE, :], 0.0)]
    edge_after = [jnp.where((tiles[0] >= 1) & (tiles[0] <= N_TILES - 2), ps[1][0:1, :], 0.0),
                  jnp.where((tiles[1] >= 1) & (tiles[1] <= N_TILES - 2), pnext_ref[0:1, :], 0.0)]
    row = lax.broadcasted_iota(jnp.int32, (TILE, CONV_WIDTH), 0)
    cw = cw_ref[...]

    def gated_input(h):
        _, _, u_ref, ys_ref, _, _ = halves[h][:N_ROW_INPUTS]
        y = d_ref[...] * u_ref[...] + _chunk_layout_to_tile(ys_ref, swap_ref)
        g = jax.nn.gelu(y)
        return g, jnp.dot(g.astype(_BF16), wglu_ref[...], preferred_element_type=_F32)

    def mixed_and_modulated(h, g, z):
        group = halves[h]
        x_ref, attn_ref, _, _, gb_ref, _ = group[:N_ROW_INPUTS]
        gt1_ref, sh2_ref, sc2_ref, _ = group[N_ROW_INPUTS:]
        ssm = g * jax.nn.sigmoid(z + bglu_ref[...])
        p = ps[h]
        p_before = jnp.where(row == 0, edge_before[h], pltpu.roll(p, 1, 0))
        p_after = jnp.where(row == TILE - 1, edge_after[h], pltpu.roll(p, TILE - 1, 0))
        conv = gb_ref[...] * (p_before * cw[0:1, :] + p * cw[1:2, :] + p_after * cw[2:3, :])
        mix = jnp.concatenate([attn_ref[...], ssm.astype(_BF16), conv.astype(_BF16)], axis=1)
        yv = jnp.dot(mix, wout_ref[...], preferred_element_type=_F32)
        x = x_ref[...] + gt1_ref[...] * _rms(yv, gpost1_ref[...])
        return x, _modulated(x, gpre2_ref[...], sh2_ref[...], sc2_ref[...]).astype(_BF16)

    gz0, gz1 = gated_input(0), gated_input(1)
    (x0, h0), (x1, h1) = mixed_and_modulated(0, *gz0), mixed_and_modulated(1, *gz1)
    y0 = _swiglu(h0, wgu_ref, wd_ref)
    y1 = _swiglu(h1, wgu_ref, wd_ref)
    for h, (x, y) in enumerate(((x0, y0), (x1, y1))):
        gt2_ref = halves[h][N_ROW_INPUTS + 3]
        o_ref[h * TILE:(h + 1) * TILE, :] = x + MACARON * (gt2_ref[...] * _rms(y, gpost2_ref[...]))


def _post(x_all, mods, norm_pre, norm_post, attn, u_ssm, y_ssm, gb, p, ssm_d, w_glu, b_glu, conv_w,
          w_out, wgu, wd, layer, latent_only):
    tile = _latent_tile if latent_only else _stream_tile
    n_pairs = BATCH * (N_TILES - 1) // 2 if latent_only else BATCH * N_TILES // 2
    rows8 = TILE // 8
    in_specs, args = [], []
    for h in range(2):
        for arr in (x_all, attn, u_ssm, y_ssm, gb, p):
            if arr is y_ssm:
                in_specs.append(pl.BlockSpec((SSM_GROUPS, None, TILE_CHUNKS, SSM_WIDTH),
                                             lambda p_, h=h: (0, tile(p_, h), 0, 0)))
            else:
                in_specs.append(_row_spec(arr.shape[1], tile, h))
            args.append(arr)
        in_specs += [_mod_spec(layer, j, tile, h) for j in (5, 6, 7, 8)]
        args += [mods] * 4
    in_specs += [
        pl.BlockSpec((8, CONV_WIDTH), lambda p_: (jnp.maximum(tile(p_, 0) * rows8 - 1, 0), 0)),
        pl.BlockSpec((8, CONV_WIDTH),
                     lambda p_: (jnp.minimum((tile(p_, 1) + 1) * rows8, BATCH * TOKENS // 8 - 1), 0)),
        _norm_spec(layer, 1), _norm_spec(layer, 2), _norm_spec(layer, 2),
        _const_spec((None, 1, SSM_WIDTH), (layer, 0, 0)),
        _const_spec((None, SSM_WIDTH, SSM_WIDTH), (layer, 0, 0)),
        _const_spec((None, 1, SSM_WIDTH), (layer, 0, 0)),
        _const_spec((None, 3, CONV_WIDTH), (layer, 0, 0)),
        _const_spec((None, D_MODEL, D_MODEL), (layer, 0, 0)),
        *_ffn_weight_specs(layer, 1),
        _const_spec((TILE, TILE), (0, 0)),
    ]
    args += [p, p, norm_post, norm_pre, norm_post, ssm_d, w_glu, b_glu, conv_w, w_out, wgu, wd,
             _swap_rows_matrix()]
    return pl.pallas_call(
        functools.partial(_post_kernel, tile_fn=tile),
        grid=(n_pairs,),
        in_specs=in_specs,
        out_specs=pl.BlockSpec((2 * TILE, D_MODEL), lambda p_: (p_, 0)),
        out_shape=jax.ShapeDtypeStruct((n_pairs * 2 * TILE, D_MODEL), _F32),
        compiler_params=_TOKEN_PARAMS,
        name="mix_out_ffn",
    )(*args)


def _rope_tables():
    pos = jnp.arange(SEQ)
    row = (pos // GRID_W).astype(_F32)
    col = (pos % GRID_W).astype(_F32)
    inv_freq = ROPE_BASE ** (-jnp.arange(ROPE_PAIRS, dtype=_F32) / ROPE_PAIRS)
    ang = jnp.stack([row[:, None] * inv_freq, col[:, None] * inv_freq], axis=1)
    cos, sin = jnp.cos(ang), jnp.sin(ang)
    cos_h = jnp.concatenate([cos, cos], axis=-1).reshape(SEQ, HEAD_DIM)
    sin_h = jnp.concatenate([-sin, sin], axis=-1).reshape(SEQ, HEAD_DIM)
    cos_t = jnp.concatenate([jnp.ones((CTX_LEN, HEAD_DIM), _F32), cos_h], axis=0)
    sin_t = jnp.concatenate([jnp.zeros((CTX_LEN, HEAD_DIM), _F32), sin_h], axis=0)
    return jnp.tile(cos_t, (1, 128 // HEAD_DIM)), jnp.tile(sin_t, (1, 128 // HEAD_DIM))


def _s5_matrices(lam_re, lam_im, log_step, b_re, b_im, c_re, c_im):
    n, grp = SSM_CHUNK, SSM_GROUP
    k = jnp.arange(n + 1, dtype=_F32)
    dt = jnp.exp(log_step)[..., None, None]
    lr, li = lam_re[..., None], lam_im[..., None]
    mag = jnp.exp(k * lr * dt)
    pr, pi = mag * jnp.cos(k * li * dt), mag * jnp.sin(k * li * dt)
    ar, ai = pr[..., 1], pi[..., 1]
    den = lam_re * lam_re + lam_im * lam_im
    gr = ((ar - 1) * lam_re + ai * lam_im) / den
    gi = (ai * lam_re - (ar - 1) * lam_im) / den
    bbr = gr[..., None] * b_re - gi[..., None] * b_im
    bbi = gr[..., None] * b_im + gi[..., None] * b_re
    ct_re, ct_im = jnp.swapaxes(c_re, -1, -2), jnp.swapaxes(c_im, -1, -2)
    car = ct_re[..., None, :] * pr[..., None] - ct_im[..., None, :] * pi[..., None]
    cai = ct_re[..., None, :] * pi[..., None] + ct_im[..., None, :] * pr[..., None]
    kt = jnp.einsum('ldgpj,ldgpkh->ldgjkh', bbr, car) - jnp.einsum('ldgpj,ldgpkh->ldgjkh', bbi, cai)
    lanes = lambda m: m.reshape(m.shape[:-2] + (m.shape[-2] * m.shape[-1],))
    kf = lanes(kt[:, 0])
    kb = lanes(kt[:, 1, :, :, ::-1])
    pad = lambda m, lo, hi: jnp.pad(m, ((0, 0),) * (m.ndim - 1) + ((lo, hi),))
    intra = jnp.stack(
        [pad(kf[..., :(n - tau) * grp], tau * grp, 0)
         + pad(kb[..., (n - tau) * grp:], 0, (n - 1 - tau) * grp) for tau in range(n)],
        axis=2).reshape(DEPTH, SSM_GROUPS, n * grp, n * grp)

    def state_in(d, qr, qi):
        qr, qi = jnp.moveaxis(qr, -1, -2)[..., None, :], jnp.moveaxis(qi, -1, -2)[..., None, :]
        br, bi = jnp.swapaxes(bbr[:, d], -1, -2)[:, :, None], jnp.swapaxes(bbi[:, d], -1, -2)[:, :, None]
        rows = lambda m: m.reshape(DEPTH, SSM_GROUPS, n * grp, SSM_STATE)
        return rows(qr * br - qi * bi), rows(qr * bi + qi * br)

    f_re, f_im = state_in(0, pr[:, 0, ..., n - 1::-1], pi[:, 0, ..., n - 1::-1])
    b_re_, b_im_ = state_in(1, pr[:, 1, ..., :n], pi[:, 1, ..., :n])
    w1 = jnp.concatenate([intra, f_re, b_re_, f_im, b_im_], axis=-1)

    fo_re, fo_im = lanes(car[:, 0, ..., 1:, :]), -lanes(cai[:, 0, ..., 1:, :])
    bo_re, bo_im = lanes(car[:, 1, ..., :0:-1, :]), -lanes(cai[:, 1, ..., :0:-1, :])
    w2 = jnp.concatenate([fo_re, bo_re, fo_im, bo_im], axis=-2)
    a_pow = jnp.stack([jnp.concatenate([pr[:, 0, ..., n], pr[:, 1, ..., n]], axis=-1),
                       jnp.concatenate([pi[:, 0, ..., n], pi[:, 1, ..., n]], axis=-1)], axis=-2)
    return w1.astype(_BF16), w2.astype(_BF16), a_pow


def kernel(x, c, ctx, c_ctx, w_ada, b_ada, norm_pre, norm_post, ffn_w_gate, ffn_w_up, ffn_w_down,
           w_in, w_out, attn_sink, ssm_lambda_re, ssm_lambda_im, ssm_log_step, ssm_b_re, ssm_b_im,
           ssm_c_re, ssm_c_im, ssm_d, ssm_w_glu, ssm_b_glu, conv_w):
    cc = jnp.concatenate([c, c_ctx[None, :], jnp.zeros((MOD_ROWS - BATCH - 1, D_MODEL), _F32)], axis=0)
    mods = _ada_table(cc, w_ada, b_ada)
    cos_t, sin_t = _rope_tables()
    w1, w2, a_pow = _s5_matrices(ssm_lambda_re, ssm_lambda_im, ssm_log_step,
                                 ssm_b_re, ssm_b_im, ssm_c_re, ssm_c_im)
    npre = norm_pre.reshape(DEPTH, 3, 1, D_MODEL)
    npost = norm_post.reshape(DEPTH, 3, 1, D_MODEL)
    wgu = jnp.concatenate([ffn_w_gate, ffn_w_up], axis=-1).astype(_BF16)
    wd = ffn_w_down.astype(_BF16)
    w_in_b, w_out_b, w_glu_b = (w.astype(_BF16) for w in (w_in, w_out, ssm_w_glu))
    d_skip = ssm_d.reshape(DEPTH, 1, SSM_WIDTH)
    b_glu = ssm_b_glu.reshape(DEPTH, 1, SSM_WIDTH)

    xa = (ctx, x)
    seq = lambda t: t.reshape(BATCH, TOKENS, t.shape[-1])
    for l in range(DEPTH):
        xa, q, k, v, u_ssm, u_chunks, gb, p = _pre(xa, mods, npre, npost, wgu, wd, w_in_b, cos_t, sin_t, l)
        attn = _attention(attn_sink[l], q, seq(k), seq(v))
        y_ssm = _s5_mixer(u_chunks, w1[l], w2[l], a_pow[l])
        xa = _post(xa, mods, npre, npost, attn, u_ssm, y_ssm, gb, p, d_skip, w_glu_b,
                   b_glu, conv_w, w_out_b, wgu, wd, l, latent_only=l == DEPTH - 1)
    return xa.reshape(BATCH, SEQ, D_MODEL)
```

```python
import functools
import math

import jax
import jax.numpy as jnp
import numpy as np
from jax import lax
from jax.experimental import pallas as pl
from jax.experimental.pallas import tpu as pltpu

D_MODEL = 1024
BATCH = 16
SEQ = 4096
DEPTH = 4
GRID_W = 64
CTX_LEN = 256
HEAD_DIM = 64
ATTN_WIDTH = D_MODEL // 2
N_HEADS = ATTN_WIDTH // HEAD_DIM
KV_HEADS = N_HEADS // 4
Q_PER_KV = N_HEADS // KV_HEADS
KV_WIDTH = KV_HEADS * HEAD_DIM
WINDOW = 128
ATTN_SCALE = HEAD_DIM ** -0.5
LOG2_E = math.log2(math.e)
ROPE_BASE = 10000.0
ROPE_PAIRS = HEAD_DIM // 4
SSM_WIDTH = D_MODEL // 4
SSM_GROUP = 16
SSM_GROUPS = SSM_WIDTH // SSM_GROUP
SSM_STATE = 64
CONV_WIDTH = D_MODEL // 4
IN_COLS = ATTN_WIDTH + 2 * KV_WIDTH + SSM_WIDTH + 3 * CONV_WIDTH
D_FF = ((8 * D_MODEL // 3 + 127) // 128) * 128
MACARON = 0.5
N_MOD = 9
EPS = 1e-6
NEG_INF = -1e30

LANES = 128
SUBLANES = 8
V7X_VMEM_BYTES = 64 * 1024 * 1024

TOKENS = CTX_LEN + SEQ
TILE = 256
N_TILES = TOKENS // TILE
MOD_ROWS = -(-(BATCH + 1) // SUBLANES) * SUBLANES
CTX_ROW = BATCH
N_LOCAL = TILE + 2 * WINDOW
VMEM_LIMIT = 3 * V7X_VMEM_BYTES // 4

_F32 = jnp.float32
_BF16 = jnp.bfloat16


def _unit_rms(x):
    return x * lax.rsqrt(jnp.mean(x * x, axis=-1, keepdims=True) + EPS)


def _modulated(x, g, shift, scale):
    return _unit_rms(x) * (g * (1 + scale)) + shift


def _gated_norm(y, g, gate, factor=1.0):
    return _unit_rms(y) * (factor * (gate * g))


def _ada_kernel(cc_ref, w_ref, b_ref, o_ref):
    cc = cc_ref[...]
    s = cc * jax.nn.sigmoid(cc)
    o_ref[...] = jnp.dot(s, w_ref[...], precision=lax.Precision.HIGHEST,
                         preferred_element_type=_F32) + b_ref[...]


def _ada_table(cc, w_ada, b_ada):
    out = pl.pallas_call(
        _ada_kernel,
        grid=(DEPTH, N_MOD),
        in_specs=[
            pl.BlockSpec((MOD_ROWS, D_MODEL), lambda l, j: (0, 0)),
            pl.BlockSpec((None, D_MODEL, D_MODEL), lambda l, j: (l, 0, j)),
            pl.BlockSpec((None, None, 1, D_MODEL), lambda l, j: (l, j, 0, 0)),
        ],
        out_specs=pl.BlockSpec((None, None, MOD_ROWS, D_MODEL), lambda l, j: (l, j, 0, 0)),
        out_shape=jax.ShapeDtypeStruct((DEPTH, N_MOD, MOD_ROWS, D_MODEL), _F32),
        name="ada_table",
    )(cc, w_ada, b_ada.reshape(DEPTH, N_MOD, 1, D_MODEL))
    return out.reshape(DEPTH, N_MOD, MOD_ROWS, 1, D_MODEL)


def _stream_tile(p, h):
    return 2 * p + h


def _latent_tile(p, h):
    per_batch = (N_TILES - 1) // 2
    return (p // per_batch) * N_TILES + 1 + 2 * (p % per_batch) + h


def _row_spec(width, tile_fn, h):
    return pl.BlockSpec((TILE, width), lambda p: (tile_fn(p, h), 0))


def _mod_spec(layer, j, tile_fn, h):
    def index(p):
        t = tile_fn(p, h)
        return (layer, j, jnp.where(t % N_TILES == 0, CTX_ROW, t // N_TILES), 0, 0)
    return pl.BlockSpec((None, None, None, 1, D_MODEL), index)


def _const_spec(shape, index):
    return pl.BlockSpec(shape, lambda p: index, pipeline_mode=pl.Buffered(1))


def _norm_spec(layer, sub):
    return _const_spec((None, None, 1, D_MODEL), (layer, sub, 0, 0))


def _ffn_weight_specs(layer, which):
    return [
        _const_spec((None, None, D_MODEL, 2 * D_FF), (layer, which, 0, 0)),
        _const_spec((None, None, D_FF, D_MODEL), (layer, which, 0, 0)),
    ]


_TOKEN_PARAMS = pltpu.CompilerParams(dimension_semantics=("parallel",), vmem_limit_bytes=VMEM_LIMIT)


SSM_CHUNK = SSM_WIDTH // SSM_GROUP
TILE_CHUNKS = TILE // SSM_CHUNK
ATOM = 16
assert SSM_CHUNK == SSM_GROUPS == TILE_CHUNKS == SSM_GROUP == BATCH == ATOM


def _swap_rows_matrix():
    r = np.arange(ATOM * ATOM)
    perm = np.zeros((ATOM * ATOM, ATOM * ATOM), np.float32)
    perm[r, (r % ATOM) * ATOM + r // ATOM] = 1.0
    return jnp.asarray(perm, _BF16)


def _atom_transpose(blocks):
    out = list(blocks)
    for i in range(ATOM // 2):
        (a_lo, a_hi), (b_lo, b_hi) = blocks[i], blocks[i + ATOM // 2]
        out[i], out[i + ATOM // 2] = (a_lo, b_lo), (a_hi, b_hi)
    blocks = out
    lane_atom = lax.broadcasted_iota(jnp.int32, (ATOM, LANES), 1) // ATOM
    for s in (4, 2, 1):
        keep = (lane_atom & s) == 0
        out = list(blocks)
        for i in range(ATOM):
            if i & s:
                continue
            a, b = blocks[i], blocks[i + s]
            out[i] = tuple(jnp.where(keep, x, pltpu.roll(y, s * ATOM, 1)) for x, y in zip(a, b))
            out[i + s] = tuple(jnp.where(keep, pltpu.roll(x, LANES - s * ATOM, 1), y) for x, y in zip(a, b))
        blocks = out
    return blocks


def _tile_to_chunk_layout(u_tile, swap_ref):
    z = jnp.dot(swap_ref[...], u_tile.astype(_BF16), preferred_element_type=_F32)
    blocks = [(z[i * ATOM:(i + 1) * ATOM, :LANES], z[i * ATOM:(i + 1) * ATOM, LANES:]) for i in range(ATOM)]
    return [jnp.concatenate(pair, axis=1) for pair in _atom_transpose(blocks)]


def _chunk_layout_to_tile(y_groups, swap_ref):
    blocks = [(y_groups[g, :, :LANES].astype(_F32), y_groups[g, :, LANES:].astype(_F32)) for g in range(ATOM)]
    z = jnp.concatenate([jnp.concatenate(pair, axis=1) for pair in _atom_transpose(blocks)], axis=0)
    return jnp.dot(swap_ref[...], z.astype(_BF16), preferred_element_type=_F32)


def _swiglu(h, wgu_ref, wd_ref):
    ab = jnp.dot(h, wgu_ref[...], preferred_element_type=_F32)
    a, b = ab[:, :D_FF], ab[:, D_FF:]
    s = ((a * jax.nn.sigmoid(a)) * b).astype(_BF16)
    return jnp.dot(s, wd_ref[...], preferred_element_type=_F32)


def _ffn_pair(xs, mods, gpre, gpost, wgu_ref, wd_ref):
    hs = [_modulated(x, gpre, sh[...], sc[...]).astype(_BF16) for x, (sh, sc, _) in zip(xs, mods)]
    ys = [_swiglu(h, wgu_ref, wd_ref) for h in hs]
    return [x + _gated_norm(y, gpost, gt[...], MACARON) for x, y, (_, _, gt) in zip(xs, ys, mods)]


def _pre_kernel(*refs, split_input):
    n_x = 2 if split_input else 1
    halves = []
    for h in range(2):
        group, refs = refs[:n_x + 7], refs[n_x + 7:]
        halves.append(group)
    (gpre0_ref, gpost0_ref, gpre1_ref, wgu_ref, wd_ref, w_ref, swap_ref,
     xo_ref, q_ref, k_ref, v_ref, u_ref, uc_ref, gb_ref, p_ref) = refs
    xs = []
    for h, group in enumerate(halves):
        if split_input:
            is_ctx = _stream_tile(pl.program_id(0), h) % N_TILES == 0
            xs.append(jnp.where(is_ctx, group[0][...], group[1][...]))
        else:
            xs.append(group[0][...])
    mods0 = [group[n_x:n_x + 3] for group in halves]
    xs = _ffn_pair(xs, mods0, gpre0_ref[...], gpost0_ref[...], wgu_ref, wd_ref)
    for h, x in enumerate(xs):
        xo_ref[h * TILE:(h + 1) * TILE, :] = x
    hs = [_modulated(x, gpre1_ref[...], group[n_x + 3][...], group[n_x + 4][...]).astype(_BF16)
          for x, group in zip(xs, halves)]
    prs = [jnp.dot(h, w_ref[...], preferred_element_type=_F32) for h in hs]
    lane = lax.broadcasted_iota(jnp.int32, (TILE, LANES), 1)
    first_half = (lane % (2 * ROPE_PAIRS)) < ROPE_PAIRS
    for h, (pr, group) in enumerate(zip(prs, halves)):
        rows = slice(h * TILE, (h + 1) * TILE)
        cosv = group[n_x + 5][...]
        sinv = group[n_x + 6][...]

        def rope(t, cosv=cosv, sinv=sinv):
            partner = jnp.where(first_half, pltpu.roll(t, LANES - ROPE_PAIRS, 1), pltpu.roll(t, ROPE_PAIRS, 1))
            return t * cosv + partner * sinv

        for j in range(0, ATTN_WIDTH, LANES):
            q_ref[rows, j:j + LANES] = (rope(pr[:, j:j + LANES]) * (ATTN_SCALE * LOG2_E)).astype(_BF16)
        c = ATTN_WIDTH
        k_ref[rows, :] = rope(pr[:, c:c + KV_WIDTH]).astype(_BF16)
        c += KV_WIDTH
        ones = jnp.ones((TILE, HEAD_DIM), _F32)
        v_ref[rows, :] = jnp.concatenate(
            [piece for kv in range(KV_HEADS)
             for piece in (pr[:, c + kv * HEAD_DIM:c + (kv + 1) * HEAD_DIM], ones)], axis=1).astype(_BF16)
        c += KV_WIDTH
        u_ref[rows, :] = pr[:, c:c + SSM_WIDTH]
        for g, u_group in enumerate(_tile_to_chunk_layout(pr[:, c:c + SSM_WIDTH], swap_ref)):
            uc_ref[g, h, :, :] = u_group.astype(_BF16)
        c += SSM_WIDTH
        gb_ref[rows, :] = pr[:, c:c + CONV_WIDTH]
        c += CONV_WIDTH
        p_ref[rows, :] = pr[:, c:c + CONV_WIDTH] * pr[:, c + CONV_WIDTH:c + 2 * CONV_WIDTH]


def _pre(x_in, mods, norm_pre, norm_post, wgu, wd, w_in, cos_t, sin_t, layer):
    split_input = isinstance(x_in, tuple)
    tile = _stream_tile
    in_specs, args = [], []
    for h in range(2):
        if split_input:
            in_specs += [
                pl.BlockSpec((None, CTX_LEN, D_MODEL), lambda p, h=h: (tile(p, h) // N_TILES, 0, 0)),
                pl.BlockSpec((None, TILE, D_MODEL),
                             lambda p, h=h: (tile(p, h) // N_TILES, jnp.maximum(tile(p, h) % N_TILES - 1, 0), 0)),
            ]
            args += list(x_in)
        else:
            in_specs.append(_row_spec(D_MODEL, tile, h))
            args.append(x_in)
        in_specs += [_mod_spec(layer, j, tile, h) for j in range(5)]
        args += [mods] * 5
        in_specs += [pl.BlockSpec((TILE, LANES), lambda p, h=h: (tile(p, h) % N_TILES, 0)) for _ in range(2)]
        args += [cos_t, sin_t]
    in_specs += [_norm_spec(layer, 0), _norm_spec(layer, 0), _norm_spec(layer, 1),
                 *_ffn_weight_specs(layer, 0), _const_spec((None, D_MODEL, IN_COLS), (layer, 0, 0)),
                 _const_spec((TILE, TILE), (0, 0))]
    args += [norm_pre, norm_post, norm_pre, wgu, wd, w_in, _swap_rows_matrix()]
    pair = lambda w: pl.BlockSpec((2 * TILE, w), lambda p: (p, 0))
    tok = lambda w, dt: jax.ShapeDtypeStruct((BATCH * TOKENS, w), dt)
    return pl.pallas_call(
        functools.partial(_pre_kernel, split_input=split_input),
        grid=(BATCH * N_TILES // 2,),
        in_specs=in_specs,
        out_specs=[pair(D_MODEL), pair(ATTN_WIDTH), pair(KV_WIDTH), pair(2 * KV_WIDTH), pair(SSM_WIDTH),
                   pl.BlockSpec((SSM_GROUPS, 2, TILE_CHUNKS, SSM_WIDTH), lambda p: (0, p, 0, 0)),
                   pair(CONV_WIDTH), pair(CONV_WIDTH)],
        out_shape=[
            tok(D_MODEL, _F32),
            tok(ATTN_WIDTH, _BF16), tok(KV_WIDTH, _BF16), tok(2 * KV_WIDTH, _BF16),
            tok(SSM_WIDTH, _F32),
            jax.ShapeDtypeStruct((SSM_GROUPS, BATCH * N_TILES, TILE_CHUNKS, SSM_WIDTH), _BF16),
            tok(CONV_WIDTH, _F32), tok(CONV_WIDTH, _F32),
        ],
        compiler_params=_TOKEN_PARAMS,
        name="ffn_in_proj",
    )(*args)


def _dot_nt(a, b):
    return lax.dot_general(a, b, (((1,), (1,)), ((), ())), preferred_element_type=_F32)


def _dot_tn(a, b):
    return lax.dot_general(a, b, (((0,), (0,)), ((), ())), preferred_element_type=_F32)


SCORES_AHEAD = 2


def _attn_kernel(sink_ref, q0_ref, k0_ref, v0_ref, q1_ref, k1_ref, v1_ref, o_ref):
    halves = []
    for half, (q_ref, k_ref, v_ref) in enumerate(((q0_ref, k0_ref, v0_ref), (q1_ref, k1_ref, v1_ref))):
        i = _stream_tile(pl.program_id(0), half) % N_TILES
        q0 = i * TILE
        start = pl.multiple_of(jnp.clip(q0 - WINDOW, CTX_LEN, TOKENS - N_LOCAL), WINDOW)
        kj = start + lax.broadcasted_iota(jnp.int32, (N_LOCAL, TILE), 0)
        qi = q0 + lax.broadcasted_iota(jnp.int32, (N_LOCAL, TILE), 1)
        valid = (jnp.abs(kj - qi) <= WINDOW) & (i >= 1)
        halves.append(dict(
            q=q_ref[...], bias=jnp.where(valid, 0.0, NEG_INF).astype(_F32),
            k_loc=k_ref[pl.ds(start, N_LOCAL), :], v_loc=v_ref[pl.ds(start, N_LOCAL), :],
            k_ctx=k_ref[0:CTX_LEN, :], v_ctx=v_ref[0:CTX_LEN, :]))

    def scores(unit):
        t, h = halves[unit // N_HEADS], unit % N_HEADS
        lo = (h // Q_PER_KV) * HEAD_DIM
        qh = t["q"][:, h * HEAD_DIM:(h + 1) * HEAD_DIM]
        return (_dot_nt(t["k_loc"][:, lo:lo + HEAD_DIM], qh) + t["bias"],
                _dot_nt(t["k_ctx"][:, lo:lo + HEAD_DIM], qh))

    outs = []

    def values(unit, p_loc, p_ctx, m, sink):
        half, h = unit // N_HEADS, unit % N_HEADS
        t = halves[half]
        vlo = 2 * (h // Q_PER_KV) * HEAD_DIM
        ov = (_dot_tn(t["v_loc"][:, vlo:vlo + 2 * HEAD_DIM], p_loc)
              + _dot_tn(t["v_ctx"][:, vlo:vlo + 2 * HEAD_DIM], p_ctx))
        den = ov[HEAD_DIM:HEAD_DIM + 1, :] + jnp.exp2(sink - m)
        outs.append(ov[0:HEAD_DIM, :] / den)
        if h % 2 == 1:
            pair = jnp.concatenate(outs, axis=0).T.astype(_BF16)
            o_ref[half * TILE:(half + 1) * TILE, (h - 1) * HEAD_DIM:(h + 1) * HEAD_DIM] = pair
            outs.clear()

    ahead = [scores(u) for u in range(SCORES_AHEAD)]
    pending = None
    for unit in range(2 * N_HEADS):
        s_loc, s_ctx = ahead.pop(0)
        if unit + SCORES_AHEAD < 2 * N_HEADS:
            ahead.append(scores(unit + SCORES_AHEAD))
        if pending is not None:
            values(*pending)
        sink = sink_ref[unit % N_HEADS] * LOG2_E
        m = jnp.maximum(jnp.maximum(jnp.max(s_loc, axis=0, keepdims=True),
                                    jnp.max(s_ctx, axis=0, keepdims=True)), sink)
        pending = (unit, jnp.exp2((s_loc - m).astype(_BF16)), jnp.exp2((s_ctx - m).astype(_BF16)), m, sink)
    values(*pending)


def _attention(sink, q, k, v):
    in_specs, args = [pl.BlockSpec(memory_space=pltpu.SMEM)], [sink]
    for h in range(2):
        in_specs.append(_row_spec(ATTN_WIDTH, _stream_tile, h))
        in_specs += [pl.BlockSpec((None, TOKENS, a.shape[-1]), lambda p, h=h: (_stream_tile(p, h) // N_TILES, 0, 0))
                     for a in (k, v)]
        args += [q, k, v]
    return pl.pallas_call(
        _attn_kernel,
        grid=(BATCH * N_TILES // 2,),
        in_specs=in_specs,
        out_specs=pl.BlockSpec((2 * TILE, ATTN_WIDTH), lambda p: (p, 0)),
        out_shape=jax.ShapeDtypeStruct((BATCH * TOKENS, ATTN_WIDTH), _BF16),
        compiler_params=_TOKEN_PARAMS,
        name="window_attn",
    )(*args)


N_CHUNKS = TOKENS // SSM_CHUNK
N_CTX_CHUNKS = CTX_LEN // SSM_CHUNK
CHUNK_ROWS = N_CHUNKS * BATCH
STATE_LANES = 2 * SSM_STATE


def _s5_kernel(u_ref, swap_ref, w1_ref, w2_ref, a_ref, y_ref, ucb_ref, xw_ref, s_ref):
    tile_rows = BATCH * TILE_CHUNKS
    for j in range(N_TILES):
        rows_bc = jnp.concatenate([u_ref[N_TILES * b + j] for b in range(BATCH)], axis=0)
        ucb_ref[j * tile_rows:(j + 1) * tile_rows, :] = jnp.dot(
            swap_ref[...], rows_bc, preferred_element_type=_F32).astype(_BF16)
    xw_ref[...] = jnp.dot(ucb_ref[...], w1_ref[...], preferred_element_type=_F32)
    a_r = jnp.broadcast_to(a_ref[0:1, :], (BATCH, STATE_LANES))
    a_i = jnp.broadcast_to(a_ref[1:2, :], (BATCH, STATE_LANES))
    fwd_lanes = lax.broadcasted_iota(jnp.int32, (BATCH, STATE_LANES), 1) < SSM_STATE
    x_re, x_im = SSM_WIDTH, SSM_WIDTH + STATE_LANES

    def body(i, carry):
        s_r, s_i = carry
        cb = jnp.where(i < N_CTX_CHUNKS, N_CTX_CHUNKS - 1 - i, N_CHUNKS + N_CTX_CHUNKS - 1 - i)
        rf = pl.ds(pl.multiple_of(i * BATCH, BATCH), BATCH)
        rb = pl.ds(pl.multiple_of(cb * BATCH, BATCH), BATCH)
        s_ref[rf, 0:SSM_STATE] = s_r[:, 0:SSM_STATE]
        s_ref[rb, SSM_STATE:STATE_LANES] = s_r[:, SSM_STATE:STATE_LANES]
        s_ref[rf, STATE_LANES:STATE_LANES + SSM_STATE] = s_i[:, 0:SSM_STATE]
        s_ref[rb, STATE_LANES + SSM_STATE:2 * STATE_LANES] = s_i[:, SSM_STATE:STATE_LANES]
        xr = jnp.where(fwd_lanes, xw_ref[rf, x_re:x_re + STATE_LANES], xw_ref[rb, x_re:x_re + STATE_LANES])
        xi = jnp.where(fwd_lanes, xw_ref[rf, x_im:x_im + STATE_LANES], xw_ref[rb, x_im:x_im + STATE_LANES])
        return a_r * s_r - a_i * s_i + xr, a_r * s_i + a_i * s_r + xi

    zero = jnp.zeros((BATCH, STATE_LANES), _F32)
    lax.fori_loop(0, N_CHUNKS, body, (zero, zero), unroll=8)
    half = CHUNK_ROWS // 2
    for r in (0, half):
        y = xw_ref[r:r + half, 0:SSM_WIDTH] + jnp.dot(
            s_ref[r:r + half, :].astype(_BF16), w2_ref[...], preferred_element_type=_F32)
        ucb_ref[r:r + half, :] = y.astype(_BF16)
    for j in range(N_TILES):
        rows_bc = jnp.dot(swap_ref[...], ucb_ref[j * tile_rows:(j + 1) * tile_rows, :],
                          preferred_element_type=_F32).astype(_BF16)
        for b in range(BATCH):
            y_ref[N_TILES * b + j] = rows_bc[b * TILE_CHUNKS:(b + 1) * TILE_CHUNKS, :]


def _s5_mixer(uc, w1, w2, a_pow):
    per_group = lambda a: pl.BlockSpec((None,) + a.shape[1:], lambda g: (g,) + (0,) * (a.ndim - 1))
    swap = _swap_rows_matrix()
    return pl.pallas_call(
        _s5_kernel,
        grid=(SSM_GROUPS,),
        in_specs=[per_group(uc), pl.BlockSpec(swap.shape, lambda g: (0, 0)),
                  per_group(w1), per_group(w2), per_group(a_pow)],
        out_specs=per_group(uc),
        out_shape=jax.ShapeDtypeStruct(uc.shape, _BF16),
        scratch_shapes=[
            pltpu.VMEM((CHUNK_ROWS, SSM_WIDTH), _BF16),
            pltpu.VMEM((CHUNK_ROWS, SSM_WIDTH + 2 * STATE_LANES), _F32),
            pltpu.VMEM((CHUNK_ROWS, 2 * STATE_LANES), _F32),
        ],
        compiler_params=pltpu.CompilerParams(
            dimension_semantics=("parallel",), vmem_limit_bytes=VMEM_LIMIT),
        name="s5_chunked",
    )(uc, swap, w1, w2, a_pow)


N_ROW_INPUTS = 6


def _post_kernel(*refs, tile_fn):
    halves = []
    for h in range(2):
        group, refs = refs[:N_ROW_INPUTS + 4], refs[N_ROW_INPUTS + 4:]
        halves.append(group)
    (pprev_ref, pnext_ref, gpost1_ref, gpre2_ref, gpost2_ref, d_ref, wglu_ref, bglu_ref, cw_ref,
     wout_ref, wgu_ref, wd_ref, swap_ref, o_ref) = refs
    tiles = [tile_fn(pl.program_id(0), h) % N_TILES for h in range(2)]
    ps = [group[5][...] for group in halves]
    edge_before = [jnp.where(tiles[0] >= 2, pprev_ref[SUBLANES - 1:SUBLANES, :], 0.0),
                   jnp.where(tiles[1] >= 2, ps[0][TILE - 1:TILE, :], 0.0)]
    edge_after = [jnp.where((tiles[0] >= 1) & (tiles[0] <= N_TILES - 2), ps[1][0:1, :], 0.0),
                  jnp.where((tiles[1] >= 1) & (tiles[1] <= N_TILES - 2), pnext_ref[0:1, :], 0.0)]
    row = lax.broadcasted_iota(jnp.int32, (TILE, CONV_WIDTH), 0)
    cw = cw_ref[...]

    def gated_input(h):
        _, _, u_ref, ys_ref, _, _ = halves[h][:N_ROW_INPUTS]
        y = d_ref[...] * u_ref[...] + _chunk_layout_to_tile(ys_ref, swap_ref)
        g = jax.nn.gelu(y)
        return g, jnp.dot(g.astype(_BF16), wglu_ref[...], preferred_element_type=_F32)

    def mixed_and_modulated(h, g, z):
        group = halves[h]
        x_ref, attn_ref, _, _, gb_ref, _ = group[:N_ROW_INPUTS]
        gt1_ref, sh2_ref, sc2_ref, _ = group[N_ROW_INPUTS:]
        ssm = g * jax.nn.sigmoid(z + bglu_ref[...])
        p = ps[h]
        p_before = jnp.where(row == 0, edge_before[h], pltpu.roll(p, 1, 0))
        p_after = jnp.where(row == TILE - 1, edge_after[h], pltpu.roll(p, TILE - 1, 0))
        conv = gb_ref[...] * (p_before * cw[0:1, :] + p * cw[1:2, :] + p_after * cw[2:3, :])
        mix = jnp.concatenate([attn_ref[...], ssm.astype(_BF16), conv.astype(_BF16)], axis=1)
        yv = jnp.dot(mix, wout_ref[...], preferred_element_type=_F32)
        x = x_ref[...] + _gated_norm(yv, gpost1_ref[...], gt1_ref[...])
        return x, _modulated(x, gpre2_ref[...], sh2_ref[...], sc2_ref[...]).astype(_BF16)

    gz0, gz1 = gated_input(0), gated_input(1)
    (x0, h0), (x1, h1) = mixed_and_modulated(0, *gz0), mixed_and_modulated(1, *gz1)
    y0 = _swiglu(h0, wgu_ref, wd_ref)
    y1 = _swiglu(h1, wgu_ref, wd_ref)
    for h, (x, y) in enumerate(((x0, y0), (x1, y1))):
        gt2_ref = halves[h][N_ROW_INPUTS + 3]
        o_ref[h * TILE:(h + 1) * TILE, :] = x + _gated_norm(y, gpost2_ref[...], gt2_ref[...], MACARON)


def _post(x_all, mods, norm_pre, norm_post, attn, u_ssm, y_ssm, gb, p, ssm_d, w_glu, b_glu, conv_w,
          w_out, wgu, wd, layer, latent_only):
    tile = _latent_tile if latent_only else _stream_tile
    n_pairs = BATCH * (N_TILES - 1) // 2 if latent_only else BATCH * N_TILES // 2
    halo_blocks = TILE // SUBLANES
    in_specs, args = [], []
    for h in range(2):
        for arr in (x_all, attn, u_ssm, y_ssm, gb, p):
            if arr is y_ssm:
                in_specs.append(pl.BlockSpec((SSM_GROUPS, None, TILE_CHUNKS, SSM_WIDTH),
                                             lambda p_, h=h: (0, tile(p_, h), 0, 0)))
            else:
                in_specs.append(_row_spec(arr.shape[1], tile, h))
            args.append(arr)
        in_specs += [_mod_spec(layer, j, tile, h) for j in (5, 6, 7, 8)]
        args += [mods] * 4
    in_specs += [
        pl.BlockSpec((SUBLANES, CONV_WIDTH), lambda p_: (jnp.maximum(tile(p_, 0) * halo_blocks - 1, 0), 0)),
        pl.BlockSpec((SUBLANES, CONV_WIDTH),
                     lambda p_: (jnp.minimum((tile(p_, 1) + 1) * halo_blocks, BATCH * TOKENS // SUBLANES - 1), 0)),
        _norm_spec(layer, 1), _norm_spec(layer, 2), _norm_spec(layer, 2),
        _const_spec((None, 1, SSM_WIDTH), (layer, 0, 0)),
        _const_spec((None, SSM_WIDTH, SSM_WIDTH), (layer, 0, 0)),
        _const_spec((None, 1, SSM_WIDTH), (layer, 0, 0)),
        _const_spec((None, 3, CONV_WIDTH), (layer, 0, 0)),
        _const_spec((None, D_MODEL, D_MODEL), (layer, 0, 0)),
        *_ffn_weight_specs(layer, 1),
        _const_spec((TILE, TILE), (0, 0)),
    ]
    args += [p, p, norm_post, norm_pre, norm_post, ssm_d, w_glu, b_glu, conv_w, w_out, wgu, wd,
             _swap_rows_matrix()]
    return pl.pallas_call(
        functools.partial(_post_kernel, tile_fn=tile),
        grid=(n_pairs,),
        in_specs=in_specs,
        out_specs=pl.BlockSpec((2 * TILE, D_MODEL), lambda p_: (p_, 0)),
        out_shape=jax.ShapeDtypeStruct((n_pairs * 2 * TILE, D_MODEL), _F32),
        compiler_params=_TOKEN_PARAMS,
        name="mix_out_ffn",
    )(*args)


def _rope_tables():
    pos = jnp.arange(SEQ)
    row = (pos // GRID_W).astype(_F32)
    col = (pos % GRID_W).astype(_F32)
    inv_freq = ROPE_BASE ** (-jnp.arange(ROPE_PAIRS, dtype=_F32) / ROPE_PAIRS)
    ang = jnp.stack([row[:, None] * inv_freq, col[:, None] * inv_freq], axis=1)
    cos, sin = jnp.cos(ang), jnp.sin(ang)
    cos_h = jnp.concatenate([cos, cos], axis=-1).reshape(SEQ, HEAD_DIM)
    sin_h = jnp.concatenate([-sin, sin], axis=-1).reshape(SEQ, HEAD_DIM)
    cos_t = jnp.concatenate([jnp.ones((CTX_LEN, HEAD_DIM), _F32), cos_h], axis=0)
    sin_t = jnp.concatenate([jnp.zeros((CTX_LEN, HEAD_DIM), _F32), sin_h], axis=0)
    return jnp.tile(cos_t, (1, LANES // HEAD_DIM)), jnp.tile(sin_t, (1, LANES // HEAD_DIM))


def _s5_matrices(lam_re, lam_im, log_step, b_re, b_im, c_re, c_im):
    n, grp = SSM_CHUNK, SSM_GROUP
    k = jnp.arange(n + 1, dtype=_F32)
    dt = jnp.exp(log_step)[..., None, None]
    lr, li = lam_re[..., None], lam_im[..., None]
    mag = jnp.exp(k * lr * dt)
    pr, pi = mag * jnp.cos(k * li * dt), mag * jnp.sin(k * li * dt)
    ar, ai = pr[..., 1], pi[..., 1]
    den = lam_re * lam_re + lam_im * lam_im
    gr = ((ar - 1) * lam_re + ai * lam_im) / den
    gi = (ai * lam_re - (ar - 1) * lam_im) / den
    bbr = gr[..., None] * b_re - gi[..., None] * b_im
    bbi = gr[..., None] * b_im + gi[..., None] * b_re
    ct_re, ct_im = jnp.swapaxes(c_re, -1, -2), jnp.swapaxes(c_im, -1, -2)
    car = ct_re[..., None, :] * pr[..., None] - ct_im[..., None, :] * pi[..., None]
    cai = ct_re[..., None, :] * pi[..., None] + ct_im[..., None, :] * pr[..., None]
    kt = jnp.einsum('ldgpj,ldgpkh->ldgjkh', bbr, car) - jnp.einsum('ldgpj,ldgpkh->ldgjkh', bbi, cai)
    lanes = lambda m: m.reshape(m.shape[:-2] + (m.shape[-2] * m.shape[-1],))
    kf = lanes(kt[:, 0])
    kb = lanes(kt[:, 1, :, :, ::-1])
    pad = lambda m, lo, hi: jnp.pad(m, ((0, 0),) * (m.ndim - 1) + ((lo, hi),))
    intra = jnp.stack(
        [pad(kf[..., :(n - tau) * grp], tau * grp, 0)
         + pad(kb[..., (n - tau) * grp:], 0, (n - 1 - tau) * grp) for tau in range(n)],
        axis=2).reshape(DEPTH, SSM_GROUPS, n * grp, n * grp)

    def state_in(d, qr, qi):
        qr, qi = jnp.moveaxis(qr, -1, -2)[..., None, :], jnp.moveaxis(qi, -1, -2)[..., None, :]
        br, bi = jnp.swapaxes(bbr[:, d], -1, -2)[:, :, None], jnp.swapaxes(bbi[:, d], -1, -2)[:, :, None]
        rows = lambda m: m.reshape(DEPTH, SSM_GROUPS, n * grp, SSM_STATE)
        return rows(qr * br - qi * bi), rows(qr * bi + qi * br)

    f_re, f_im = state_in(0, pr[:, 0, ..., n - 1::-1], pi[:, 0, ..., n - 1::-1])
    b_re_, b_im_ = state_in(1, pr[:, 1, ..., :n], pi[:, 1, ..., :n])
    w1 = jnp.concatenate([intra, f_re, b_re_, f_im, b_im_], axis=-1)

    fo_re, fo_im = lanes(car[:, 0, ..., 1:, :]), -lanes(cai[:, 0, ..., 1:, :])
    bo_re, bo_im = lanes(car[:, 1, ..., :0:-1, :]), -lanes(cai[:, 1, ..., :0:-1, :])
    w2 = jnp.concatenate([fo_re, bo_re, fo_im, bo_im], axis=-2)
    a_pow = jnp.stack([jnp.concatenate([pr[:, 0, ..., n], pr[:, 1, ..., n]], axis=-1),
                       jnp.concatenate([pi[:, 0, ..., n], pi[:, 1, ..., n]], axis=-1)], axis=-2)
    return w1.astype(_BF16), w2.astype(_BF16), a_pow


def kernel(x, c, ctx, c_ctx, w_ada, b_ada, norm_pre, norm_post, ffn_w_gate, ffn_w_up, ffn_w_down,
           w_in, w_out, attn_sink, ssm_lambda_re, ssm_lambda_im, ssm_log_step, ssm_b_re, ssm_b_im,
           ssm_c_re, ssm_c_im, ssm_d, ssm_w_glu, ssm_b_glu, conv_w):
    cc = jnp.concatenate([c, c_ctx[None, :], jnp.zeros((MOD_ROWS - BATCH - 1, D_MODEL), _F32)], axis=0)
    mods = _ada_table(cc, w_ada, b_ada)
    cos_t, sin_t = _rope_tables()
    w1, w2, a_pow = _s5_matrices(ssm_lambda_re, ssm_lambda_im, ssm_log_step,
                                 ssm_b_re, ssm_b_im, ssm_c_re, ssm_c_im)
    npre = norm_pre.reshape(DEPTH, 3, 1, D_MODEL)
    npost = norm_post.reshape(DEPTH, 3, 1, D_MODEL)
    wgu = jnp.concatenate([ffn_w_gate, ffn_w_up], axis=-1).astype(_BF16)
    wd = ffn_w_down.astype(_BF16)
    w_in_b, w_out_b, w_glu_b = (w.astype(_BF16) for w in (w_in, w_out, ssm_w_glu))
    d_skip = ssm_d.reshape(DEPTH, 1, SSM_WIDTH)
    b_glu = ssm_b_glu.reshape(DEPTH, 1, SSM_WIDTH)

    xa = (ctx, x)
    seq = lambda t: t.reshape(BATCH, TOKENS, t.shape[-1])
    for l in range(DEPTH):
        xa, q, k, v, u_ssm, u_chunks, gb, p = _pre(xa, mods, npre, npost, wgu, wd, w_in_b, cos_t, sin_t, l)
        attn = _attention(attn_sink[l], q, seq(k), seq(v))
        y_ssm = _s5_mixer(u_chunks, w1[l], w2[l], a_pow[l])
        xa = _post(xa, mods, npre, npost, attn, u_ssm, y_ssm, gb, p, d_skip, w_glu_b,
                   b_glu, conv_w, w_out_b, wgu, wd, l, latent_only=l == DEPTH - 1)
    return xa.reshape(BATCH, SEQ, D_MODEL)
```

```python
import functools
import math

import jax
import jax.numpy as jnp
import numpy as np
from jax import lax
from jax.experimental import pallas as pl
from jax.experimental.pallas import tpu as pltpu

D_MODEL = 1024
BATCH = 16
SEQ = 4096
DEPTH = 4
GRID_W = 64
CTX_LEN = 256
HEAD_DIM = 64
ATTN_WIDTH = D_MODEL // 2
N_HEADS = ATTN_WIDTH // HEAD_DIM
KV_HEADS = N_HEADS // 4
Q_PER_KV = N_HEADS // KV_HEADS
KV_WIDTH = KV_HEADS * HEAD_DIM
WINDOW = 128
ATTN_SCALE = HEAD_DIM ** -0.5
LOG2_E = math.log2(math.e)
ROPE_BASE = 10000.0
ROPE_PAIRS = HEAD_DIM // 4
SSM_WIDTH = D_MODEL // 4
SSM_GROUP = 16
SSM_GROUPS = SSM_WIDTH // SSM_GROUP
SSM_STATE = 64
CONV_WIDTH = D_MODEL // 4
IN_COLS = ATTN_WIDTH + 2 * KV_WIDTH + SSM_WIDTH + 3 * CONV_WIDTH
D_FF = ((8 * D_MODEL // 3 + 127) // 128) * 128
MACARON = 0.5
N_MOD = 9
EPS = 1e-6
NEG_INF = -1e30

LANES = 128
SUBLANES = 8
V7X_VMEM_BYTES = 64 * 1024 * 1024

TOKENS = CTX_LEN + SEQ
TILE = 256
N_TILES = TOKENS // TILE
MOD_ROWS = -(-(BATCH + 1) // SUBLANES) * SUBLANES
CTX_ROW = BATCH
N_LOCAL = TILE + 2 * WINDOW
VMEM_LIMIT = 3 * V7X_VMEM_BYTES // 4

_F32 = jnp.float32
_BF16 = jnp.bfloat16


def _unit_rms(x):
    return x * lax.rsqrt(jnp.mean(x * x, axis=-1, keepdims=True) + EPS)


def _modulated(x, g, shift, scale):
    return _unit_rms(x) * (g * (1 + scale)) + shift


def _gated_norm(y, g, gate, factor=1.0):
    return _unit_rms(y) * (factor * (gate * g))


def _ada_kernel(cc_ref, w_ref, b_ref, o_ref):
    cc = cc_ref[...]
    s = cc * jax.nn.sigmoid(cc)
    o_ref[...] = jnp.dot(s, w_ref[...], precision=lax.Precision.HIGHEST,
                         preferred_element_type=_F32) + b_ref[...]


def _ada_table(cc, w_ada, b_ada):
    out = pl.pallas_call(
        _ada_kernel,
        grid=(DEPTH, N_MOD),
        in_specs=[
            pl.BlockSpec((MOD_ROWS, D_MODEL), lambda l, j: (0, 0)),
            pl.BlockSpec((None, D_MODEL, D_MODEL), lambda l, j: (l, 0, j)),
            pl.BlockSpec((None, None, 1, D_MODEL), lambda l, j: (l, j, 0, 0)),
        ],
        out_specs=pl.BlockSpec((None, None, MOD_ROWS, D_MODEL), lambda l, j: (l, j, 0, 0)),
        out_shape=jax.ShapeDtypeStruct((DEPTH, N_MOD, MOD_ROWS, D_MODEL), _F32),
        name="ada_table",
    )(cc, w_ada, b_ada.reshape(DEPTH, N_MOD, 1, D_MODEL))
    return out.reshape(DEPTH, N_MOD, MOD_ROWS, 1, D_MODEL)


def _stream_tile(p, h):
    return 2 * p + h


def _latent_tile(p, h):
    per_batch = (N_TILES - 1) // 2
    return (p // per_batch) * N_TILES + 1 + 2 * (p % per_batch) + h


def _row_spec(width, tile_fn, h):
    return pl.BlockSpec((TILE, width), lambda p: (tile_fn(p, h), 0))


def _mod_spec(layer, j, tile_fn, h):
    def index(p):
        t = tile_fn(p, h)
        return (layer, j, jnp.where(t % N_TILES == 0, CTX_ROW, t // N_TILES), 0, 0)
    return pl.BlockSpec((None, None, None, 1, D_MODEL), index)


def _const_spec(shape, index):
    return pl.BlockSpec(shape, lambda p: index, pipeline_mode=pl.Buffered(1))


def _norm_spec(layer, sub):
    return _const_spec((None, None, 1, D_MODEL), (layer, sub, 0, 0))


def _ffn_weight_specs(layer, which):
    return [
        _const_spec((None, None, D_MODEL, D_FF), (layer, which, 0, 0)),
        _const_spec((None, None, D_MODEL, D_FF), (layer, which, 0, 0)),
        _const_spec((None, None, D_FF, D_MODEL), (layer, which, 0, 0)),
    ]


_TOKEN_PARAMS = pltpu.CompilerParams(dimension_semantics=("parallel",), vmem_limit_bytes=VMEM_LIMIT)


SSM_CHUNK = SSM_WIDTH // SSM_GROUP
TILE_CHUNKS = TILE // SSM_CHUNK
ATOM = 16
assert SSM_CHUNK == SSM_GROUPS == TILE_CHUNKS == SSM_GROUP == BATCH == ATOM


def _swap_rows_matrix():
    r = np.arange(ATOM * ATOM)
    perm = np.zeros((ATOM * ATOM, ATOM * ATOM), np.float32)
    perm[r, (r % ATOM) * ATOM + r // ATOM] = 1.0
    return jnp.asarray(perm, _BF16)


def _atom_transpose(blocks):
    out = list(blocks)
    for i in range(ATOM // 2):
        (a_lo, a_hi), (b_lo, b_hi) = blocks[i], blocks[i + ATOM // 2]
        out[i], out[i + ATOM // 2] = (a_lo, b_lo), (a_hi, b_hi)
    blocks = out
    lane_atom = lax.broadcasted_iota(jnp.int32, (ATOM, LANES), 1) // ATOM
    for s in (4, 2, 1):
        keep = (lane_atom & s) == 0
        out = list(blocks)
        for i in range(ATOM):
            if i & s:
                continue
            a, b = blocks[i], blocks[i + s]
            out[i] = tuple(jnp.where(keep, x, pltpu.roll(y, s * ATOM, 1)) for x, y in zip(a, b))
            out[i + s] = tuple(jnp.where(keep, pltpu.roll(x, LANES - s * ATOM, 1), y) for x, y in zip(a, b))
        blocks = out
    return blocks


def _tile_to_chunk_layout(u_tile, swap_ref):
    z = jnp.dot(swap_ref[...], u_tile.astype(_BF16), preferred_element_type=_F32)
    blocks = [(z[i * ATOM:(i + 1) * ATOM, :LANES], z[i * ATOM:(i + 1) * ATOM, LANES:]) for i in range(ATOM)]
    return [jnp.concatenate(pair, axis=1) for pair in _atom_transpose(blocks)]


def _chunk_layout_to_tile(y_groups, swap_ref):
    blocks = [(y_groups[g, :, :LANES].astype(_F32), y_groups[g, :, LANES:].astype(_F32)) for g in range(ATOM)]
    z = jnp.concatenate([jnp.concatenate(pair, axis=1) for pair in _atom_transpose(blocks)], axis=0)
    return jnp.dot(swap_ref[...], z.astype(_BF16), preferred_element_type=_F32)


def _swiglu(h, wg_ref, wu_ref, wd_ref):
    a = jnp.dot(h, wg_ref[...], preferred_element_type=_F32)
    b = jnp.dot(h, wu_ref[...], preferred_element_type=_F32)
    s = ((a * jax.nn.sigmoid(a)) * b).astype(_BF16)
    return jnp.dot(s, wd_ref[...], preferred_element_type=_F32)


def _ffn_pair(xs, mods, gpre, gpost, wg_ref, wu_ref, wd_ref):
    hs = [_modulated(x, gpre, sh[...], sc[...]).astype(_BF16) for x, (sh, sc, _) in zip(xs, mods)]
    ys = [_swiglu(h, wg_ref, wu_ref, wd_ref) for h in hs]
    return [x + _gated_norm(y, gpost, gt[...], MACARON) for x, y, (_, _, gt) in zip(xs, ys, mods)]


def _pre_kernel(*refs, split_input):
    n_x = 2 if split_input else 1
    halves = []
    for h in range(2):
        group, refs = refs[:n_x + 7], refs[n_x + 7:]
        halves.append(group)
    (gpre0_ref, gpost0_ref, gpre1_ref, wg_ref, wu_ref, wd_ref, w_ref, swap_ref,
     xo_ref, q_ref, k_ref, v_ref, u_ref, uc_ref, gb_ref, p_ref) = refs
    xs = []
    for h, group in enumerate(halves):
        if split_input:
            is_ctx = _stream_tile(pl.program_id(0), h) % N_TILES == 0
            xs.append(jnp.where(is_ctx, group[0][...], group[1][...]))
        else:
            xs.append(group[0][...])
    mods0 = [group[n_x:n_x + 3] for group in halves]
    xs = _ffn_pair(xs, mods0, gpre0_ref[...], gpost0_ref[...], wg_ref, wu_ref, wd_ref)
    for h, x in enumerate(xs):
        xo_ref[h * TILE:(h + 1) * TILE, :] = x
    hs = [_modulated(x, gpre1_ref[...], group[n_x + 3][...], group[n_x + 4][...]).astype(_BF16)
          for x, group in zip(xs, halves)]
    prs = [jnp.dot(h, w_ref[...], preferred_element_type=_F32) for h in hs]
    lane = lax.broadcasted_iota(jnp.int32, (TILE, LANES), 1)
    first_half = (lane % (2 * ROPE_PAIRS)) < ROPE_PAIRS
    for h, (pr, group) in enumerate(zip(prs, halves)):
        rows = slice(h * TILE, (h + 1) * TILE)
        cosv = group[n_x + 5][...]
        sinv = group[n_x + 6][...]

        def rope(t, cosv=cosv, sinv=sinv):
            partner = jnp.where(first_half, pltpu.roll(t, LANES - ROPE_PAIRS, 1), pltpu.roll(t, ROPE_PAIRS, 1))
            return t * cosv + partner * sinv

        for j in range(0, ATTN_WIDTH, LANES):
            q_ref[rows, j:j + LANES] = (rope(pr[:, j:j + LANES]) * (ATTN_SCALE * LOG2_E)).astype(_BF16)
        c = ATTN_WIDTH
        k_ref[rows, :] = rope(pr[:, c:c + KV_WIDTH]).astype(_BF16)
        c += KV_WIDTH
        ones = jnp.ones((TILE, HEAD_DIM), _F32)
        v_ref[rows, :] = jnp.concatenate(
            [piece for kv in range(KV_HEADS)
             for piece in (pr[:, c + kv * HEAD_DIM:c + (kv + 1) * HEAD_DIM], ones)], axis=1).astype(_BF16)
        c += KV_WIDTH
        u_ref[rows, :] = pr[:, c:c + SSM_WIDTH]
        for g, u_group in enumerate(_tile_to_chunk_layout(pr[:, c:c + SSM_WIDTH], swap_ref)):
            uc_ref[g, h, :, :] = u_group.astype(_BF16)
        c += SSM_WIDTH
        gb_ref[rows, :] = pr[:, c:c + CONV_WIDTH]
        c += CONV_WIDTH
        p_ref[rows, :] = pr[:, c:c + CONV_WIDTH] * pr[:, c + CONV_WIDTH:c + 2 * CONV_WIDTH]


def _pre(x_in, mods, norm_pre, norm_post, wg, wu, wd, w_in, cos_t, sin_t, layer):
    split_input = isinstance(x_in, tuple)
    tile = _stream_tile
    in_specs, args = [], []
    for h in range(2):
        if split_input:
            in_specs += [
                pl.BlockSpec((None, CTX_LEN, D_MODEL), lambda p, h=h: (tile(p, h) // N_TILES, 0, 0)),
                pl.BlockSpec((None, TILE, D_MODEL),
                             lambda p, h=h: (tile(p, h) // N_TILES, jnp.maximum(tile(p, h) % N_TILES - 1, 0), 0)),
            ]
            args += list(x_in)
        else:
            in_specs.append(_row_spec(D_MODEL, tile, h))
            args.append(x_in)
        in_specs += [_mod_spec(layer, j, tile, h) for j in range(5)]
        args += [mods] * 5
        in_specs += [pl.BlockSpec((TILE, LANES), lambda p, h=h: (tile(p, h) % N_TILES, 0)) for _ in range(2)]
        args += [cos_t, sin_t]
    in_specs += [_norm_spec(layer, 0), _norm_spec(layer, 0), _norm_spec(layer, 1),
                 *_ffn_weight_specs(layer, 0), _const_spec((None, D_MODEL, IN_COLS), (layer, 0, 0)),
                 _const_spec((TILE, TILE), (0, 0))]
    args += [norm_pre, norm_post, norm_pre, wg, wu, wd, w_in, _swap_rows_matrix()]
    pair = lambda w: pl.BlockSpec((2 * TILE, w), lambda p: (p, 0))
    tok = lambda w, dt: jax.ShapeDtypeStruct((BATCH * TOKENS, w), dt)
    return pl.pallas_call(
        functools.partial(_pre_kernel, split_input=split_input),
        grid=(BATCH * N_TILES // 2,),
        in_specs=in_specs,
        out_specs=[pair(D_MODEL), pair(ATTN_WIDTH), pair(KV_WIDTH), pair(2 * KV_WIDTH), pair(SSM_WIDTH),
                   pl.BlockSpec((SSM_GROUPS, 2, TILE_CHUNKS, SSM_WIDTH), lambda p: (0, p, 0, 0)),
                   pair(CONV_WIDTH), pair(CONV_WIDTH)],
        out_shape=[
            tok(D_MODEL, _F32),
            tok(ATTN_WIDTH, _BF16), tok(KV_WIDTH, _BF16), tok(2 * KV_WIDTH, _BF16),
            tok(SSM_WIDTH, _F32),
            jax.ShapeDtypeStruct((SSM_GROUPS, BATCH * N_TILES, TILE_CHUNKS, SSM_WIDTH), _BF16),
            tok(CONV_WIDTH, _F32), tok(CONV_WIDTH, _F32),
        ],
        compiler_params=_TOKEN_PARAMS,
        name="ffn_in_proj",
    )(*args)


def _dot_nt(a, b):
    return lax.dot_general(a, b, (((1,), (1,)), ((), ())), preferred_element_type=_F32)


def _dot_tn(a, b):
    return lax.dot_general(a, b, (((0,), (0,)), ((), ())), preferred_element_type=_F32)


SCORES_AHEAD = 2


def _attn_kernel(sink_ref, q0_ref, k0_ref, v0_ref, q1_ref, k1_ref, v1_ref, o_ref):
    halves = []
    for half, (q_ref, k_ref, v_ref) in enumerate(((q0_ref, k0_ref, v0_ref), (q1_ref, k1_ref, v1_ref))):
        i = _stream_tile(pl.program_id(0), half) % N_TILES
        q0 = i * TILE
        start = pl.multiple_of(jnp.clip(q0 - WINDOW, CTX_LEN, TOKENS - N_LOCAL), WINDOW)
        kj = start + lax.broadcasted_iota(jnp.int32, (N_LOCAL, TILE), 0)
        qi = q0 + lax.broadcasted_iota(jnp.int32, (N_LOCAL, TILE), 1)
        valid = (jnp.abs(kj - qi) <= WINDOW) & (i >= 1)
        halves.append(dict(
            q=q_ref[...], bias=jnp.where(valid, 0.0, NEG_INF).astype(_F32),
            k_loc=k_ref[pl.ds(start, N_LOCAL), :], v_loc=v_ref[pl.ds(start, N_LOCAL), :],
            k_ctx=k_ref[0:CTX_LEN, :], v_ctx=v_ref[0:CTX_LEN, :]))

    def scores(unit):
        t, h = halves[unit // N_HEADS], unit % N_HEADS
        lo = (h // Q_PER_KV) * HEAD_DIM
        qh = t["q"][:, h * HEAD_DIM:(h + 1) * HEAD_DIM]
        return (_dot_nt(t["k_loc"][:, lo:lo + HEAD_DIM], qh) + t["bias"],
                _dot_nt(t["k_ctx"][:, lo:lo + HEAD_DIM], qh))

    outs = []

    def values(unit, p_loc, p_ctx, m, sink):
        half, h = unit // N_HEADS, unit % N_HEADS
        t = halves[half]
        vlo = 2 * (h // Q_PER_KV) * HEAD_DIM
        ov = (_dot_tn(t["v_loc"][:, vlo:vlo + 2 * HEAD_DIM], p_loc)
              + _dot_tn(t["v_ctx"][:, vlo:vlo + 2 * HEAD_DIM], p_ctx))
        den = ov[HEAD_DIM:HEAD_DIM + 1, :] + jnp.exp2(sink - m)
        outs.append(ov[0:HEAD_DIM, :] / den)
        if h % 2 == 1:
            pair = jnp.concatenate(outs, axis=0).T.astype(_BF16)
            o_ref[half * TILE:(half + 1) * TILE, (h - 1) * HEAD_DIM:(h + 1) * HEAD_DIM] = pair
            outs.clear()

    ahead = [scores(u) for u in range(SCORES_AHEAD)]
    pending = None
    for unit in range(2 * N_HEADS):
        s_loc, s_ctx = ahead.pop(0)
        if unit + SCORES_AHEAD < 2 * N_HEADS:
            ahead.append(scores(unit + SCORES_AHEAD))
        if pending is not None:
            values(*pending)
        sink = sink_ref[unit % N_HEADS] * LOG2_E
        m = jnp.maximum(jnp.maximum(jnp.max(s_loc, axis=0, keepdims=True),
                                    jnp.max(s_ctx, axis=0, keepdims=True)), sink)
        pending = (unit, jnp.exp2((s_loc - m).astype(_BF16)), jnp.exp2((s_ctx - m).astype(_BF16)), m, sink)
    values(*pending)


def _attention(sink, q, k, v):
    in_specs, args = [pl.BlockSpec(memory_space=pltpu.SMEM)], [sink]
    for h in range(2):
        in_specs.append(_row_spec(ATTN_WIDTH, _stream_tile, h))
        in_specs += [pl.BlockSpec((None, TOKENS, a.shape[-1]), lambda p, h=h: (_stream_tile(p, h) // N_TILES, 0, 0))
                     for a in (k, v)]
        args += [q, k, v]
    return pl.pallas_call(
        _attn_kernel,
        grid=(BATCH * N_TILES // 2,),
        in_specs=in_specs,
        out_specs=pl.BlockSpec((2 * TILE, ATTN_WIDTH), lambda p: (p, 0)),
        out_shape=jax.ShapeDtypeStruct((BATCH * TOKENS, ATTN_WIDTH), _BF16),
        compiler_params=_TOKEN_PARAMS,
        name="window_attn",
    )(*args)


N_CHUNKS = TOKENS // SSM_CHUNK
N_CTX_CHUNKS = CTX_LEN // SSM_CHUNK
CHUNK_ROWS = N_CHUNKS * BATCH
STATE_LANES = 2 * SSM_STATE


def _s5_kernel(u_ref, swap_ref, w1_ref, w2_ref, a_ref, y_ref, ucb_ref, xw_ref, s_ref):
    tile_rows = BATCH * TILE_CHUNKS
    for j in range(N_TILES):
        rows_bc = jnp.concatenate([u_ref[N_TILES * b + j] for b in range(BATCH)], axis=0)
        ucb_ref[j * tile_rows:(j + 1) * tile_rows, :] = jnp.dot(
            swap_ref[...], rows_bc, preferred_element_type=_F32).astype(_BF16)
    xw_ref[...] = jnp.dot(ucb_ref[...], w1_ref[...], preferred_element_type=_F32)
    a_r = jnp.broadcast_to(a_ref[0:1, :], (BATCH, STATE_LANES))
    a_i = jnp.broadcast_to(a_ref[1:2, :], (BATCH, STATE_LANES))
    fwd_lanes = lax.broadcasted_iota(jnp.int32, (BATCH, STATE_LANES), 1) < SSM_STATE
    x_re, x_im = SSM_WIDTH, SSM_WIDTH + STATE_LANES

    def body(i, carry):
        s_r, s_i = carry
        cb = jnp.where(i < N_CTX_CHUNKS, N_CTX_CHUNKS - 1 - i, N_CHUNKS + N_CTX_CHUNKS - 1 - i)
        rf = pl.ds(pl.multiple_of(i * BATCH, BATCH), BATCH)
        rb = pl.ds(pl.multiple_of(cb * BATCH, BATCH), BATCH)
        s_ref[rf, 0:SSM_STATE] = s_r[:, 0:SSM_STATE]
        s_ref[rb, SSM_STATE:STATE_LANES] = s_r[:, SSM_STATE:STATE_LANES]
        s_ref[rf, STATE_LANES:STATE_LANES + SSM_STATE] = s_i[:, 0:SSM_STATE]
        s_ref[rb, STATE_LANES + SSM_STATE:2 * STATE_LANES] = s_i[:, SSM_STATE:STATE_LANES]
        xr = jnp.where(fwd_lanes, xw_ref[rf, x_re:x_re + STATE_LANES], xw_ref[rb, x_re:x_re + STATE_LANES])
        xi = jnp.where(fwd_lanes, xw_ref[rf, x_im:x_im + STATE_LANES], xw_ref[rb, x_im:x_im + STATE_LANES])
        return a_r * s_r - a_i * s_i + xr, a_r * s_i + a_i * s_r + xi

    zero = jnp.zeros((BATCH, STATE_LANES), _F32)
    lax.fori_loop(0, N_CHUNKS, body, (zero, zero), unroll=8)
    half = CHUNK_ROWS // 2
    for r in (0, half):
        y = xw_ref[r:r + half, 0:SSM_WIDTH] + jnp.dot(
            s_ref[r:r + half, :].astype(_BF16), w2_ref[...], preferred_element_type=_F32)
        ucb_ref[r:r + half, :] = y.astype(_BF16)
    for j in range(N_TILES):
        rows_bc = jnp.dot(swap_ref[...], ucb_ref[j * tile_rows:(j + 1) * tile_rows, :],
                          preferred_element_type=_F32).astype(_BF16)
        for b in range(BATCH):
            y_ref[N_TILES * b + j] = rows_bc[b * TILE_CHUNKS:(b + 1) * TILE_CHUNKS, :]


def _s5_mixer(uc, w1, w2, a_pow):
    per_group = lambda a: pl.BlockSpec((None,) + a.shape[1:], lambda g: (g,) + (0,) * (a.ndim - 1))
    swap = _swap_rows_matrix()
    return pl.pallas_call(
        _s5_kernel,
        grid=(SSM_GROUPS,),
        in_specs=[per_group(uc), pl.BlockSpec(swap.shape, lambda g: (0, 0)),
                  per_group(w1), per_group(w2), per_group(a_pow)],
        out_specs=per_group(uc),
        out_shape=jax.ShapeDtypeStruct(uc.shape, _BF16),
        scratch_shapes=[
            pltpu.VMEM((CHUNK_ROWS, SSM_WIDTH), _BF16),
            pltpu.VMEM((CHUNK_ROWS, SSM_WIDTH + 2 * STATE_LANES), _F32),
            pltpu.VMEM((CHUNK_ROWS, 2 * STATE_LANES), _F32),
        ],
        compiler_params=pltpu.CompilerParams(
            dimension_semantics=("parallel",), vmem_limit_bytes=VMEM_LIMIT),
        name="s5_chunked",
    )(uc, swap, w1, w2, a_pow)


N_ROW_INPUTS = 6


def _post_kernel(*refs, tile_fn):
    halves = []
    for h in range(2):
        group, refs = refs[:N_ROW_INPUTS + 4], refs[N_ROW_INPUTS + 4:]
        halves.append(group)
    (pprev_ref, pnext_ref, gpost1_ref, gpre2_ref, gpost2_ref, d_ref, wglu_ref, bglu_ref, cw_ref,
     wout_ref, wg_ref, wu_ref, wd_ref, swap_ref, o_ref) = refs
    tiles = [tile_fn(pl.program_id(0), h) % N_TILES for h in range(2)]
    ps = [group[5][...] for group in halves]
    edge_before = [jnp.where(tiles[0] >= 2, pprev_ref[SUBLANES - 1:SUBLANES, :], 0.0),
                   jnp.where(tiles[1] >= 2, ps[0][TILE - 1:TILE, :], 0.0)]
    edge_after = [jnp.where((tiles[0] >= 1) & (tiles[0] <= N_TILES - 2), ps[1][0:1, :], 0.0),
                  jnp.where((tiles[1] >= 1) & (tiles[1] <= N_TILES - 2), pnext_ref[0:1, :], 0.0)]
    row = lax.broadcasted_iota(jnp.int32, (TILE, CONV_WIDTH), 0)
    cw = cw_ref[...]

    def gated_input(h):
        _, _, u_ref, ys_ref, _, _ = halves[h][:N_ROW_INPUTS]
        y = d_ref[...] * u_ref[...] + _chunk_layout_to_tile(ys_ref, swap_ref)
        g = jax.nn.gelu(y)
        return g, jnp.dot(g.astype(_BF16), wglu_ref[...], preferred_element_type=_F32)

    def mixed_and_modulated(h, g, z):
        group = halves[h]
        x_ref, attn_ref, _, _, gb_ref, _ = group[:N_ROW_INPUTS]
        gt1_ref, sh2_ref, sc2_ref, _ = group[N_ROW_INPUTS:]
        ssm = g * jax.nn.sigmoid(z + bglu_ref[...])
        p = ps[h]
        p_before = jnp.where(row == 0, edge_before[h], pltpu.roll(p, 1, 0))
        p_after = jnp.where(row == TILE - 1, edge_after[h], pltpu.roll(p, TILE - 1, 0))
        conv = gb_ref[...] * (p_before * cw[0:1, :] + p * cw[1:2, :] + p_after * cw[2:3, :])
        mix = jnp.concatenate([attn_ref[...], ssm.astype(_BF16), conv.astype(_BF16)], axis=1)
        yv = jnp.dot(mix, wout_ref[...], preferred_element_type=_F32)
        x = x_ref[...] + _gated_norm(yv, gpost1_ref[...], gt1_ref[...])
        return x, _modulated(x, gpre2_ref[...], sh2_ref[...], sc2_ref[...]).astype(_BF16)

    gz0, gz1 = gated_input(0), gated_input(1)
    (x0, h0), (x1, h1) = mixed_and_modulated(0, *gz0), mixed_and_modulated(1, *gz1)
    y0 = _swiglu(h0, wg_ref, wu_ref, wd_ref)
    y1 = _swiglu(h1, wg_ref, wu_ref, wd_ref)
    for h, (x, y) in enumerate(((x0, y0), (x1, y1))):
        gt2_ref = halves[h][N_ROW_INPUTS + 3]
        o_ref[h * TILE:(h + 1) * TILE, :] = x + _gated_norm(y, gpost2_ref[...], gt2_ref[...], MACARON)


def _post(x_all, mods, norm_pre, norm_post, attn, u_ssm, y_ssm, gb, p, ssm_d, w_glu, b_glu, conv_w,
          w_out, wg, wu, wd, layer, latent_only):
    tile = _latent_tile if latent_only else _stream_tile
    n_pairs = BATCH * (N_TILES - 1) // 2 if latent_only else BATCH * N_TILES // 2
    halo_blocks = TILE // SUBLANES
    in_specs, args = [], []
    for h in range(2):
        for arr in (x_all, attn, u_ssm, y_ssm, gb, p):
            if arr is y_ssm:
                in_specs.append(pl.BlockSpec((SSM_GROUPS, None, TILE_CHUNKS, SSM_WIDTH),
                                             lambda p_, h=h: (0, tile(p_, h), 0, 0)))
            else:
                in_specs.append(_row_spec(arr.shape[1], tile, h))
            args.append(arr)
        in_specs += [_mod_spec(layer, j, tile, h) for j in (5, 6, 7, 8)]
        args += [mods] * 4
    in_specs += [
        pl.BlockSpec((SUBLANES, CONV_WIDTH), lambda p_: (jnp.maximum(tile(p_, 0) * halo_blocks - 1, 0), 0)),
        pl.BlockSpec((SUBLANES, CONV_WIDTH),
                     lambda p_: (jnp.minimum((tile(p_, 1) + 1) * halo_blocks, BATCH * TOKENS // SUBLANES - 1), 0)),
        _norm_spec(layer, 1), _norm_spec(layer, 2), _norm_spec(layer, 2),
        _const_spec((None, 1, SSM_WIDTH), (layer, 0, 0)),
        _const_spec((None, SSM_WIDTH, SSM_WIDTH), (layer, 0, 0)),
        _const_spec((None, 1, SSM_WIDTH), (layer, 0, 0)),
        _const_spec((None, 3, CONV_WIDTH), (layer, 0, 0)),
        _const_spec((None, D_MODEL, D_MODEL), (layer, 0, 0)),
        *_ffn_weight_specs(layer, 1),
        _const_spec((TILE, TILE), (0, 0)),
    ]
    args += [p, p, norm_post, norm_pre, norm_post, ssm_d, w_glu, b_glu, conv_w, w_out, wg, wu, wd,
             _swap_rows_matrix()]
    return pl.pallas_call(
        functools.partial(_post_kernel, tile_fn=tile),
        grid=(n_pairs,),
        in_specs=in_specs,
        out_specs=pl.BlockSpec((2 * TILE, D_MODEL), lambda p_: (p_, 0)),
        out_shape=jax.ShapeDtypeStruct((n_pairs * 2 * TILE, D_MODEL), _F32),
        compiler_params=_TOKEN_PARAMS,
        name="mix_out_ffn",
    )(*args)


def _rope_tables():
    pos = jnp.arange(SEQ)
    row = (pos // GRID_W).astype(_F32)
    col = (pos % GRID_W).astype(_F32)
    inv_freq = ROPE_BASE ** (-jnp.arange(ROPE_PAIRS, dtype=_F32) / ROPE_PAIRS)
    ang = jnp.stack([row[:, None] * inv_freq, col[:, None] * inv_freq], axis=1)
    cos, sin = jnp.cos(ang), jnp.sin(ang)
    cos_h = jnp.concatenate([cos, cos], axis=-1).reshape(SEQ, HEAD_DIM)
    sin_h = jnp.concatenate([-sin, sin], axis=-1).reshape(SEQ, HEAD_DIM)
    cos_t = jnp.concatenate([jnp.ones((CTX_LEN, HEAD_DIM), _F32), cos_h], axis=0)
    sin_t = jnp.concatenate([jnp.zeros((CTX_LEN, HEAD_DIM), _F32), sin_h], axis=0)
    return jnp.tile(cos_t, (1, LANES // HEAD_DIM)), jnp.tile(sin_t, (1, LANES // HEAD_DIM))


def _s5_matrices(lam_re, lam_im, log_step, b_re, b_im, c_re, c_im):
    n, grp = SSM_CHUNK, SSM_GROUP
    k = jnp.arange(n + 1, dtype=_F32)
    dt = jnp.exp(log_step)[..., None, None]
    lr, li = lam_re[..., None], lam_im[..., None]
    mag = jnp.exp(k * lr * dt)
    pr, pi = mag * jnp.cos(k * li * dt), mag * jnp.sin(k * li * dt)
    ar, ai = pr[..., 1], pi[..., 1]
    den = lam_re * lam_re + lam_im * lam_im
    gr = ((ar - 1) * lam_re + ai * lam_im) / den
    gi = (ai * lam_re - (ar - 1) * lam_im) / den
    bbr = gr[..., None] * b_re - gi[..., None] * b_im
    bbi = gr[..., None] * b_im + gi[..., None] * b_re
    ct_re, ct_im = jnp.swapaxes(c_re, -1, -2), jnp.swapaxes(c_im, -1, -2)
    car = ct_re[..., None, :] * pr[..., None] - ct_im[..., None, :] * pi[..., None]
    cai = ct_re[..., None, :] * pi[..., None] + ct_im[..., None, :] * pr[..., None]
    kt = jnp.einsum('ldgpj,ldgpkh->ldgjkh', bbr, car) - jnp.einsum('ldgpj,ldgpkh->ldgjkh', bbi, cai)
    lanes = lambda m: m.reshape(m.shape[:-2] + (m.shape[-2] * m.shape[-1],))
    kf = lanes(kt[:, 0])
    kb = lanes(kt[:, 1, :, :, ::-1])
    pad = lambda m, lo, hi: jnp.pad(m, ((0, 0),) * (m.ndim - 1) + ((lo, hi),))
    intra = jnp.stack(
        [pad(kf[..., :(n - tau) * grp], tau * grp, 0)
         + pad(kb[..., (n - tau) * grp:], 0, (n - 1 - tau) * grp) for tau in range(n)],
        axis=2).reshape(DEPTH, SSM_GROUPS, n * grp, n * grp)

    def state_in(d, qr, qi):
        qr, qi = jnp.moveaxis(qr, -1, -2)[..., None, :], jnp.moveaxis(qi, -1, -2)[..., None, :]
        br, bi = jnp.swapaxes(bbr[:, d], -1, -2)[:, :, None], jnp.swapaxes(bbi[:, d], -1, -2)[:, :, None]
        rows = lambda m: m.reshape(DEPTH, SSM_GROUPS, n * grp, SSM_STATE)
        return rows(qr * br - qi * bi), rows(qr * bi + qi * br)

    f_re, f_im = state_in(0, pr[:, 0, ..., n - 1::-1], pi[:, 0, ..., n - 1::-1])
    b_re_, b_im_ = state_in(1, pr[:, 1, ..., :n], pi[:, 1, ..., :n])
    w1 = jnp.concatenate([intra, f_re, b_re_, f_im, b_im_], axis=-1)

    fo_re, fo_im = lanes(car[:, 0, ..., 1:, :]), -lanes(cai[:, 0, ..., 1:, :])
    bo_re, bo_im = lanes(car[:, 1, ..., :0:-1, :]), -lanes(cai[:, 1, ..., :0:-1, :])
    w2 = jnp.concatenate([fo_re, bo_re, fo_im, bo_im], axis=-2)
    a_pow = jnp.stack([jnp.concatenate([pr[:, 0, ..., n], pr[:, 1, ..., n]], axis=-1),
                       jnp.concatenate([pi[:, 0, ..., n], pi[:, 1, ..., n]], axis=-1)], axis=-2)
    return w1.astype(_BF16), w2.astype(_BF16), a_pow


def kernel(x, c, ctx, c_ctx, w_ada, b_ada, norm_pre, norm_post, ffn_w_gate, ffn_w_up, ffn_w_down,
           w_in, w_out, attn_sink, ssm_lambda_re, ssm_lambda_im, ssm_log_step, ssm_b_re, ssm_b_im,
           ssm_c_re, ssm_c_im, ssm_d, ssm_w_glu, ssm_b_glu, conv_w):
    cc = jnp.concatenate([c, c_ctx[None, :], jnp.zeros((MOD_ROWS - BATCH - 1, D_MODEL), _F32)], axis=0)
    mods = _ada_table(cc, w_ada, b_ada)
    cos_t, sin_t = _rope_tables()
    w1, w2, a_pow = _s5_matrices(ssm_lambda_re, ssm_lambda_im, ssm_log_step,
                                 ssm_b_re, ssm_b_im, ssm_c_re, ssm_c_im)
    npre = norm_pre.reshape(DEPTH, 3, 1, D_MODEL)
    npost = norm_post.reshape(DEPTH, 3, 1, D_MODEL)
    wg, wu, wd = (w.astype(_BF16) for w in (ffn_w_gate, ffn_w_up, ffn_w_down))
    w_in_b, w_out_b, w_glu_b = (w.astype(_BF16) for w in (w_in, w_out, ssm_w_glu))
    d_skip = ssm_d.reshape(DEPTH, 1, SSM_WIDTH)
    b_glu = ssm_b_glu.reshape(DEPTH, 1, SSM_WIDTH)

    xa = (ctx, x)
    seq = lambda t: t.reshape(BATCH, TOKENS, t.shape[-1])
    for l in range(DEPTH):
        xa, q, k, v, u_ssm, u_chunks, gb, p = _pre(xa, mods, npre, npost, wg, wu, wd, w_in_b, cos_t, sin_t, l)
        attn = _attention(attn_sink[l], q, seq(k), seq(v))
        y_ssm = _s5_mixer(u_chunks, w1[l], w2[l], a_pow[l])
        xa = _post(xa, mods, npre, npost, attn, u_ssm, y_ssm, gb, p, d_skip, w_glu_b,
                   b_glu, conv_w, w_out_b, wg, wu, wd, l, latent_only=l == DEPTH - 1)
    return xa.reshape(BATCH, SEQ, D_MODEL)
```

```python
import functools
import math

import jax
import jax.numpy as jnp
import numpy as np
from jax import lax
from jax.experimental import pallas as pl
from jax.experimental.pallas import tpu as pltpu

D_MODEL = 1024
BATCH = 16
SEQ = 4096
DEPTH = 4
GRID_W = 64
CTX_LEN = 256
HEAD_DIM = 64
ATTN_WIDTH = D_MODEL // 2
N_HEADS = ATTN_WIDTH // HEAD_DIM
KV_HEADS = N_HEADS // 4
Q_PER_KV = N_HEADS // KV_HEADS
KV_WIDTH = KV_HEADS * HEAD_DIM
WINDOW = 128
ATTN_SCALE = HEAD_DIM ** -0.5
LOG2_E = math.log2(math.e)
ROPE_BASE = 10000.0
ROPE_PAIRS = HEAD_DIM // 4
SSM_WIDTH = D_MODEL // 4
SSM_GROUP = 16
SSM_GROUPS = SSM_WIDTH // SSM_GROUP
SSM_STATE = 64
CONV_WIDTH = D_MODEL // 4
IN_COLS = ATTN_WIDTH + 2 * KV_WIDTH + SSM_WIDTH + 3 * CONV_WIDTH
D_FF = ((8 * D_MODEL // 3 + 127) // 128) * 128
MACARON = 0.5
N_MOD = 9
EPS = 1e-6
NEG_INF = -1e30

LANES = 128
SUBLANES = 8
V7X_VMEM_BYTES = 64 * 1024 * 1024

TOKENS = CTX_LEN + SEQ
TILE = 256
N_TILES = TOKENS // TILE
MOD_ROWS = -(-(BATCH + 1) // SUBLANES) * SUBLANES
CTX_ROW = BATCH
N_LOCAL = TILE + 2 * WINDOW
VMEM_LIMIT = 3 * V7X_VMEM_BYTES // 4

_F32 = jnp.float32
_BF16 = jnp.bfloat16


def _unit_rms(x):
    return x * lax.rsqrt(jnp.mean(x * x, axis=-1, keepdims=True) + EPS)


def _modulated(x, g, shift, scale):
    return _unit_rms(x) * (g * (1 + scale)) + shift


def _gated_norm(y, g, gate, factor=1.0):
    return _unit_rms(y) * (factor * (gate * g))


def _ada_kernel(cc_ref, w_ref, b_ref, o_ref):
    cc = cc_ref[...]
    s = cc * jax.nn.sigmoid(cc)
    o_ref[...] = jnp.dot(s, w_ref[...], precision=lax.Precision.HIGHEST,
                         preferred_element_type=_F32) + b_ref[...]


def _ada_table(cc, w_ada, b_ada):
    out = pl.pallas_call(
        _ada_kernel,
        grid=(DEPTH, N_MOD),
        in_specs=[
            pl.BlockSpec((MOD_ROWS, D_MODEL), lambda l, j: (0, 0)),
            pl.BlockSpec((None, D_MODEL, D_MODEL), lambda l, j: (l, 0, j)),
            pl.BlockSpec((None, None, 1, D_MODEL), lambda l, j: (l, j, 0, 0)),
        ],
        out_specs=pl.BlockSpec((None, None, MOD_ROWS, D_MODEL), lambda l, j: (l, j, 0, 0)),
        out_shape=jax.ShapeDtypeStruct((DEPTH, N_MOD, MOD_ROWS, D_MODEL), _F32),
        name="ada_table",
    )(cc, w_ada, b_ada.reshape(DEPTH, N_MOD, 1, D_MODEL))
    return out.reshape(DEPTH, N_MOD, MOD_ROWS, 1, D_MODEL)


def _stream_tile(p, h):
    return 2 * p + h


def _latent_tile(p, h):
    per_batch = (N_TILES - 1) // 2
    return (p // per_batch) * N_TILES + 1 + 2 * (p % per_batch) + h


def _row_spec(width, tile_fn, h):
    return pl.BlockSpec((TILE, width), lambda p: (tile_fn(p, h), 0))


def _mod_spec(layer, j, tile_fn, h):
    def index(p):
        t = tile_fn(p, h)
        return (layer, j, jnp.where(t % N_TILES == 0, CTX_ROW, t // N_TILES), 0, 0)
    return pl.BlockSpec((None, None, None, 1, D_MODEL), index)


def _const_spec(shape, index):
    return pl.BlockSpec(shape, lambda p: index, pipeline_mode=pl.Buffered(1))


def _norm_spec(layer, sub):
    return _const_spec((None, None, 1, D_MODEL), (layer, sub, 0, 0))


def _ffn_weight_specs(layer, which):
    return [
        _const_spec((None, None, D_MODEL, D_FF), (layer, which, 0, 0)),
        _const_spec((None, None, D_MODEL, D_FF), (layer, which, 0, 0)),
        _const_spec((None, None, D_FF, D_MODEL), (layer, which, 0, 0)),
    ]


_TOKEN_PARAMS = pltpu.CompilerParams(dimension_semantics=("parallel",), vmem_limit_bytes=VMEM_LIMIT)


SSM_CHUNK = SSM_WIDTH // SSM_GROUP
TILE_CHUNKS = TILE // SSM_CHUNK
ATOM = 16
assert SSM_CHUNK == SSM_GROUPS == TILE_CHUNKS == SSM_GROUP == BATCH == ATOM


def _swap_rows_matrix():
    r = np.arange(ATOM * ATOM)
    perm = np.zeros((ATOM * ATOM, ATOM * ATOM), np.float32)
    perm[r, (r % ATOM) * ATOM + r // ATOM] = 1.0
    return jnp.asarray(perm, _BF16)


def _atom_transpose(blocks):
    out = list(blocks)
    for i in range(ATOM // 2):
        (a_lo, a_hi), (b_lo, b_hi) = blocks[i], blocks[i + ATOM // 2]
        out[i], out[i + ATOM // 2] = (a_lo, b_lo), (a_hi, b_hi)
    blocks = out
    lane_atom = lax.broadcasted_iota(jnp.int32, (ATOM, LANES), 1) // ATOM
    for s in (4, 2, 1):
        keep = (lane_atom & s) == 0
        out = list(blocks)
        for i in range(ATOM):
            if i & s:
                continue
            a, b = blocks[i], blocks[i + s]
            out[i] = tuple(jnp.where(keep, x, pltpu.roll(y, s * ATOM, 1)) for x, y in zip(a, b))
            out[i + s] = tuple(jnp.where(keep, pltpu.roll(x, LANES - s * ATOM, 1), y) for x, y in zip(a, b))
        blocks = out
    return blocks


def _tile_to_chunk_layout(u_tile, swap_ref):
    z = jnp.dot(swap_ref[...], u_tile.astype(_BF16), preferred_element_type=_F32)
    blocks = [(z[i * ATOM:(i + 1) * ATOM, :LANES], z[i * ATOM:(i + 1) * ATOM, LANES:]) for i in range(ATOM)]
    return [jnp.concatenate(pair, axis=1) for pair in _atom_transpose(blocks)]


def _chunk_layout_to_tile(y_groups, swap_ref):
    blocks = [(y_groups[g, :, :LANES].astype(_F32), y_groups[g, :, LANES:].astype(_F32)) for g in range(ATOM)]
    z = jnp.concatenate([jnp.concatenate(pair, axis=1) for pair in _atom_transpose(blocks)], axis=0)
    return jnp.dot(swap_ref[...], z.astype(_BF16), preferred_element_type=_F32)


def _swiglu(h, wg_ref, wu_ref, wd_ref):
    a = jnp.dot(h, wg_ref[...], preferred_element_type=_F32)
    b = jnp.dot(h, wu_ref[...], preferred_element_type=_F32)
    s = ((a * jax.nn.sigmoid(a)) * b).astype(_BF16)
    return jnp.dot(s, wd_ref[...], preferred_element_type=_F32)


def _ffn_pair(xs, mods, gpre, gpost, wg_ref, wu_ref, wd_ref):
    hs = [_modulated(x, gpre, sh[...], sc[...]).astype(_BF16) for x, (sh, sc, _) in zip(xs, mods)]
    ys = [_swiglu(h, wg_ref, wu_ref, wd_ref) for h in hs]
    return [x + _gated_norm(y, gpost, gt[...], MACARON) for x, y, (_, _, gt) in zip(xs, ys, mods)]


def _pre_kernel(*refs, split_input):
    n_x = 2 if split_input else 1
    halves = []
    for h in range(2):
        group, refs = refs[:n_x + 7], refs[n_x + 7:]
        halves.append(group)
    (gpre0_ref, gpost0_ref, gpre1_ref, wg_ref, wu_ref, wd_ref, w_ref, swap_ref,
     xo_ref, q_ref, k_ref, v_ref, u_ref, uc_ref, gb_ref, p_ref) = refs
    xs = []
    for h, group in enumerate(halves):
        if split_input:
            is_ctx = _stream_tile(pl.program_id(0), h) % N_TILES == 0
            xs.append(jnp.where(is_ctx, group[0][...], group[1][...]))
        else:
            xs.append(group[0][...])
    mods0 = [group[n_x:n_x + 3] for group in halves]
    xs = _ffn_pair(xs, mods0, gpre0_ref[...], gpost0_ref[...], wg_ref, wu_ref, wd_ref)
    for h, x in enumerate(xs):
        xo_ref[h * TILE:(h + 1) * TILE, :] = x
    hs = [_modulated(x, gpre1_ref[...], group[n_x + 3][...], group[n_x + 4][...]).astype(_BF16)
          for x, group in zip(xs, halves)]
    prs = [jnp.dot(h, w_ref[...], preferred_element_type=_F32) for h in hs]
    lane = lax.broadcasted_iota(jnp.int32, (TILE, LANES), 1)
    first_half = (lane % (2 * ROPE_PAIRS)) < ROPE_PAIRS
    for h, (pr, group) in enumerate(zip(prs, halves)):
        rows = slice(h * TILE, (h + 1) * TILE)
        cosv = group[n_x + 5][...]
        sinv = group[n_x + 6][...]

        def rope(t, cosv=cosv, sinv=sinv):
            partner = jnp.where(first_half, pltpu.roll(t, LANES - ROPE_PAIRS, 1), pltpu.roll(t, ROPE_PAIRS, 1))
            return t * cosv + partner * sinv

        for j in range(0, ATTN_WIDTH, LANES):
            q_ref[rows, j:j + LANES] = (rope(pr[:, j:j + LANES]) * (ATTN_SCALE * LOG2_E)).astype(_BF16)
        c = ATTN_WIDTH
        k_ref[rows, :] = rope(pr[:, c:c + KV_WIDTH]).astype(_BF16)
        c += KV_WIDTH
        ones = jnp.ones((TILE, HEAD_DIM), _F32)
        v_ref[rows, :] = jnp.concatenate(
            [piece for kv in range(KV_HEADS)
             for piece in (pr[:, c + kv * HEAD_DIM:c + (kv + 1) * HEAD_DIM], ones)], axis=1).astype(_BF16)
        c += KV_WIDTH
        u_ref[rows, :] = pr[:, c:c + SSM_WIDTH]
        for g, u_group in enumerate(_tile_to_chunk_layout(pr[:, c:c + SSM_WIDTH], swap_ref)):
            uc_ref[g, h, :, :] = u_group.astype(_BF16)
        c += SSM_WIDTH
        gb_ref[rows, :] = pr[:, c:c + CONV_WIDTH]
        c += CONV_WIDTH
        p_ref[rows, :] = pr[:, c:c + CONV_WIDTH] * pr[:, c + CONV_WIDTH:c + 2 * CONV_WIDTH]


def _pre(x_in, mods, norm_pre, norm_post, wg, wu, wd, w_in, cos_t, sin_t, layer):
    split_input = isinstance(x_in, tuple)
    tile = _stream_tile
    in_specs, args = [], []
    for h in range(2):
        if split_input:
            in_specs += [
                pl.BlockSpec((None, CTX_LEN, D_MODEL), lambda p, h=h: (tile(p, h) // N_TILES, 0, 0)),
                pl.BlockSpec((None, TILE, D_MODEL),
                             lambda p, h=h: (tile(p, h) // N_TILES, jnp.maximum(tile(p, h) % N_TILES - 1, 0), 0)),
            ]
            args += list(x_in)
        else:
            in_specs.append(_row_spec(D_MODEL, tile, h))
            args.append(x_in)
        in_specs += [_mod_spec(layer, j, tile, h) for j in range(5)]
        args += [mods] * 5
        in_specs += [pl.BlockSpec((TILE, LANES), lambda p, h=h: (tile(p, h) % N_TILES, 0)) for _ in range(2)]
        args += [cos_t, sin_t]
    in_specs += [_norm_spec(layer, 0), _norm_spec(layer, 0), _norm_spec(layer, 1),
                 *_ffn_weight_specs(layer, 0), _const_spec((None, D_MODEL, IN_COLS), (layer, 0, 0)),
                 _const_spec((TILE, TILE), (0, 0))]
    args += [norm_pre, norm_post, norm_pre, wg, wu, wd, w_in, _swap_rows_matrix()]
    pair = lambda w: pl.BlockSpec((2 * TILE, w), lambda p: (p, 0))
    tok = lambda w, dt: jax.ShapeDtypeStruct((BATCH * TOKENS, w), dt)
    return pl.pallas_call(
        functools.partial(_pre_kernel, split_input=split_input),
        grid=(BATCH * N_TILES // 2,),
        in_specs=in_specs,
        out_specs=[pair(D_MODEL), pair(ATTN_WIDTH), pair(KV_WIDTH), pair(2 * KV_WIDTH), pair(SSM_WIDTH),
                   pl.BlockSpec((SSM_GROUPS, 2, TILE_CHUNKS, SSM_WIDTH), lambda p: (0, p, 0, 0)),
                   pair(CONV_WIDTH), pair(CONV_WIDTH)],
        out_shape=[
            tok(D_MODEL, _F32),
            tok(ATTN_WIDTH, _BF16), tok(KV_WIDTH, _BF16), tok(2 * KV_WIDTH, _BF16),
            tok(SSM_WIDTH, _F32),
            jax.ShapeDtypeStruct((SSM_GROUPS, BATCH * N_TILES, TILE_CHUNKS, SSM_WIDTH), _BF16),
            tok(CONV_WIDTH, _F32), tok(CONV_WIDTH, _F32),
        ],
        compiler_params=_TOKEN_PARAMS,
        name="ffn_in_proj",
    )(*args)


def _dot_nt(a, b):
    return lax.dot_general(a, b, (((1,), (1,)), ((), ())), preferred_element_type=_F32)


def _dot_tn(a, b):
    return lax.dot_general(a, b, (((0,), (0,)), ((), ())), preferred_element_type=_F32)


SCORES_AHEAD = 2


def _window_start(i):
    return jnp.clip(i * TILE - WINDOW, CTX_LEN, TOKENS - N_LOCAL)


def _attn_bias_table():
    i = jnp.arange(N_TILES)[:, None, None]
    kj = _window_start(i) + jnp.arange(N_LOCAL)[None, :, None]
    qi = i * TILE + jnp.arange(TILE)[None, None, :]
    valid = (jnp.abs(kj - qi) <= WINDOW) & (i >= 1)
    return jnp.where(valid, 0.0, NEG_INF).astype(_F32)


def _attn_kernel(sink_ref, q0_ref, k0_ref, v0_ref, b0_ref, q1_ref, k1_ref, v1_ref, b1_ref, o_ref):
    halves = []
    for half, (q_ref, k_ref, v_ref, b_ref) in enumerate(
            ((q0_ref, k0_ref, v0_ref, b0_ref), (q1_ref, k1_ref, v1_ref, b1_ref))):
        i = _stream_tile(pl.program_id(0), half) % N_TILES
        start = pl.multiple_of(_window_start(i), WINDOW)
        halves.append(dict(
            q=q_ref[...], bias=b_ref[...],
            k_loc=k_ref[pl.ds(start, N_LOCAL), :], v_loc=v_ref[pl.ds(start, N_LOCAL), :],
            k_ctx=k_ref[0:CTX_LEN, :], v_ctx=v_ref[0:CTX_LEN, :]))

    def scores(unit):
        t, h = halves[unit // N_HEADS], unit % N_HEADS
        lo = (h // Q_PER_KV) * HEAD_DIM
        qh = t["q"][:, h * HEAD_DIM:(h + 1) * HEAD_DIM]
        return (_dot_nt(t["k_loc"][:, lo:lo + HEAD_DIM], qh) + t["bias"],
                _dot_nt(t["k_ctx"][:, lo:lo + HEAD_DIM], qh))

    outs = []

    def values(unit, p_loc, p_ctx, m, sink):
        half, h = unit // N_HEADS, unit % N_HEADS
        t = halves[half]
        vlo = 2 * (h // Q_PER_KV) * HEAD_DIM
        ov = (_dot_tn(t["v_loc"][:, vlo:vlo + 2 * HEAD_DIM], p_loc)
              + _dot_tn(t["v_ctx"][:, vlo:vlo + 2 * HEAD_DIM], p_ctx))
        den = ov[HEAD_DIM:HEAD_DIM + 1, :] + jnp.exp2(sink - m)
        outs.append(ov[0:HEAD_DIM, :] / den)
        if h % 2 == 1:
            pair = jnp.concatenate(outs, axis=0).T.astype(_BF16)
            o_ref[half * TILE:(half + 1) * TILE, (h - 1) * HEAD_DIM:(h + 1) * HEAD_DIM] = pair
            outs.clear()

    ahead = [scores(u) for u in range(SCORES_AHEAD)]
    pending = None
    for unit in range(2 * N_HEADS):
        s_loc, s_ctx = ahead.pop(0)
        if unit + SCORES_AHEAD < 2 * N_HEADS:
            ahead.append(scores(unit + SCORES_AHEAD))
        if pending is not None:
            values(*pending)
        sink = sink_ref[unit % N_HEADS] * LOG2_E
        m = jnp.maximum(jnp.maximum(jnp.max(s_loc, axis=0, keepdims=True),
                                    jnp.max(s_ctx, axis=0, keepdims=True)), sink)
        pending = (unit, jnp.exp2((s_loc - m).astype(_BF16)), jnp.exp2((s_ctx - m).astype(_BF16)), m, sink)
    values(*pending)


def _attention(sink, q, k, v, bias):
    in_specs, args = [pl.BlockSpec(memory_space=pltpu.SMEM)], [sink]
    for h in range(2):
        in_specs.append(_row_spec(ATTN_WIDTH, _stream_tile, h))
        in_specs += [pl.BlockSpec((None, TOKENS, a.shape[-1]), lambda p, h=h: (_stream_tile(p, h) // N_TILES, 0, 0))
                     for a in (k, v)]
        in_specs.append(pl.BlockSpec((None, N_LOCAL, TILE), lambda p, h=h: (_stream_tile(p, h) % N_TILES, 0, 0)))
        args += [q, k, v, bias]
    return pl.pallas_call(
        _attn_kernel,
        grid=(BATCH * N_TILES // 2,),
        in_specs=in_specs,
        out_specs=pl.BlockSpec((2 * TILE, ATTN_WIDTH), lambda p: (p, 0)),
        out_shape=jax.ShapeDtypeStruct((BATCH * TOKENS, ATTN_WIDTH), _BF16),
        compiler_params=_TOKEN_PARAMS,
        name="window_attn",
    )(*args)


N_CHUNKS = TOKENS // SSM_CHUNK
N_CTX_CHUNKS = CTX_LEN // SSM_CHUNK
CHUNK_ROWS = N_CHUNKS * BATCH
STATE_LANES = 2 * SSM_STATE


def _s5_kernel(u_ref, swap_ref, w1_ref, w2_ref, a_ref, y_ref, ucb_ref, xw_ref, s_ref):
    tile_rows = BATCH * TILE_CHUNKS
    for j in range(N_TILES):
        rows_bc = jnp.concatenate([u_ref[N_TILES * b + j] for b in range(BATCH)], axis=0)
        ucb_ref[j * tile_rows:(j + 1) * tile_rows, :] = jnp.dot(
            swap_ref[...], rows_bc, preferred_element_type=_F32).astype(_BF16)
    xw_ref[...] = jnp.dot(ucb_ref[...], w1_ref[...], preferred_element_type=_F32)
    a_r = jnp.broadcast_to(a_ref[0:1, :], (BATCH, STATE_LANES))
    a_i = jnp.broadcast_to(a_ref[1:2, :], (BATCH, STATE_LANES))
    fwd_lanes = lax.broadcasted_iota(jnp.int32, (BATCH, STATE_LANES), 1) < SSM_STATE
    x_re, x_im = SSM_WIDTH, SSM_WIDTH + STATE_LANES

    def body(i, carry):
        s_r, s_i = carry
        cb = jnp.where(i < N_CTX_CHUNKS, N_CTX_CHUNKS - 1 - i, N_CHUNKS + N_CTX_CHUNKS - 1 - i)
        rf = pl.ds(pl.multiple_of(i * BATCH, BATCH), BATCH)
        rb = pl.ds(pl.multiple_of(cb * BATCH, BATCH), BATCH)
        s_ref[rf, 0:SSM_STATE] = s_r[:, 0:SSM_STATE]
        s_ref[rb, SSM_STATE:STATE_LANES] = s_r[:, SSM_STATE:STATE_LANES]
        s_ref[rf, STATE_LANES:STATE_LANES + SSM_STATE] = s_i[:, 0:SSM_STATE]
        s_ref[rb, STATE_LANES + SSM_STATE:2 * STATE_LANES] = s_i[:, SSM_STATE:STATE_LANES]
        xr = jnp.where(fwd_lanes, xw_ref[rf, x_re:x_re + STATE_LANES], xw_ref[rb, x_re:x_re + STATE_LANES])
        xi = jnp.where(fwd_lanes, xw_ref[rf, x_im:x_im + STATE_LANES], xw_ref[rb, x_im:x_im + STATE_LANES])
        return a_r * s_r - a_i * s_i + xr, a_r * s_i + a_i * s_r + xi

    zero = jnp.zeros((BATCH, STATE_LANES), _F32)
    lax.fori_loop(0, N_CHUNKS, body, (zero, zero), unroll=8)
    half = CHUNK_ROWS // 2
    for r in (0, half):
        y = xw_ref[r:r + half, 0:SSM_WIDTH] + jnp.dot(
            s_ref[r:r + half, :].astype(_BF16), w2_ref[...], preferred_element_type=_F32)
        ucb_ref[r:r + half, :] = y.astype(_BF16)
    for j in range(N_TILES):
        rows_bc = jnp.dot(swap_ref[...], ucb_ref[j * tile_rows:(j + 1) * tile_rows, :],
                          preferred_element_type=_F32).astype(_BF16)
        for b in range(BATCH):
            y_ref[N_TILES * b + j] = rows_bc[b * TILE_CHUNKS:(b + 1) * TILE_CHUNKS, :]


def _s5_mixer(uc, w1, w2, a_pow):
    per_group = lambda a: pl.BlockSpec((None,) + a.shape[1:], lambda g: (g,) + (0,) * (a.ndim - 1))
    swap = _swap_rows_matrix()
    return pl.pallas_call(
        _s5_kernel,
        grid=(SSM_GROUPS,),
        in_specs=[per_group(uc), pl.BlockSpec(swap.shape, lambda g: (0, 0)),
                  per_group(w1), per_group(w2), per_group(a_pow)],
        out_specs=per_group(uc),
        out_shape=jax.ShapeDtypeStruct(uc.shape, _BF16),
        scratch_shapes=[
            pltpu.VMEM((CHUNK_ROWS, SSM_WIDTH), _BF16),
            pltpu.VMEM((CHUNK_ROWS, SSM_WIDTH + 2 * STATE_LANES), _F32),
            pltpu.VMEM((CHUNK_ROWS, 2 * STATE_LANES), _F32),
        ],
        compiler_params=pltpu.CompilerParams(
            dimension_semantics=("parallel",), vmem_limit_bytes=VMEM_LIMIT),
        name="s5_chunked",
    )(uc, swap, w1, w2, a_pow)


N_ROW_INPUTS = 6


def _post_kernel(*refs, tile_fn):
    halves = []
    for h in range(2):
        group, refs = refs[:N_ROW_INPUTS + 4], refs[N_ROW_INPUTS + 4:]
        halves.append(group)
    (pprev_ref, pnext_ref, gpost1_ref, gpre2_ref, gpost2_ref, d_ref, wglu_ref, bglu_ref, cw_ref,
     wout_ref, wg_ref, wu_ref, wd_ref, swap_ref, o_ref) = refs
    tiles = [tile_fn(pl.program_id(0), h) % N_TILES for h in range(2)]
    ps = [group[5][...] for group in halves]
    edge_before = [jnp.where(tiles[0] >= 2, pprev_ref[SUBLANES - 1:SUBLANES, :], 0.0),
                   jnp.where(tiles[1] >= 2, ps[0][TILE - 1:TILE, :], 0.0)]
    edge_after = [jnp.where((tiles[0] >= 1) & (tiles[0] <= N_TILES - 2), ps[1][0:1, :], 0.0),
                  jnp.where((tiles[1] >= 1) & (tiles[1] <= N_TILES - 2), pnext_ref[0:1, :], 0.0)]
    row = lax.broadcasted_iota(jnp.int32, (TILE, CONV_WIDTH), 0)
    cw = cw_ref[...]

    def gated_input(h):
        _, _, u_ref, ys_ref, _, _ = halves[h][:N_ROW_INPUTS]
        y = d_ref[...] * u_ref[...] + _chunk_layout_to_tile(ys_ref, swap_ref)
        g = jax.nn.gelu(y)
        return g, jnp.dot(g.astype(_BF16), wglu_ref[...], preferred_element_type=_F32)

    def mixed_and_modulated(h, g, z):
        group = halves[h]
        x_ref, attn_ref, _, _, gb_ref, _ = group[:N_ROW_INPUTS]
        gt1_ref, sh2_ref, sc2_ref, _ = group[N_ROW_INPUTS:]
        ssm = g * jax.nn.sigmoid(z + bglu_ref[...])
        p = ps[h]
        p_before = jnp.where(row == 0, edge_before[h], pltpu.roll(p, 1, 0))
        p_after = jnp.where(row == TILE - 1, edge_after[h], pltpu.roll(p, TILE - 1, 0))
        conv = gb_ref[...] * (p_before * cw[0:1, :] + p * cw[1:2, :] + p_after * cw[2:3, :])
        mix = jnp.concatenate([attn_ref[...], ssm.astype(_BF16), conv.astype(_BF16)], axis=1)
        yv = jnp.dot(mix, wout_ref[...], preferred_element_type=_F32)
        x = x_ref[...] + _gated_norm(yv, gpost1_ref[...], gt1_ref[...])
        return x, _modulated(x, gpre2_ref[...], sh2_ref[...], sc2_ref[...]).astype(_BF16)

    gz0, gz1 = gated_input(0), gated_input(1)
    (x0, h0), (x1, h1) = mixed_and_modulated(0, *gz0), mixed_and_modulated(1, *gz1)
    y0 = _swiglu(h0, wg_ref, wu_ref, wd_ref)
    y1 = _swiglu(h1, wg_ref, wu_ref, wd_ref)
    for h, (x, y) in enumerate(((x0, y0), (x1, y1))):
        gt2_ref = halves[h][N_ROW_INPUTS + 3]
        o_ref[h * TILE:(h + 1) * TILE, :] = x + _gated_norm(y, gpost2_ref[...], gt2_ref[...], MACARON)


def _post(x_all, mods, norm_pre, norm_post, attn, u_ssm, y_ssm, gb, p, ssm_d, w_glu, b_glu, conv_w,
          w_out, wg, wu, wd, layer, latent_only):
    tile = _latent_tile if latent_only else _stream_tile
    n_pairs = BATCH * (N_TILES - 1) // 2 if latent_only else BATCH * N_TILES // 2
    halo_blocks = TILE // SUBLANES
    in_specs, args = [], []
    for h in range(2):
        for arr in (x_all, attn, u_ssm, y_ssm, gb, p):
            if arr is y_ssm:
                in_specs.append(pl.BlockSpec((SSM_GROUPS, None, TILE_CHUNKS, SSM_WIDTH),
                                             lambda p_, h=h: (0, tile(p_, h), 0, 0)))
            else:
                in_specs.append(_row_spec(arr.shape[1], tile, h))
            args.append(arr)
        in_specs += [_mod_spec(layer, j, tile, h) for j in (5, 6, 7, 8)]
        args += [mods] * 4
    in_specs += [
        pl.BlockSpec((SUBLANES, CONV_WIDTH), lambda p_: (jnp.maximum(tile(p_, 0) * halo_blocks - 1, 0), 0)),
        pl.BlockSpec((SUBLANES, CONV_WIDTH),
                     lambda p_: (jnp.minimum((tile(p_, 1) + 1) * halo_blocks, BATCH * TOKENS // SUBLANES - 1), 0)),
        _norm_spec(layer, 1), _norm_spec(layer, 2), _norm_spec(layer, 2),
        _const_spec((None, 1, SSM_WIDTH), (layer, 0, 0)),
        _const_spec((None, SSM_WIDTH, SSM_WIDTH), (layer, 0, 0)),
        _const_spec((None, 1, SSM_WIDTH), (layer, 0, 0)),
        _const_spec((None, 3, CONV_WIDTH), (layer, 0, 0)),
        _const_spec((None, D_MODEL, D_MODEL), (layer, 0, 0)),
        *_ffn_weight_specs(layer, 1),
        _const_spec((TILE, TILE), (0, 0)),
    ]
    args += [p, p, norm_post, norm_pre, norm_post, ssm_d, w_glu, b_glu, conv_w, w_out, wg, wu, wd,
             _swap_rows_matrix()]
    return pl.pallas_call(
        functools.partial(_post_kernel, tile_fn=tile),
        grid=(n_pairs,),
        in_specs=in_specs,
        out_specs=pl.BlockSpec((2 * TILE, D_MODEL), lambda p_: (p_, 0)),
        out_shape=jax.ShapeDtypeStruct((n_pairs * 2 * TILE, D_MODEL), _F32),
        compiler_params=_TOKEN_PARAMS,
        name="mix_out_ffn",
    )(*args)


def _rope_tables():
    pos = jnp.arange(SEQ)
    row = (pos // GRID_W).astype(_F32)
    col = (pos % GRID_W).astype(_F32)
    inv_freq = ROPE_BASE ** (-jnp.arange(ROPE_PAIRS, dtype=_F32) / ROPE_PAIRS)
    ang = jnp.stack([row[:, None] * inv_freq, col[:, None] * inv_freq], axis=1)
    cos, sin = jnp.cos(ang), jnp.sin(ang)
    cos_h = jnp.concatenate([cos, cos], axis=-1).reshape(SEQ, HEAD_DIM)
    sin_h = jnp.concatenate([-sin, sin], axis=-1).reshape(SEQ, HEAD_DIM)
    cos_t = jnp.concatenate([jnp.ones((CTX_LEN, HEAD_DIM), _F32), cos_h], axis=0)
    sin_t = jnp.concatenate([jnp.zeros((CTX_LEN, HEAD_DIM), _F32), sin_h], axis=0)
    return jnp.tile(cos_t, (1, LANES // HEAD_DIM)), jnp.tile(sin_t, (1, LANES // HEAD_DIM))


def _s5_matrices(lam_re, lam_im, log_step, b_re, b_im, c_re, c_im):
    n, grp = SSM_CHUNK, SSM_GROUP
    k = jnp.arange(n + 1, dtype=_F32)
    dt = jnp.exp(log_step)[..., None, None]
    lr, li = lam_re[..., None], lam_im[..., None]
    mag = jnp.exp(k * lr * dt)
    pr, pi = mag * jnp.cos(k * li * dt), mag * jnp.sin(k * li * dt)
    ar, ai = pr[..., 1], pi[..., 1]
    den = lam_re * lam_re + lam_im * lam_im
    gr = ((ar - 1) * lam_re + ai * lam_im) / den
    gi = (ai * lam_re - (ar - 1) * lam_im) / den
    bbr = gr[..., None] * b_re - gi[..., None] * b_im
    bbi = gr[..., None] * b_im + gi[..., None] * b_re
    ct_re, ct_im = jnp.swapaxes(c_re, -1, -2), jnp.swapaxes(c_im, -1, -2)
    car = ct_re[..., None, :] * pr[..., None] - ct_im[..., None, :] * pi[..., None]
    cai = ct_re[..., None, :] * pi[..., None] + ct_im[..., None, :] * pr[..., None]
    kt = jnp.einsum('ldgpj,ldgpkh->ldgjkh', bbr, car) - jnp.einsum('ldgpj,ldgpkh->ldgjkh', bbi, cai)
    lanes = lambda m: m.reshape(m.shape[:-2] + (m.shape[-2] * m.shape[-1],))
    kf = lanes(kt[:, 0])
    kb = lanes(kt[:, 1, :, :, ::-1])
    pad = lambda m, lo, hi: jnp.pad(m, ((0, 0),) * (m.ndim - 1) + ((lo, hi),))
    intra = jnp.stack(
        [pad(kf[..., :(n - tau) * grp], tau * grp, 0)
         + pad(kb[..., (n - tau) * grp:], 0, (n - 1 - tau) * grp) for tau in range(n)],
        axis=2).reshape(DEPTH, SSM_GROUPS, n * grp, n * grp)

    def state_in(d, qr, qi):
        qr, qi = jnp.moveaxis(qr, -1, -2)[..., None, :], jnp.moveaxis(qi, -1, -2)[..., None, :]
        br, bi = jnp.swapaxes(bbr[:, d], -1, -2)[:, :, None], jnp.swapaxes(bbi[:, d], -1, -2)[:, :, None]
        rows = lambda m: m.reshape(DEPTH, SSM_GROUPS, n * grp, SSM_STATE)
        return rows(qr * br - qi * bi), rows(qr * bi + qi * br)

    f_re, f_im = state_in(0, pr[:, 0, ..., n - 1::-1], pi[:, 0, ..., n - 1::-1])
    b_re_, b_im_ = state_in(1, pr[:, 1, ..., :n], pi[:, 1, ..., :n])
    w1 = jnp.concatenate([m.astype(_BF16) for m in (intra, f_re, b_re_, f_im, b_im_)], axis=-1)

    fo_re, fo_im = lanes(car[:, 0, ..., 1:, :]), -lanes(cai[:, 0, ..., 1:, :])
    bo_re, bo_im = lanes(car[:, 1, ..., :0:-1, :]), -lanes(cai[:, 1, ..., :0:-1, :])
    w2 = jnp.concatenate([m.astype(_BF16) for m in (fo_re, bo_re, fo_im, bo_im)], axis=-2)
    a_pow = jnp.stack([jnp.concatenate([pr[:, 0, ..., n], pr[:, 1, ..., n]], axis=-1),
                       jnp.concatenate([pi[:, 0, ..., n], pi[:, 1, ..., n]], axis=-1)], axis=-2)
    return w1, w2, a_pow


def kernel(x, c, ctx, c_ctx, w_ada, b_ada, norm_pre, norm_post, ffn_w_gate, ffn_w_up, ffn_w_down,
           w_in, w_out, attn_sink, ssm_lambda_re, ssm_lambda_im, ssm_log_step, ssm_b_re, ssm_b_im,
           ssm_c_re, ssm_c_im, ssm_d, ssm_w_glu, ssm_b_glu, conv_w):
    cc = jnp.concatenate([c, c_ctx[None, :], jnp.zeros((MOD_ROWS - BATCH - 1, D_MODEL), _F32)], axis=0)
    mods = _ada_table(cc, w_ada, b_ada)
    cos_t, sin_t = _rope_tables()
    attn_bias = _attn_bias_table()
    w1, w2, a_pow = _s5_matrices(ssm_lambda_re, ssm_lambda_im, ssm_log_step,
                                 ssm_b_re, ssm_b_im, ssm_c_re, ssm_c_im)
    npre = norm_pre.reshape(DEPTH, 3, 1, D_MODEL)
    npost = norm_post.reshape(DEPTH, 3, 1, D_MODEL)
    wg, wu, wd = (w.astype(_BF16) for w in (ffn_w_gate, ffn_w_up, ffn_w_down))
    w_in_b, w_out_b, w_glu_b = (w.astype(_BF16) for w in (w_in, w_out, ssm_w_glu))
    d_skip = ssm_d.reshape(DEPTH, 1, SSM_WIDTH)
    b_glu = ssm_b_glu.reshape(DEPTH, 1, SSM_WIDTH)

    xa = (ctx, x)
    seq = lambda t: t.reshape(BATCH, TOKENS, t.shape[-1])
    for l in range(DEPTH):
        xa, q, k, v, u_ssm, u_chunks, gb, p = _pre(xa, mods, npre, npost, wg, wu, wd, w_in_b, cos_t, sin_t, l)
        attn = _attention(attn_sink[l], q, seq(k), seq(v), attn_bias)
        y_ssm = _s5_mixer(u_chunks, w1[l], w2[l], a_pow[l])
        xa = _post(xa, mods, npre, npost, attn, u_ssm, y_ssm, gb, p, d_skip, w_glu_b,
                   b_glu, conv_w, w_out_b, wg, wu, wd, l, latent_only=l == DEPTH - 1)
    return xa.reshape(BATCH, SEQ, D_MODEL)
```

```python
import functools
import math

import jax
import jax.numpy as jnp
import numpy as np
from jax import lax
from jax.experimental import pallas as pl
from jax.experimental.pallas import tpu as pltpu

D_MODEL = 1024
BATCH = 16
SEQ = 4096
DEPTH = 4
GRID_W = 64
CTX_LEN = 256
HEAD_DIM = 64
ATTN_WIDTH = D_MODEL // 2
N_HEADS = ATTN_WIDTH // HEAD_DIM
KV_HEADS = N_HEADS // 4
Q_PER_KV = N_HEADS // KV_HEADS
KV_WIDTH = KV_HEADS * HEAD_DIM
WINDOW = 128
ATTN_SCALE = HEAD_DIM ** -0.5
LOG2_E = math.log2(math.e)
ROPE_BASE = 10000.0
ROPE_PAIRS = HEAD_DIM // 4
SSM_WIDTH = D_MODEL // 4
SSM_GROUP = 16
SSM_GROUPS = SSM_WIDTH // SSM_GROUP
SSM_STATE = 64
CONV_WIDTH = D_MODEL // 4
IN_COLS = ATTN_WIDTH + 2 * KV_WIDTH + SSM_WIDTH + 3 * CONV_WIDTH
D_FF = ((8 * D_MODEL // 3 + 127) // 128) * 128
MACARON = 0.5
N_MOD = 9
EPS = 1e-6
NEG_INF = -1e30

LANES = 128
SUBLANES = 8
V7X_VMEM_BYTES = 64 * 1024 * 1024

TOKENS = CTX_LEN + SEQ
TILE = 256
N_TILES = TOKENS // TILE
MOD_ROWS = -(-(BATCH + 1) // SUBLANES) * SUBLANES
CTX_ROW = BATCH
N_LOCAL = TILE + 2 * WINDOW
VMEM_LIMIT = 3 * V7X_VMEM_BYTES // 4

_F32 = jnp.float32
_BF16 = jnp.bfloat16


def _unit_rms(x):
    return x * lax.rsqrt(jnp.mean(x * x, axis=-1, keepdims=True) + EPS)


def _modulated(x, g, shift, scale):
    return _unit_rms(x) * (g * (1 + scale)) + shift


def _gated_norm(y, g, gate, factor=1.0):
    return _unit_rms(y) * (factor * (gate * g))


def _ada_kernel(cc_ref, w_ref, b_ref, o_ref):
    cc = cc_ref[...]
    s = cc * jax.nn.sigmoid(cc)
    o_ref[...] = jnp.dot(s, w_ref[...], precision=lax.Precision.HIGHEST,
                         preferred_element_type=_F32) + b_ref[...]


def _ada_table(cc, w_ada, b_ada):
    out = pl.pallas_call(
        _ada_kernel,
        grid=(DEPTH, N_MOD),
        in_specs=[
            pl.BlockSpec((MOD_ROWS, D_MODEL), lambda l, j: (0, 0)),
            pl.BlockSpec((None, D_MODEL, D_MODEL), lambda l, j: (l, 0, j)),
            pl.BlockSpec((None, None, 1, D_MODEL), lambda l, j: (l, j, 0, 0)),
        ],
        out_specs=pl.BlockSpec((None, None, MOD_ROWS, D_MODEL), lambda l, j: (l, j, 0, 0)),
        out_shape=jax.ShapeDtypeStruct((DEPTH, N_MOD, MOD_ROWS, D_MODEL), _F32),
        name="ada_table",
    )(cc, w_ada, b_ada.reshape(DEPTH, N_MOD, 1, D_MODEL))
    return out.transpose(0, 2, 1, 3)


def _stream_tile(p, h):
    return 2 * p + h


def _latent_tile(p, h):
    per_batch = (N_TILES - 1) // 2
    return (p // per_batch) * N_TILES + 1 + 2 * (p % per_batch) + h


def _row_spec(width, tile_fn, h):
    return pl.BlockSpec((TILE, width), lambda p: (tile_fn(p, h), 0))


def _mod_spec(layer, tile_fn, h):
    def index(p):
        t = tile_fn(p, h)
        return (layer, jnp.where(t % N_TILES == 0, CTX_ROW, t // N_TILES), 0, 0)
    return pl.BlockSpec((None, None, N_MOD, D_MODEL), index)


def _mod(mods_ref, j):
    return mods_ref[j:j + 1, :]


def _const_spec(shape, index):
    return pl.BlockSpec(shape, lambda p: index, pipeline_mode=pl.Buffered(1))


def _norm_spec(layer, sub):
    return _const_spec((None, None, 1, D_MODEL), (layer, sub, 0, 0))


def _ffn_weight_specs(layer, which):
    return [
        _const_spec((None, None, D_MODEL, D_FF), (layer, which, 0, 0)),
        _const_spec((None, None, D_MODEL, D_FF), (layer, which, 0, 0)),
        _const_spec((None, None, D_FF, D_MODEL), (layer, which, 0, 0)),
    ]


_TOKEN_PARAMS = pltpu.CompilerParams(dimension_semantics=("parallel",), vmem_limit_bytes=VMEM_LIMIT)


SSM_CHUNK = SSM_WIDTH // SSM_GROUP
TILE_CHUNKS = TILE // SSM_CHUNK
ATOM = 16
assert SSM_CHUNK == SSM_GROUPS == TILE_CHUNKS == SSM_GROUP == BATCH == ATOM


def _swap_rows_matrix():
    r = np.arange(ATOM * ATOM)
    perm = np.zeros((ATOM * ATOM, ATOM * ATOM), np.float32)
    perm[r, (r % ATOM) * ATOM + r // ATOM] = 1.0
    return jnp.asarray(perm, _BF16)


def _atom_transpose(blocks):
    out = list(blocks)
    for i in range(ATOM // 2):
        (a_lo, a_hi), (b_lo, b_hi) = blocks[i], blocks[i + ATOM // 2]
        out[i], out[i + ATOM // 2] = (a_lo, b_lo), (a_hi, b_hi)
    blocks = out
    lane_atom = lax.broadcasted_iota(jnp.int32, (ATOM, LANES), 1) // ATOM
    for s in (4, 2, 1):
        keep = (lane_atom & s) == 0
        out = list(blocks)
        for i in range(ATOM):
            if i & s:
                continue
            a, b = blocks[i], blocks[i + s]
            out[i] = tuple(jnp.where(keep, x, pltpu.roll(y, s * ATOM, 1)) for x, y in zip(a, b))
            out[i + s] = tuple(jnp.where(keep, pltpu.roll(x, LANES - s * ATOM, 1), y) for x, y in zip(a, b))
        blocks = out
    return blocks


def _tile_to_chunk_layout(u_tile, swap_ref):
    z = jnp.dot(swap_ref[...], u_tile.astype(_BF16), preferred_element_type=_F32)
    blocks = [(z[i * ATOM:(i + 1) * ATOM, :LANES], z[i * ATOM:(i + 1) * ATOM, LANES:]) for i in range(ATOM)]
    return [jnp.concatenate(pair, axis=1) for pair in _atom_transpose(blocks)]


def _chunk_layout_to_tile(y_groups, swap_ref):
    blocks = [(y_groups[g, :, :LANES].astype(_F32), y_groups[g, :, LANES:].astype(_F32)) for g in range(ATOM)]
    z = jnp.concatenate([jnp.concatenate(pair, axis=1) for pair in _atom_transpose(blocks)], axis=0)
    return jnp.dot(swap_ref[...], z.astype(_BF16), preferred_element_type=_F32)


def _swiglu(h, wg_ref, wu_ref, wd_ref):
    a = jnp.dot(h, wg_ref[...], preferred_element_type=_F32)
    b = jnp.dot(h, wu_ref[...], preferred_element_type=_F32)
    s = ((a * jax.nn.sigmoid(a)) * b).astype(_BF16)
    return jnp.dot(s, wd_ref[...], preferred_element_type=_F32)


def _ffn_pair(xs, mods, gpre, gpost, wg_ref, wu_ref, wd_ref):
    hs = [_modulated(x, gpre, sh, sc).astype(_BF16) for x, (sh, sc, _) in zip(xs, mods)]
    ys = [_swiglu(h, wg_ref, wu_ref, wd_ref) for h in hs]
    return [x + _gated_norm(y, gpost, gt, MACARON) for x, y, (_, _, gt) in zip(xs, ys, mods)]


def _pre_kernel(*refs, split_input):
    n_x = 2 if split_input else 1
    halves = []
    for h in range(2):
        group, refs = refs[:n_x + 3], refs[n_x + 3:]
        halves.append(group)
    (gpre0_ref, gpost0_ref, gpre1_ref, wg_ref, wu_ref, wd_ref, w_ref, swap_ref,
     xo_ref, q_ref, k_ref, v_ref, u_ref, uc_ref, gb_ref, p_ref) = refs
    xs = []
    for h, group in enumerate(halves):
        if split_input:
            is_ctx = _stream_tile(pl.program_id(0), h) % N_TILES == 0
            xs.append(jnp.where(is_ctx, group[0][...], group[1][...]))
        else:
            xs.append(group[0][...])
    mods0 = [[_mod(group[n_x], j) for j in (0, 1, 2)] for group in halves]
    xs = _ffn_pair(xs, mods0, gpre0_ref[...], gpost0_ref[...], wg_ref, wu_ref, wd_ref)
    for h, x in enumerate(xs):
        xo_ref[h * TILE:(h + 1) * TILE, :] = x
    hs = [_modulated(x, gpre1_ref[...], _mod(group[n_x], 3), _mod(group[n_x], 4)).astype(_BF16)
          for x, group in zip(xs, halves)]
    prs = [jnp.dot(h, w_ref[...], preferred_element_type=_F32) for h in hs]
    lane = lax.broadcasted_iota(jnp.int32, (TILE, LANES), 1)
    first_half = (lane % (2 * ROPE_PAIRS)) < ROPE_PAIRS
    for h, (pr, group) in enumerate(zip(prs, halves)):
        rows = slice(h * TILE, (h + 1) * TILE)
        cosv = group[n_x + 1][...]
        sinv = group[n_x + 2][...]

        def rope(t, cosv=cosv, sinv=sinv):
            partner = jnp.where(first_half, pltpu.roll(t, LANES - ROPE_PAIRS, 1), pltpu.roll(t, ROPE_PAIRS, 1))
            return t * cosv + partner * sinv

        for j in range(0, ATTN_WIDTH, LANES):
            q_ref[rows, j:j + LANES] = (rope(pr[:, j:j + LANES]) * (ATTN_SCALE * LOG2_E)).astype(_BF16)
        c = ATTN_WIDTH
        k_ref[rows, :] = rope(pr[:, c:c + KV_WIDTH]).astype(_BF16)
        c += KV_WIDTH
        ones = jnp.ones((TILE, HEAD_DIM), _F32)
        v_ref[rows, :] = jnp.concatenate(
            [piece for kv in range(KV_HEADS)
             for piece in (pr[:, c + kv * HEAD_DIM:c + (kv + 1) * HEAD_DIM], ones)], axis=1).astype(_BF16)
        c += KV_WIDTH
        u_ref[rows, :] = pr[:, c:c + SSM_WIDTH]
        for g, u_group in enumerate(_tile_to_chunk_layout(pr[:, c:c + SSM_WIDTH], swap_ref)):
            uc_ref[g, h, :, :] = u_group.astype(_BF16)
        c += SSM_WIDTH
        gb_ref[rows, :] = pr[:, c:c + CONV_WIDTH]
        c += CONV_WIDTH
        p_ref[rows, :] = pr[:, c:c + CONV_WIDTH] * pr[:, c + CONV_WIDTH:c + 2 * CONV_WIDTH]


def _pre(x_in, mods, norm_pre, norm_post, wg, wu, wd, w_in, cos_t, sin_t, layer):
    split_input = isinstance(x_in, tuple)
    tile = _stream_tile
    in_specs, args = [], []
    for h in range(2):
        if split_input:
            in_specs += [
                pl.BlockSpec((None, CTX_LEN, D_MODEL), lambda p, h=h: (tile(p, h) // N_TILES, 0, 0)),
                pl.BlockSpec((None, TILE, D_MODEL),
                             lambda p, h=h: (tile(p, h) // N_TILES, jnp.maximum(tile(p, h) % N_TILES - 1, 0), 0)),
            ]
            args += list(x_in)
        else:
            in_specs.append(_row_spec(D_MODEL, tile, h))
            args.append(x_in)
        in_specs.append(_mod_spec(layer, tile, h))
        args.append(mods)
        in_specs += [pl.BlockSpec((TILE, LANES), lambda p, h=h: (tile(p, h) % N_TILES, 0)) for _ in range(2)]
        args += [cos_t, sin_t]
    in_specs += [_norm_spec(layer, 0), _norm_spec(layer, 0), _norm_spec(layer, 1),
                 *_ffn_weight_specs(layer, 0), _const_spec((None, D_MODEL, IN_COLS), (layer, 0, 0)),
                 _const_spec((TILE, TILE), (0, 0))]
    args += [norm_pre, norm_post, norm_pre, wg, wu, wd, w_in, _swap_rows_matrix()]
    pair = lambda w: pl.BlockSpec((2 * TILE, w), lambda p: (p, 0))
    tok = lambda w, dt: jax.ShapeDtypeStruct((BATCH * TOKENS, w), dt)
    return pl.pallas_call(
        functools.partial(_pre_kernel, split_input=split_input),
        grid=(BATCH * N_TILES // 2,),
        in_specs=in_specs,
        out_specs=[pair(D_MODEL), pair(ATTN_WIDTH), pair(KV_WIDTH), pair(2 * KV_WIDTH), pair(SSM_WIDTH),
                   pl.BlockSpec((SSM_GROUPS, 2, TILE_CHUNKS, SSM_WIDTH), lambda p: (0, p, 0, 0)),
                   pair(CONV_WIDTH), pair(CONV_WIDTH)],
        out_shape=[
            tok(D_MODEL, _F32),
            tok(ATTN_WIDTH, _BF16), tok(KV_WIDTH, _BF16), tok(2 * KV_WIDTH, _BF16),
            tok(SSM_WIDTH, _F32),
            jax.ShapeDtypeStruct((SSM_GROUPS, BATCH * N_TILES, TILE_CHUNKS, SSM_WIDTH), _BF16),
            tok(CONV_WIDTH, _F32), tok(CONV_WIDTH, _F32),
        ],
        compiler_params=_TOKEN_PARAMS,
        name="ffn_in_proj",
    )(*args)


def _dot_nt(a, b):
    return lax.dot_general(a, b, (((1,), (1,)), ((), ())), preferred_element_type=_F32)


def _dot_tn(a, b):
    return lax.dot_general(a, b, (((0,), (0,)), ((), ())), preferred_element_type=_F32)


SCORES_AHEAD = 2


def _window_start(i):
    return jnp.clip(i * TILE - WINDOW, CTX_LEN, TOKENS - N_LOCAL)


def _attn_bias_table():
    i = jnp.arange(N_TILES)[:, None, None]
    kj = _window_start(i) + jnp.arange(N_LOCAL)[None, :, None]
    qi = i * TILE + jnp.arange(TILE)[None, None, :]
    valid = (jnp.abs(kj - qi) <= WINDOW) & (i >= 1)
    return jnp.where(valid, 0.0, NEG_INF).astype(_F32)


def _attn_kernel(sink_ref, q0_ref, k0_ref, v0_ref, b0_ref, q1_ref, k1_ref, v1_ref, b1_ref, o_ref):
    halves = []
    for half, (q_ref, k_ref, v_ref, b_ref) in enumerate(
            ((q0_ref, k0_ref, v0_ref, b0_ref), (q1_ref, k1_ref, v1_ref, b1_ref))):
        i = _stream_tile(pl.program_id(0), half) % N_TILES
        start = pl.multiple_of(_window_start(i), WINDOW)
        halves.append(dict(
            q=q_ref[...], bias=b_ref[...],
            k_loc=k_ref[pl.ds(start, N_LOCAL), :], v_loc=v_ref[pl.ds(start, N_LOCAL), :],
            k_ctx=k_ref[0:CTX_LEN, :], v_ctx=v_ref[0:CTX_LEN, :]))

    def scores(unit):
        t, h = halves[unit // N_HEADS], unit % N_HEADS
        lo = (h // Q_PER_KV) * HEAD_DIM
        qh = t["q"][:, h * HEAD_DIM:(h + 1) * HEAD_DIM]
        return (_dot_nt(t["k_loc"][:, lo:lo + HEAD_DIM], qh) + t["bias"],
                _dot_nt(t["k_ctx"][:, lo:lo + HEAD_DIM], qh))

    outs = []

    def values(unit, p_loc, p_ctx, m, sink):
        half, h = unit // N_HEADS, unit % N_HEADS
        t = halves[half]
        vlo = 2 * (h // Q_PER_KV) * HEAD_DIM
        ov = (_dot_tn(t["v_loc"][:, vlo:vlo + 2 * HEAD_DIM], p_loc)
              + _dot_tn(t["v_ctx"][:, vlo:vlo + 2 * HEAD_DIM], p_ctx))
        den = ov[HEAD_DIM:HEAD_DIM + 1, :] + jnp.exp2(sink - m)
        outs.append(ov[0:HEAD_DIM, :] / den)
        if h % 2 == 1:
            pair = jnp.concatenate(outs, axis=0).T.astype(_BF16)
            o_ref[half * TILE:(half + 1) * TILE, (h - 1) * HEAD_DIM:(h + 1) * HEAD_DIM] = pair
            outs.clear()

    ahead = [scores(u) for u in range(SCORES_AHEAD)]
    pending = None
    for unit in range(2 * N_HEADS):
        s_loc, s_ctx = ahead.pop(0)
        if unit + SCORES_AHEAD < 2 * N_HEADS:
            ahead.append(scores(unit + SCORES_AHEAD))
        if pending is not None:
            values(*pending)
        sink = sink_ref[unit % N_HEADS] * LOG2_E
        m = jnp.maximum(jnp.maximum(jnp.max(s_loc, axis=0, keepdims=True),
                                    jnp.max(s_ctx, axis=0, keepdims=True)), sink)
        pending = (unit, jnp.exp2((s_loc - m).astype(_BF16)), jnp.exp2((s_ctx - m).astype(_BF16)), m, sink)
    values(*pending)


def _attention(sink, q, k, v, bias):
    in_specs, args = [pl.BlockSpec(memory_space=pltpu.SMEM)], [sink]
    for h in range(2):
        in_specs.append(_row_spec(ATTN_WIDTH, _stream_tile, h))
        in_specs += [pl.BlockSpec((None, TOKENS, a.shape[-1]), lambda p, h=h: (_stream_tile(p, h) // N_TILES, 0, 0))
                     for a in (k, v)]
        in_specs.append(pl.BlockSpec((None, N_LOCAL, TILE), lambda p, h=h: (_stream_tile(p, h) % N_TILES, 0, 0)))
        args += [q, k, v, bias]
    return pl.pallas_call(
        _attn_kernel,
        grid=(BATCH * N_TILES // 2,),
        in_specs=in_specs,
        out_specs=pl.BlockSpec((2 * TILE, ATTN_WIDTH), lambda p: (p, 0)),
        out_shape=jax.ShapeDtypeStruct((BATCH * TOKENS, ATTN_WIDTH), _BF16),
        compiler_params=_TOKEN_PARAMS,
        name="window_attn",
    )(*args)


N_CHUNKS = TOKENS // SSM_CHUNK
N_CTX_CHUNKS = CTX_LEN // SSM_CHUNK
CHUNK_ROWS = N_CHUNKS * BATCH
STATE_LANES = 2 * SSM_STATE


def _s5_kernel(u_ref, swap_ref, w1_ref, w2_ref, a_ref, y_ref, ucb_ref, xw_ref, s_ref):
    tile_rows = BATCH * TILE_CHUNKS
    for j in range(N_TILES):
        rows_bc = jnp.concatenate([u_ref[N_TILES * b + j] for b in range(BATCH)], axis=0)
        ucb_ref[j * tile_rows:(j + 1) * tile_rows, :] = jnp.dot(
            swap_ref[...], rows_bc, preferred_element_type=_F32).astype(_BF16)
    xw_ref[...] = jnp.dot(ucb_ref[...], w1_ref[...], preferred_element_type=_F32)
    a_r = jnp.broadcast_to(a_ref[0:1, :], (BATCH, STATE_LANES))
    a_i = jnp.broadcast_to(a_ref[1:2, :], (BATCH, STATE_LANES))
    fwd_lanes = lax.broadcasted_iota(jnp.int32, (BATCH, STATE_LANES), 1) < SSM_STATE
    x_re, x_im = SSM_WIDTH, SSM_WIDTH + STATE_LANES

    def body(i, carry):
        s_r, s_i = carry
        cb = jnp.where(i < N_CTX_CHUNKS, N_CTX_CHUNKS - 1 - i, N_CHUNKS + N_CTX_CHUNKS - 1 - i)
        rf = pl.ds(pl.multiple_of(i * BATCH, BATCH), BATCH)
        rb = pl.ds(pl.multiple_of(cb * BATCH, BATCH), BATCH)
        s_ref[rf, 0:SSM_STATE] = s_r[:, 0:SSM_STATE]
        s_ref[rb, SSM_STATE:STATE_LANES] = s_r[:, SSM_STATE:STATE_LANES]
        s_ref[rf, STATE_LANES:STATE_LANES + SSM_STATE] = s_i[:, 0:SSM_STATE]
        s_ref[rb, STATE_LANES + SSM_STATE:2 * STATE_LANES] = s_i[:, SSM_STATE:STATE_LANES]
        xr = jnp.where(fwd_lanes, xw_ref[rf, x_re:x_re + STATE_LANES], xw_ref[rb, x_re:x_re + STATE_LANES])
        xi = jnp.where(fwd_lanes, xw_ref[rf, x_im:x_im + STATE_LANES], xw_ref[rb, x_im:x_im + STATE_LANES])
        return a_r * s_r - a_i * s_i + xr, a_r * s_i + a_i * s_r + xi

    zero = jnp.zeros((BATCH, STATE_LANES), _F32)
    lax.fori_loop(0, N_CHUNKS, body, (zero, zero), unroll=8)
    half = CHUNK_ROWS // 2
    for r in (0, half):
        y = xw_ref[r:r + half, 0:SSM_WIDTH] + jnp.dot(
            s_ref[r:r + half, :].astype(_BF16), w2_ref[...], preferred_element_type=_F32)
        ucb_ref[r:r + half, :] = y.astype(_BF16)
    for j in range(N_TILES):
        rows_bc = jnp.dot(swap_ref[...], ucb_ref[j * tile_rows:(j + 1) * tile_rows, :],
                          preferred_element_type=_F32).astype(_BF16)
        for b in range(BATCH):
            y_ref[N_TILES * b + j] = rows_bc[b * TILE_CHUNKS:(b + 1) * TILE_CHUNKS, :]


def _s5_mixer(uc, w1, w2, a_pow):
    per_group = lambda a: pl.BlockSpec((None,) + a.shape[1:], lambda g: (g,) + (0,) * (a.ndim - 1))
    swap = _swap_rows_matrix()
    return pl.pallas_call(
        _s5_kernel,
        grid=(SSM_GROUPS,),
        in_specs=[per_group(uc), pl.BlockSpec(swap.shape, lambda g: (0, 0)),
                  per_group(w1), per_group(w2), per_group(a_pow)],
        out_specs=per_group(uc),
        out_shape=jax.ShapeDtypeStruct(uc.shape, _BF16),
        scratch_shapes=[
            pltpu.VMEM((CHUNK_ROWS, SSM_WIDTH), _BF16),
            pltpu.VMEM((CHUNK_ROWS, SSM_WIDTH + 2 * STATE_LANES), _F32),
            pltpu.VMEM((CHUNK_ROWS, 2 * STATE_LANES), _F32),
        ],
        compiler_params=pltpu.CompilerParams(
            dimension_semantics=("parallel",), vmem_limit_bytes=VMEM_LIMIT),
        name="s5_chunked",
    )(uc, swap, w1, w2, a_pow)


N_ROW_INPUTS = 6


def _post_kernel(*refs, tile_fn):
    halves = []
    for h in range(2):
        group, refs = refs[:N_ROW_INPUTS + 1], refs[N_ROW_INPUTS + 1:]
        halves.append(group)
    (pprev_ref, pnext_ref, gpost1_ref, gpre2_ref, gpost2_ref, d_ref, wglu_ref, bglu_ref, cw_ref,
     wout_ref, wg_ref, wu_ref, wd_ref, swap_ref, o_ref) = refs
    tiles = [tile_fn(pl.program_id(0), h) % N_TILES for h in range(2)]
    ps = [group[5][...] for group in halves]
    edge_before = [jnp.where(tiles[0] >= 2, pprev_ref[SUBLANES - 1:SUBLANES, :], 0.0),
                   jnp.where(tiles[1] >= 2, ps[0][TILE - 1:TILE, :], 0.0)]
    edge_after = [jnp.where((tiles[0] >= 1) & (tiles[0] <= N_TILES - 2), ps[1][0:1, :], 0.0),
                  jnp.where((tiles[1] >= 1) & (tiles[1] <= N_TILES - 2), pnext_ref[0:1, :], 0.0)]
    row = lax.broadcasted_iota(jnp.int32, (TILE, CONV_WIDTH), 0)
    cw = cw_ref[...]

    def gated_input(h):
        _, _, u_ref, ys_ref, _, _ = halves[h][:N_ROW_INPUTS]
        y = d_ref[...] * u_ref[...] + _chunk_layout_to_tile(ys_ref, swap_ref)
        g = jax.nn.gelu(y)
        return g, jnp.dot(g.astype(_BF16), wglu_ref[...], preferred_element_type=_F32)

    def mixed_and_modulated(h, g, z):
        group = halves[h]
        x_ref, attn_ref, _, _, gb_ref, _ = group[:N_ROW_INPUTS]
        mods_ref = group[N_ROW_INPUTS]
        ssm = g * jax.nn.sigmoid(z + bglu_ref[...])
        p = ps[h]
        p_before = jnp.where(row == 0, edge_before[h], pltpu.roll(p, 1, 0))
        p_after = jnp.where(row == TILE - 1, edge_after[h], pltpu.roll(p, TILE - 1, 0))
        conv = gb_ref[...] * (p_before * cw[0:1, :] + p * cw[1:2, :] + p_after * cw[2:3, :])
        mix = jnp.concatenate([attn_ref[...], ssm.astype(_BF16), conv.astype(_BF16)], axis=1)
        yv = jnp.dot(mix, wout_ref[...], preferred_element_type=_F32)
        x = x_ref[...] + _gated_norm(yv, gpost1_ref[...], _mod(mods_ref, 5))
        return x, _modulated(x, gpre2_ref[...], _mod(mods_ref, 6), _mod(mods_ref, 7)).astype(_BF16)

    gz0, gz1 = gated_input(0), gated_input(1)
    (x0, h0), (x1, h1) = mixed_and_modulated(0, *gz0), mixed_and_modulated(1, *gz1)
    y0 = _swiglu(h0, wg_ref, wu_ref, wd_ref)
    y1 = _swiglu(h1, wg_ref, wu_ref, wd_ref)
    for h, (x, y) in enumerate(((x0, y0), (x1, y1))):
        gate2 = _mod(halves[h][N_ROW_INPUTS], 8)
        o_ref[h * TILE:(h + 1) * TILE, :] = x + _gated_norm(y, gpost2_ref[...], gate2, MACARON)


def _post(x_all, mods, norm_pre, norm_post, attn, u_ssm, y_ssm, gb, p, ssm_d, w_glu, b_glu, conv_w,
          w_out, wg, wu, wd, layer, latent_only):
    tile = _latent_tile if latent_only else _stream_tile
    n_pairs = BATCH * (N_TILES - 1) // 2 if latent_only else BATCH * N_TILES // 2
    halo_blocks = TILE // SUBLANES
    in_specs, args = [], []
    for h in range(2):
        for arr in (x_all, attn, u_ssm, y_ssm, gb, p):
            if arr is y_ssm:
                in_specs.append(pl.BlockSpec((SSM_GROUPS, None, TILE_CHUNKS, SSM_WIDTH),
                                             lambda p_, h=h: (0, tile(p_, h), 0, 0)))
            else:
                in_specs.append(_row_spec(arr.shape[1], tile, h))
            args.append(arr)
        in_specs.append(_mod_spec(layer, tile, h))
        args.append(mods)
    in_specs += [
        pl.BlockSpec((SUBLANES, CONV_WIDTH), lambda p_: (jnp.maximum(tile(p_, 0) * halo_blocks - 1, 0), 0)),
        pl.BlockSpec((SUBLANES, CONV_WIDTH),
                     lambda p_: (jnp.minimum((tile(p_, 1) + 1) * halo_blocks, BATCH * TOKENS // SUBLANES - 1), 0)),
        _norm_spec(layer, 1), _norm_spec(layer, 2), _norm_spec(layer, 2),
        _const_spec((None, 1, SSM_WIDTH), (layer, 0, 0)),
        _const_spec((None, SSM_WIDTH, SSM_WIDTH), (layer, 0, 0)),
        _const_spec((None, 1, SSM_WIDTH), (layer, 0, 0)),
        _const_spec((None, 3, CONV_WIDTH), (layer, 0, 0)),
        _const_spec((None, D_MODEL, D_MODEL), (layer, 0, 0)),
        *_ffn_weight_specs(layer, 1),
        _const_spec((TILE, TILE), (0, 0)),
    ]
    args += [p, p, norm_post, norm_pre, norm_post, ssm_d, w_glu, b_glu, conv_w, w_out, wg, wu, wd,
             _swap_rows_matrix()]
    return pl.pallas_call(
        functools.partial(_post_kernel, tile_fn=tile),
        grid=(n_pairs,),
        in_specs=in_specs,
        out_specs=pl.BlockSpec((2 * TILE, D_MODEL), lambda p_: (p_, 0)),
        out_shape=jax.ShapeDtypeStruct((n_pairs * 2 * TILE, D_MODEL), _F32),
        compiler_params=_TOKEN_PARAMS,
        name="mix_out_ffn",
    )(*args)


def _rope_tables():
    pos = jnp.arange(SEQ)
    row = (pos // GRID_W).astype(_F32)
    col = (pos % GRID_W).astype(_F32)
    inv_freq = ROPE_BASE ** (-jnp.arange(ROPE_PAIRS, dtype=_F32) / ROPE_PAIRS)
    ang = jnp.stack([row[:, None] * inv_freq, col[:, None] * inv_freq], axis=1)
    cos, sin = jnp.cos(ang), jnp.sin(ang)
    cos_h = jnp.concatenate([cos, cos], axis=-1).reshape(SEQ, HEAD_DIM)
    sin_h = jnp.concatenate([-sin, sin], axis=-1).reshape(SEQ, HEAD_DIM)
    cos_t = jnp.concatenate([jnp.ones((CTX_LEN, HEAD_DIM), _F32), cos_h], axis=0)
    sin_t = jnp.concatenate([jnp.zeros((CTX_LEN, HEAD_DIM), _F32), sin_h], axis=0)
    return jnp.tile(cos_t, (1, LANES // HEAD_DIM)), jnp.tile(sin_t, (1, LANES // HEAD_DIM))


def _s5_matrices(lam_re, lam_im, log_step, b_re, b_im, c_re, c_im):
    n, grp = SSM_CHUNK, SSM_GROUP
    k = jnp.arange(n + 1, dtype=_F32)
    dt = jnp.exp(log_step)[..., None, None]
    lr, li = lam_re[..., None], lam_im[..., None]
    mag = jnp.exp(k * lr * dt)
    pr, pi = mag * jnp.cos(k * li * dt), mag * jnp.sin(k * li * dt)
    ar, ai = pr[..., 1], pi[..., 1]
    den = lam_re * lam_re + lam_im * lam_im
    gr = ((ar - 1) * lam_re + ai * lam_im) / den
    gi = (ai * lam_re - (ar - 1) * lam_im) / den
    bbr = gr[..., None] * b_re - gi[..., None] * b_im
    bbi = gr[..., None] * b_im + gi[..., None] * b_re
    ct_re, ct_im = jnp.swapaxes(c_re, -1, -2), jnp.swapaxes(c_im, -1, -2)
    car = ct_re[..., None, :] * pr[..., None] - ct_im[..., None, :] * pi[..., None]
    cai = ct_re[..., None, :] * pi[..., None] + ct_im[..., None, :] * pr[..., None]
    kt = jnp.einsum('ldgpj,ldgpkh->ldgjkh', bbr, car) - jnp.einsum('ldgpj,ldgpkh->ldgjkh', bbi, cai)
    lanes = lambda m: m.reshape(m.shape[:-2] + (m.shape[-2] * m.shape[-1],))
    kf = lanes(kt[:, 0])
    kb = lanes(kt[:, 1, :, :, ::-1])
    pad = lambda m, lo, hi: jnp.pad(m, ((0, 0),) * (m.ndim - 1) + ((lo, hi),))
    intra = jnp.stack(
        [pad(kf[..., :(n - tau) * grp], tau * grp, 0)
         + pad(kb[..., (n - tau) * grp:], 0, (n - 1 - tau) * grp) for tau in range(n)],
        axis=2).reshape(DEPTH, SSM_GROUPS, n * grp, n * grp)

    def state_in(d, qr, qi):
        qr, qi = jnp.moveaxis(qr, -1, -2)[..., None, :], jnp.moveaxis(qi, -1, -2)[..., None, :]
        br, bi = jnp.swapaxes(bbr[:, d], -1, -2)[:, :, None], jnp.swapaxes(bbi[:, d], -1, -2)[:, :, None]
        rows = lambda m: m.reshape(DEPTH, SSM_GROUPS, n * grp, SSM_STATE)
        return rows(qr * br - qi * bi), rows(qr * bi + qi * br)

    f_re, f_im = state_in(0, pr[:, 0, ..., n - 1::-1], pi[:, 0, ..., n - 1::-1])
    b_re_, b_im_ = state_in(1, pr[:, 1, ..., :n], pi[:, 1, ..., :n])
    w1 = jnp.concatenate([m.astype(_BF16) for m in (intra, f_re, b_re_, f_im, b_im_)], axis=-1)

    fo_re, fo_im = lanes(car[:, 0, ..., 1:, :]), -lanes(cai[:, 0, ..., 1:, :])
    bo_re, bo_im = lanes(car[:, 1, ..., :0:-1, :]), -lanes(cai[:, 1, ..., :0:-1, :])
    w2 = jnp.concatenate([m.astype(_BF16) for m in (fo_re, bo_re, fo_im, bo_im)], axis=-2)
    a_pow = jnp.stack([jnp.concatenate([pr[:, 0, ..., n], pr[:, 1, ..., n]], axis=-1),
                       jnp.concatenate([pi[:, 0, ..., n], pi[:, 1, ..., n]], axis=-1)], axis=-2)
    return w1, w2, a_pow


def kernel(x, c, ctx, c_ctx, w_ada, b_ada, norm_pre, norm_post, ffn_w_gate, ffn_w_up, ffn_w_down,
           w_in, w_out, attn_sink, ssm_lambda_re, ssm_lambda_im, ssm_log_step, ssm_b_re, ssm_b_im,
           ssm_c_re, ssm_c_im, ssm_d, ssm_w_glu, ssm_b_glu, conv_w):
    cc = jnp.concatenate([c, c_ctx[None, :], jnp.zeros((MOD_ROWS - BATCH - 1, D_MODEL), _F32)], axis=0)
    mods = _ada_table(cc, w_ada, b_ada)
    cos_t, sin_t = _rope_tables()
    attn_bias = _attn_bias_table()
    w1, w2, a_pow = _s5_matrices(ssm_lambda_re, ssm_lambda_im, ssm_log_step,
                                 ssm_b_re, ssm_b_im, ssm_c_re, ssm_c_im)
    npre = norm_pre.reshape(DEPTH, 3, 1, D_MODEL)
    npost = norm_post.reshape(DEPTH, 3, 1, D_MODEL)
    wg, wu, wd = (w.astype(_BF16) for w in (ffn_w_gate, ffn_w_up, ffn_w_down))
    w_in_b, w_out_b, w_glu_b = (w.astype(_BF16) for w in (w_in, w_out, ssm_w_glu))
    d_skip = ssm_d.reshape(DEPTH, 1, SSM_WIDTH)
    b_glu = ssm_b_glu.reshape(DEPTH, 1, SSM_WIDTH)

    xa = (ctx, x)
    seq = lambda t: t.reshape(BATCH, TOKENS, t.shape[-1])
    for l in range(DEPTH):
        xa, q, k, v, u_ssm, u_chunks, gb, p = _pre(xa, mods, npre, npost, wg, wu, wd, w_in_b, cos_t, sin_t, l)
        attn = _attention(attn_sink[l], q, seq(k), seq(v), attn_bias)
        y_ssm = _s5_mixer(u_chunks, w1[l], w2[l], a_pow[l])
        xa = _post(xa, mods, npre, npost, attn, u_ssm, y_ssm, gb, p, d_skip, w_glu_b,
                   b_glu, conv_w, w_out_b, wg, wu, wd, l, latent_only=l == DEPTH - 1)
    return xa.reshape(BATCH, SEQ, D_MODEL)
```
